```python
import jax, jax.numpy as jnp
from jax import lax
import numpy as np

D_MODEL = 1024
BATCH = 8
SEQ = 2048
DEPTH = 4
DEC_BATCH = 128
DEC_SEQ = 4
PAST_LEN = 16384
PAGE_SIZE = 128

W_POOL = D_MODEL // 2
POOL_WINDOWS = (2, 4, 8, 16)
N_POOL_GROUPS = len(POOL_WINDOWS)
POOL_GC = W_POOL // N_POOL_GROUPS
POOL_HIST = max(POOL_WINDOWS) - 1
W_SGU = D_MODEL // 2
SGU_HEADS = 4
SGU_HC = W_SGU // SGU_HEADS
CHUNK = 128
W_CONV = D_MODEL // 2
CONV_WIDTH = 31
CONV_HIST = CONV_WIDTH - 1
N_BRANCH = 3
IN_COLS = W_POOL + 2 * W_SGU + 2 * W_CONV + N_BRANCH * D_MODEL
N_GROUPS = 4
EXPERTS_PER_GROUP = 8
N_EXPERTS = N_GROUPS * EXPERTS_PER_GROUP
TOP_K_INNER = 2
D_EXPERT = D_MODEL // 2
MOE_BLOCK = 128
EPS = 1e-6

kernel_name = 'hybrid_pool_sgu_conv_hmoe_step'


def rms_norm(x, g):
    xf = x.astype(jnp.float32)
    y = xf * lax.rsqrt(jnp.mean(xf * xf, axis=-1, keepdims=True) + EPS)
    return (y * g.astype(jnp.float32)).astype(x.dtype)


def layer_norm(x, g, b):
    xf = x.astype(jnp.float32)
    mu = jnp.mean(xf, axis=-1, keepdims=True)
    var = jnp.mean(jnp.square(xf - mu), axis=-1, keepdims=True)
    y = (xf - mu) * lax.rsqrt(var + EPS)
    return (y * g.astype(jnp.float32) + b.astype(jnp.float32)).astype(x.dtype)


def pool_branch(p, prev, pos, pool_w, pool_scale):
    bsz, t, _ = p.shape
    ext = jnp.concatenate([prev, p], axis=1)
    cs = jnp.cumsum(ext.astype(jnp.float32), axis=1)
    cs = jnp.concatenate([jnp.zeros((bsz, 1, W_POOL), jnp.float32), cs], axis=1)
    top = cs[:, POOL_HIST + 1:]
    means = []
    for gi, w in enumerate(POOL_WINDOWS):
        sl = slice(gi * POOL_GC, (gi + 1) * POOL_GC)
        lo = cs[:, POOL_HIST + 1 - w: POOL_HIST + 1 - w + t, sl]
        cnt = jnp.minimum(pos + 1, w).astype(jnp.float32)[None, :, None]
        means.append((top[..., sl] - lo) / cnt)
    mean = jnp.concatenate(means, axis=-1)
    d = (mean - p.astype(jnp.float32)).astype(p.dtype).reshape(bsz, t, N_POOL_GROUPS, POOL_GC)
    y = jnp.einsum('btgc,gcd->btgd', d, pool_w).reshape(bsz, t, W_POOL) * pool_scale
    return y, ext[:, -POOL_HIST:]


def sgu_branch(u, v, g, b, w_s, b_s):
    bsz, t, _ = v.shape
    vn = layer_norm(v, g, b)
    ln = min(t, CHUNK)
    nc = t // ln
    ws = jnp.tril(w_s[:, :ln, :ln])
    vh = vn.reshape(bsz, nc, ln, SGU_HEADS, SGU_HC)
    s = jnp.einsum('hts,bnshc->bnthc', ws, vh) + b_s[:, :ln].T[:, :, None]
    return u * s.reshape(bsz, t, W_SGU), vn


def conv_branch(a, gate, prev, conv_w, conv_b, g, b):
    c = a * jax.nn.sigmoid(gate)
    ext = jnp.concatenate([prev, c], axis=1)
    y = lax.conv_general_dilated(ext, conv_w[:, None, :], (1,), 'VALID',
                                 dimension_numbers=('NWC', 'WIO', 'NWC'),
                                 feature_group_count=W_CONV) + conv_b
    y = jax.nn.silu(layer_norm(y, g, b))
    return y, ext[:, -CONV_HIST:]


def mixer_block(xn, pool_prev, conv_prev, pos, w_in, pool_w, pool_scale, sgu_g, sgu_bn, sgu_w, sgu_b,
                conv_w, conv_b, cn_g, cn_b, wb_pool, wb_sgu, wb_conv, w_out):
    h = xn @ w_in
    o1 = W_POOL
    o2 = o1 + W_SGU
    o3 = o2 + W_SGU
    o4 = o3 + W_CONV
    o5 = o4 + W_CONV
    p, u, v, ca, cg, gl = jnp.split(h, [o1, o2, o3, o4, o5], axis=-1)
    ya, new_pool = pool_branch(p, pool_prev, pos, pool_w, pool_scale)
    yb, vn = sgu_branch(u, v, sgu_g, sgu_bn, sgu_w, sgu_b)
    yc, new_conv = conv_branch(ca, cg, conv_prev, conv_w, conv_b, cn_g, cn_b)
    ga, gb, gc = jnp.split(jax.nn.sigmoid(gl), N_BRANCH, axis=-1)
    merged = ga * (ya @ wb_pool) + gb * (yb @ wb_sgu) + gc * (yc @ wb_conv)
    return merged @ w_out, new_pool, new_conv, vn


def moe_ffn(xn, w_group, b_group, w_router, b_router, w_gate_e, w_up_e, w_down_e):
    bsz, t, d = xn.shape
    n = bsz * t
    xt = xn.reshape(n, d)
    grp_logits = (xt @ w_group).astype(jnp.float32) + b_group.astype(jnp.float32)
    grp_prob = jax.nn.softmax(grp_logits, axis=-1)
    grp = jnp.argmax(grp_prob, axis=-1).astype(jnp.int32)
    grp_w = jnp.max(grp_prob, axis=-1)
    exp_logits = ((xt @ w_router).astype(jnp.float32) + b_router.astype(jnp.float32)).reshape(
        n, N_GROUPS, EXPERTS_PER_GROUP)
    in_grp = jnp.take_along_axis(exp_logits, grp[:, None, None], axis=1)[:, 0]
    top_val, top_idx = lax.top_k(in_grp, TOP_K_INNER)
    gate = grp_w[:, None] * jax.nn.softmax(top_val, axis=-1)
    expert = grp[:, None] * EXPERTS_PER_GROUP + top_idx.astype(jnp.int32)
    a = n * TOP_K_INNER
    e_flat = expert.reshape(a)
    tok = jnp.arange(a, dtype=jnp.int32) // TOP_K_INNER
    g_flat = gate.reshape(a).astype(xn.dtype)
    order = jnp.argsort(e_flat)
    e_sorted = e_flat[order]
    counts = jnp.bincount(e_flat, length=N_EXPERTS).astype(jnp.int32)
    starts = jnp.cumsum(counts) - counts
    padded = (counts + MOE_BLOCK - 1) // MOE_BLOCK * MOE_BLOCK
    pends = jnp.cumsum(padded)
    pstarts = pends - padded
    dest = pstarts[e_sorted] + jnp.arange(a, dtype=jnp.int32) - starts[e_sorted]
    n_blocks = -(-a // MOE_BLOCK) + N_EXPERTS
    rows = n_blocks * MOE_BLOCK
    row_tok = jnp.full((rows,), n, jnp.int32).at[dest].set(tok[order])
    row_gate = jnp.zeros((rows,), xn.dtype).at[dest].set(g_flat[order])
    block_start = jnp.arange(n_blocks, dtype=jnp.int32) * MOE_BLOCK
    block_expert = jnp.minimum(jnp.searchsorted(pends, block_start, side='right'),
                               N_EXPERTS - 1).astype(jnp.int32)
    x_pad = jnp.concatenate([xt, jnp.zeros((1, d), xt.dtype)], axis=0)

    def expert_block(args):
        tok_b, gate_b, e = args
        xb = x_pad[tok_b]
        hb = jax.nn.silu(xb @ w_gate_e[e]) * (xb @ w_up_e[e])
        return (hb @ w_down_e[e]) * gate_b[:, None]

    out = lax.map(expert_block, (row_tok.reshape(n_blocks, MOE_BLOCK),
                                 row_gate.reshape(n_blocks, MOE_BLOCK), block_expert))
    y = jnp.zeros((n + 1, d), xn.dtype).at[row_tok].add(out.reshape(rows, d))
    return y[:n].reshape(bsz, t, d)


def setup_inputs(seed: int = 0) -> dict:
    key = jax.random.key(seed)
    ks = jax.random.split(key, 32)
    f32 = jnp.float32

    def nrm(k, shape, scale):
        return jax.random.normal(k, shape, f32) * scale

    L = DEPTH
    return {
        'x_prompt': nrm(ks[0], (BATCH, SEQ, D_MODEL), 1.0),
        'x_sample': nrm(ks[1], (DEC_BATCH, DEC_SEQ, D_MODEL), 1.0),
        'state_pool': nrm(ks[2], (L, DEC_BATCH, POOL_HIST, W_POOL), 1.0),
        'state_conv': nrm(ks[3], (L, DEC_BATCH, CONV_HIST, W_CONV), 0.5),
        'mix_norm_g': 1.0 + nrm(ks[4], (L, D_MODEL), 0.05),
        'w_in': nrm(ks[5], (L, D_MODEL, IN_COLS), D_MODEL ** -0.5),
        'pool_w': nrm(ks[6], (L, N_POOL_GROUPS, POOL_GC, POOL_GC), POOL_GC ** -0.5),
        'pool_scale': 1.0 + nrm(ks[7], (L, W_POOL), 0.1),
        'sgu_norm_g': 1.0 + nrm(ks[8], (L, W_SGU), 0.05),
        'sgu_norm_b': nrm(ks[9], (L, W_SGU), 0.02),
        'sgu_w': nrm(ks[10], (L, SGU_HEADS, CHUNK, CHUNK), CHUNK ** -0.5),
        'sgu_b': 1.0 + nrm(ks[11], (L, SGU_HEADS, CHUNK), 0.1),
        'conv_w': nrm(ks[12], (L, CONV_WIDTH, W_CONV), CONV_WIDTH ** -0.5),
        'conv_b': nrm(ks[13], (L, W_CONV), 0.02),
        'conv_norm_g': 1.0 + nrm(ks[14], (L, W_CONV), 0.05),
        'conv_norm_b': nrm(ks[15], (L, W_CONV), 0.02),
        'w_branch_pool': nrm(ks[16], (L, W_POOL, D_MODEL), W_POOL ** -0.5),
        'w_branch_sgu': nrm(ks[17], (L, W_SGU, D_MODEL), W_SGU ** -0.5),
        'w_branch_conv': nrm(ks[18], (L, W_CONV, D_MODEL), W_CONV ** -0.5),
        'w_out': nrm(ks[19], (L, D_MODEL, D_MODEL), D_MODEL ** -0.5),
        'ffn_norm_g': 1.0 + nrm(ks[20], (L, D_MODEL), 0.05),
        'w_group': nrm(ks[21], (L, D_MODEL, N_GROUPS), D_MODEL ** -0.5),
        'b_group': nrm(ks[22], (L, N_GROUPS), 0.01),
        'w_router': nrm(ks[23], (L, D_MODEL, N_EXPERTS), D_MODEL ** -0.5),
        'b_router': nrm(ks[24], (L, N_EXPERTS), 0.01),
        'w_gate_e': nrm(ks[25], (L, N_EXPERTS, D_MODEL, D_EXPERT), D_MODEL ** -0.5),
        'w_up_e': nrm(ks[26], (L, N_EXPERTS, D_MODEL, D_EXPERT), D_MODEL ** -0.5),
        'w_down_e': nrm(ks[27], (L, N_EXPERTS, D_EXPERT, D_MODEL), D_EXPERT ** -0.5),
        'final_norm_g': 1.0 + nrm(ks[28], (D_MODEL,), 0.05),
    }


def reference(x_prompt, x_sample, state_pool, state_conv, mix_norm_g, w_in, pool_w, pool_scale,
              sgu_norm_g, sgu_norm_b, sgu_w, sgu_b, conv_w, conv_b, conv_norm_g, conv_norm_b,
              w_branch_pool, w_branch_sgu, w_branch_conv, w_out, ffn_norm_g, w_group, b_group,
              w_router, b_router, w_gate_e, w_up_e, w_down_e, final_norm_g):
    bp, tp, _ = x_prompt.shape
    bs, ts, _ = x_sample.shape
    pos_p = jnp.arange(tp, dtype=jnp.int32)
    pos_s = PAST_LEN + jnp.arange(ts, dtype=jnp.int32)
    pool_prev_p = jnp.zeros((bp, POOL_HIST, W_POOL), x_prompt.dtype)
    conv_prev_p = jnp.zeros((bp, CONV_HIST, W_CONV), x_prompt.dtype)
    xp, xs = x_prompt, x_sample
    pool_p, pool_s, conv_p, conv_s, chunk_v = [], [], [], [], []
    for l in range(DEPTH):
        mix_w = (w_in[l], pool_w[l], pool_scale[l], sgu_norm_g[l], sgu_norm_b[l], sgu_w[l], sgu_b[l],
                 conv_w[l], conv_b[l], conv_norm_g[l], conv_norm_b[l], w_branch_pool[l],
                 w_branch_sgu[l], w_branch_conv[l], w_out[l])
        moe_w = (w_group[l], b_group[l], w_router[l], b_router[l], w_gate_e[l], w_up_e[l], w_down_e[l])
        m, npool, nconv, _ = mixer_block(rms_norm(xp, mix_norm_g[l]), pool_prev_p, conv_prev_p, pos_p, *mix_w)
        xp = xp + m
        xp = xp + moe_ffn(rms_norm(xp, ffn_norm_g[l]), *moe_w)
        pool_p.append(npool)
        conv_p.append(nconv)
        m, npool, nconv, vn = mixer_block(rms_norm(xs, mix_norm_g[l]), state_pool[l], state_conv[l], pos_s, *mix_w)
        xs = xs + m
        xs = xs + moe_ffn(rms_norm(xs, ffn_norm_g[l]), *moe_w)
        pool_s.append(npool)
        conv_s.append(nconv)
        chunk_v.append(vn)
    y_prompt = rms_norm(xp, final_norm_g)
    y_sample = rms_norm(xs, final_norm_g)
    return (y_prompt, y_sample, jnp.stack(pool_p), jnp.stack(pool_s), jnp.stack(conv_p),
            jnp.stack(conv_s), jnp.stack(chunk_v))
```

```python
import functools

import jax
import jax.numpy as jnp
from jax import lax
from jax.experimental import pallas as pl
from jax.experimental.pallas import tpu as pltpu

D_MODEL = 1024
DEPTH = 4
PAST_LEN = 16384
W_BR = 512
POOL_WINDOWS = (2, 4, 8, 16)
POOL_GC = 128
POOL_HIST = 15
SGU_HEADS = 4
SGU_HC = 128
CHUNK = 128
CONV_WIDTH = 31
CONV_HIST = 30
N_GROUPS = 4
EPG = 8
N_EXPERTS = 32
D_EXPERT = 512
MOE_BLOCK = 128
EPS = 1e-6

LANES = 128
SUBLANES = 8
VMEM_LIMIT = 56 * 1024 * 1024

TQ = 256
ROUTE_COLS = 8
EXP_LANE0 = N_GROUPS
CONV_RB = 64
POOL_PAD = 16
CONV_PAD = 32

_MXU = jnp.bfloat16
_F32 = jnp.float32


def _dot(a, b):
    return jnp.dot(a.astype(_MXU), b.astype(_MXU), preferred_element_type=_F32)


def _rms_norm(x, g):
    return x * lax.rsqrt(jnp.mean(x * x, axis=-1, keepdims=True) + EPS) * g


def _layer_norm(x, g, b):
    mu = jnp.mean(x, axis=-1, keepdims=True)
    xc = x - mu
    var = jnp.mean(xc * xc, axis=-1, keepdims=True)
    return xc * lax.rsqrt(var + EPS) * g + b


def _const_spec(shape):
    nd = len(shape)
    return pl.BlockSpec(shape, lambda *_: (0,) * nd, pipeline_mode=pl.Buffered(1))


def _tile_rows_to_tokens(ref, start, n, stride):
    return jnp.concatenate([ref[pl.ds(start + j, n, stride=stride), :] for j in range(SUBLANES)], axis=-1)


def _mixer_input(in_refs, combine, n):
    if not combine:
        return in_refs[0][...]
    h_ref, o2_ref, rprev_ref = in_refs
    g0 = rprev_ref[:, 4:5]
    g1 = rprev_ref[:, 5:6]
    o0 = _tile_rows_to_tokens(o2_ref, 0, n, 2 * SUBLANES)
    o1 = _tile_rows_to_tokens(o2_ref, SUBLANES, n, 2 * SUBLANES)
    return h_ref[...] + (g0 * o0 + g1 * o1)


def _route(hn, wrt_ref, brt_ref, ltri_ref, cnt_acc, route_ref):
    n = hn.shape[0]
    logits = _dot(hn, wrt_ref[...]) + brt_ref[...]
    lane = lax.broadcasted_iota(jnp.int32, (n, LANES), 1).astype(_F32)
    neg = jnp.float32(-jnp.inf)
    is_grp = lane < N_GROUPS
    gl = jnp.where(is_grp, logits, neg)
    gmax = jnp.max(gl, axis=-1, keepdims=True)
    grp = jnp.min(jnp.where(gl == gmax, lane, float(LANES)), axis=-1, keepdims=True)
    gsum = jnp.sum(jnp.where(is_grp, jnp.exp(logits - gmax), 0.0), axis=-1, keepdims=True)
    grp_w = 1.0 / gsum
    lo = EXP_LANE0 + grp * EPG
    el = jnp.where((lane >= lo) & (lane < lo + EPG), logits, neg)
    v0 = jnp.max(el, axis=-1, keepdims=True)
    i0 = jnp.min(jnp.where(el == v0, lane, float(LANES)), axis=-1, keepdims=True)
    el1 = jnp.where(lane == i0, neg, el)
    v1 = jnp.max(el1, axis=-1, keepdims=True)
    i1 = jnp.min(jnp.where(el1 == v1, lane, float(LANES)), axis=-1, keepdims=True)
    ex = jnp.exp(v1 - v0)
    den = 1.0 + ex
    g0 = grp_w / den
    g1 = grp_w * ex / den
    oh0 = lane == i0
    oh1 = lane == i1
    oh = jnp.where(oh0 | oh1, 1.0, 0.0)
    before = _dot(ltri_ref[...], oh) + cnt_acc[...]
    r0 = jnp.sum(jnp.where(oh0, before, 0.0), axis=-1, keepdims=True)
    r1 = jnp.sum(jnp.where(oh1, before, 0.0), axis=-1, keepdims=True)
    cnt_acc[...] = cnt_acc[...] + jnp.sum(oh, axis=0, keepdims=True)
    route_ref[:, 0:1] = i0 - EXP_LANE0
    route_ref[:, 1:2] = i1 - EXP_LANE0
    route_ref[:, 2:3] = r0
    route_ref[:, 3:4] = r1
    route_ref[:, 4:5] = g0
    route_ref[:, 5:6] = g1
    route_ref[:, 6:8] = jnp.zeros((n, 2), _F32)


def _mixer_tail(x, xn, ya, yb, yc, win_ref, wbp_ref, wbs_ref, wbc_ref, wout_ref, gffn_ref,
                wrt_ref, brt_ref, ltri_ref, cnt_acc, hout_ref, hn_ref, route_ref):
    n = x.shape[0]
    o5 = 5 * W_BR
    ga = jax.nn.sigmoid(_dot(xn, win_ref[:, o5:o5 + D_MODEL]))
    merged = ga * _dot(ya, wbp_ref[...])
    gb = jax.nn.sigmoid(_dot(xn, win_ref[:, o5 + D_MODEL:o5 + 2 * D_MODEL]))
    merged = merged + gb * _dot(yb, wbs_ref[...])
    gc = jax.nn.sigmoid(_dot(xn, win_ref[:, o5 + 2 * D_MODEL:o5 + 3 * D_MODEL]))
    merged = merged + gc * _dot(yc, wbc_ref[...])
    h = x + _dot(merged, wout_ref[...])
    hout_ref[...] = h
    hn = _rms_norm(h, gffn_ref[...])
    for j in range(SUBLANES):
        hn_ref[pl.ds(j, n, stride=SUBLANES), :] = hn[:, j * LANES:(j + 1) * LANES]
    _route(hn, wrt_ref, brt_ref, ltri_ref, cnt_acc, route_ref)


def _mixer_prompt_kernel(*refs, combine, nt, n_main):
    n_in = 3 if combine else 1
    in_refs = refs[:n_in]
    hs_ref, hns_ref, routes_ref = refs[n_in:n_in + 3]
    hout_ref, hn_ref, route_ref = refs[-10:-7]
    step = pl.program_id(0)

    @pl.when(step >= n_main)
    def _():
        hout_ref[...] = hs_ref[...]
        hn_ref[...] = hns_ref[...]
        route_ref[...] = routes_ref[...]

    @pl.when(step < n_main)
    def _():
        _mixer_prompt_step(in_refs, refs[n_in + 3:], combine, nt)


def _mixer_prompt_step(in_refs, refs, combine, nt):
    (cnt_in_ref, gmix_ref, win_ref, poolw_ref, pscale_ref, sgug_ref, sgub_ref, sguw_ref, sgubias_ref,
     convw_ref, convb_ref, cng_ref, cnb_ref, wbp_ref, wbs_ref, wbc_ref, wout_ref,
     gffn_ref, wrt_ref, brt_ref, ltri_ref,
     hout_ref, hn_ref, route_ref, cnt_ref, pstate_ref, cstate_ref,
     pbuf, cbuf, ybuf, cnt_acc) = refs
    step = pl.program_id(0)
    i = lax.rem(step, nt)

    @pl.when(step == 0)
    def _():
        cnt_acc[...] = cnt_in_ref[...]

    @pl.when(i == 0)
    def _():
        pbuf[0:POOL_PAD, :] = jnp.zeros((POOL_PAD, W_BR), _F32)
        cbuf[0:CONV_PAD, :] = jnp.zeros((CONV_PAD, W_BR), _F32)

    x = _mixer_input(in_refs, combine, TQ)
    xn = _rms_norm(x, gmix_ref[...]).astype(_MXU)

    p = _dot(xn, win_ref[:, 0:W_BR])
    pbuf[POOL_PAD:POOL_PAD + TQ, :] = p
    pos = i * TQ + lax.broadcasted_iota(jnp.int32, (TQ, 1), 0)
    ya_parts = []
    for gi, w in enumerate(POOL_WINDOWS):
        sl = slice(gi * POOL_GC, (gi + 1) * POOL_GC)
        acc = p[:, sl]
        for j in range(1, w):
            acc = acc + pbuf[POOL_PAD - j:POOL_PAD - j + TQ, sl]
        cnt = jnp.minimum(pos + 1, w).astype(_F32)
        d = acc / cnt - p[:, sl]
        ya_parts.append(_dot(d, poolw_ref[gi]))
    ya = jnp.concatenate(ya_parts, axis=-1) * pscale_ref[...]

    @pl.when(i == nt - 1)
    def _():
        pstate_ref[0] = pbuf[POOL_PAD + TQ - POOL_HIST:POOL_PAD + TQ, :]

    pbuf[0:POOL_PAD, :] = pbuf[TQ:TQ + POOL_PAD, :]

    u = _dot(xn, win_ref[:, W_BR:2 * W_BR])
    v = _dot(xn, win_ref[:, 2 * W_BR:3 * W_BR])
    vn = _layer_norm(v, sgug_ref[...], sgub_ref[...])
    rows = []
    for c in range(TQ // CHUNK):
        heads = [_dot(sguw_ref[hd], vn[c * CHUNK:(c + 1) * CHUNK, hd * SGU_HC:(hd + 1) * SGU_HC])
                 for hd in range(SGU_HEADS)]
        rows.append(jnp.concatenate(heads, axis=-1) + sgubias_ref[...])
    yb = u * jnp.concatenate(rows, axis=0)

    ca = _dot(xn, win_ref[:, 3 * W_BR:4 * W_BR])
    cg = _dot(xn, win_ref[:, 4 * W_BR:5 * W_BR])
    cbuf[CONV_PAD:CONV_PAD + TQ, :] = ca * jax.nn.sigmoid(cg)
    for lb in range(W_BR // LANES):
        sl = slice(lb * LANES, (lb + 1) * LANES)
        for rb in range(TQ // CONV_RB):
            r0 = CONV_PAD + rb * CONV_RB
            acc = jnp.broadcast_to(convb_ref[:, sl], (CONV_RB, LANES))
            for j in range(CONV_WIDTH):
                acc = acc + cbuf[r0 - j:r0 - j + CONV_RB, sl] * convw_ref[CONV_HIST - j:CONV_WIDTH - j, sl]
            ybuf[rb * CONV_RB:(rb + 1) * CONV_RB, sl] = acc
    yc = jax.nn.silu(_layer_norm(ybuf[...], cng_ref[...], cnb_ref[...]))

    @pl.when(i == nt - 1)
    def _():
        cstate_ref[0] = cbuf[CONV_PAD + TQ - CONV_HIST:CONV_PAD + TQ, :]

    cbuf[0:CONV_PAD, :] = cbuf[TQ:TQ + CONV_PAD, :]

    _mixer_tail(x, xn, ya, yb, yc, win_ref, wbp_ref, wbs_ref, wbc_ref, wout_ref, gffn_ref,
                wrt_ref, brt_ref, ltri_ref, cnt_acc, hout_ref, hn_ref, route_ref)
    cnt_ref[...] = cnt_acc[...]


def _mixer_sample_kernel(*refs, combine, nseq, nstep):
    n_in = 3 if combine else 1
    in_refs = refs[:n_in]
    (sp_ref, sc_ref, gmix_ref, win_ref, poolw_ref, pscale_ref, sgug_ref, sgub_ref, sguw_ref, sgubias_ref,
     convw_ref, convb_ref, cng_ref, cnb_ref, wbp_ref, wbs_ref, wbc_ref, wout_ref,
     gffn_ref, wrt_ref, brt_ref, ltri_ref,
     hout_ref, hn_ref, route_ref, cnt_ref, pnew_ref, cnew_ref, vn_ref,
     ybuf, cnt_acc) = refs[n_in:]
    n = nseq * nstep
    cnt_acc[...] = jnp.zeros_like(cnt_acc)
    x = _mixer_input(in_refs, combine, n)
    xn = _rms_norm(x, gmix_ref[...]).astype(_MXU)

    def slab(a, t):
        return a[t * nseq:(t + 1) * nseq]

    p = _dot(xn, win_ref[:, 0:W_BR])
    pnew_ref[...] = p
    ya_rows = []
    for t in range(nstep):
        parts = []
        for gi, w in enumerate(POOL_WINDOWS):
            sl = slice(gi * POOL_GC, (gi + 1) * POOL_GC)
            acc = slab(p, t)[:, sl]
            for j in range(1, w):
                k = POOL_HIST + t - j
                acc = acc + (sp_ref[k, :, sl] if k < POOL_HIST else slab(p, k - POOL_HIST)[:, sl])
            cnt = float(min(PAST_LEN + t + 1, w))
            d = acc / cnt - slab(p, t)[:, sl]
            parts.append(_dot(d, poolw_ref[gi]))
        ya_rows.append(jnp.concatenate(parts, axis=-1))
    ya = jnp.concatenate(ya_rows, axis=0) * pscale_ref[...]

    u = _dot(xn, win_ref[:, W_BR:2 * W_BR])
    v = _dot(xn, win_ref[:, 2 * W_BR:3 * W_BR])
    vn = _layer_norm(v, sgug_ref[...], sgub_ref[...])
    vn_ref[...] = vn
    s_rows = []
    for t in range(nstep):
        acc = jnp.broadcast_to(sgubias_ref[t:t + 1, :], (nseq, W_BR))
        for s in range(t + 1):
            acc = acc + sguw_ref[t * nstep + s:t * nstep + s + 1, :] * slab(vn, s)
        s_rows.append(acc)
    yb = u * jnp.concatenate(s_rows, axis=0)

    ca = _dot(xn, win_ref[:, 3 * W_BR:4 * W_BR])
    cg = _dot(xn, win_ref[:, 4 * W_BR:5 * W_BR])
    cnew_ref[...] = ca * jax.nn.sigmoid(cg)
    for t in range(nstep):
        for lb in range(W_BR // LANES):
            sl = slice(lb * LANES, (lb + 1) * LANES)
            acc = jnp.broadcast_to(convb_ref[:, sl], (nseq, LANES))
            for kw in range(CONV_WIDTH):
                k = t + kw
                row = sc_ref[k, :, sl] if k < CONV_HIST else cnew_ref[(k - CONV_HIST) * nseq:(k - CONV_HIST + 1) * nseq, sl]
                acc = acc + row * convw_ref[kw:kw + 1, sl]
            ybuf[t * nseq:(t + 1) * nseq, sl] = acc
    yc = jax.nn.silu(_layer_norm(ybuf[...], cng_ref[...], cnb_ref[...]))

    _mixer_tail(x, xn, ya, yb, yc, win_ref, wbp_ref, wbs_ref, wbc_ref, wout_ref, gffn_ref,
                wrt_ref, brt_ref, ltri_ref, cnt_acc, hout_ref, hn_ref, route_ref)
    cnt_ref[...] = cnt_acc[...]


def _lane_to_sublane(row):
    sub = lax.broadcasted_iota(jnp.int32, (LANES, LANES), 0)
    lane = lax.broadcasted_iota(jnp.int32, (LANES, LANES), 1)
    return jnp.sum(jnp.where(sub == lane, jnp.broadcast_to(row, (LANES, LANES)), 0.0), axis=1, keepdims=True)


def _plan_kernel(route_ref, cnt_ref, pos_ref, binfo_ref, *, nblk_lanes):
    cnt = cnt_ref[...]
    padded = jnp.floor((cnt + (MOE_BLOCK - 1)) * (1.0 / MOE_BLOCK)) * MOE_BLOCK
    lane1 = lax.broadcasted_iota(jnp.int32, (SUBLANES, LANES), 1)
    pends = jnp.broadcast_to(padded, (SUBLANES, LANES))
    sh = 1
    while sh < LANES:
        pends = pends + jnp.where(lane1 >= sh, pltpu.roll(pends, sh, axis=1), 0.0)
        sh *= 2
    pends = pends[0:1]
    pstart = pends - padded

    n = route_ref.shape[0]
    lane = lax.broadcasted_iota(jnp.int32, (n, LANES), 1).astype(_F32)
    for k in range(2):
        e_lane = route_ref[:, k:k + 1] + EXP_LANE0
        base = jnp.sum(jnp.where(lane == e_lane, pstart, 0.0), axis=-1, keepdims=True)
        pos_ref[:, k:k + 1] = (base + route_ref[:, 2 + k:3 + k]).astype(jnp.int32)

    @pl.when(pl.program_id(0) == 0)
    def _():
        pends_c = _lane_to_sublane(pends)
        pstart_c = _lane_to_sublane(pstart)
        cnt_c = _lane_to_sublane(cnt)
        sub = lax.broadcasted_iota(jnp.int32, (LANES, LANES), 0)
        is_exp = (sub >= EXP_LANE0) & (sub < EXP_LANE0 + N_EXPERTS)
        binfo_ref[...] = jnp.zeros_like(binfo_ref)
        for q in range(nblk_lanes // LANES):
            bstart = ((lax.broadcasted_iota(jnp.int32, (LANES, LANES), 1) + q * LANES) * MOE_BLOCK).astype(_F32)
            done = jnp.where(is_exp & (pends_c <= bstart), 1.0, 0.0)
            bexp = jnp.minimum(jnp.sum(done, axis=0, keepdims=True), float(N_EXPERTS - 1))
            inside = is_exp & (pstart_c <= bstart) & (bstart < pends_c)
            valid = jnp.clip(pstart_c + cnt_c - bstart, 0.0, float(MOE_BLOCK))
            nval = jnp.sum(jnp.where(inside, valid, 0.0), axis=0, keepdims=True)
            binfo_ref[0:1, q * LANES:(q + 1) * LANES] = bexp.astype(jnp.int32)
            binfo_ref[1:2, q * LANES:(q + 1) * LANES] = nval.astype(jnp.int32)


def _rows_kernel(pos0_ref, pos1_ref, rt2_ref, *, n_tok, n_rows):
    def fill(r, c):
        rt2_ref[r] = 2 * n_tok
        return c
    lax.fori_loop(0, n_rows, fill, 0)

    def put(t, c):
        rt2_ref[pos0_ref[t]] = 2 * t
        rt2_ref[pos1_ref[t]] = 2 * t + 1
        return c
    lax.fori_loop(0, n_tok, put, 0)


def _moe_kernel(be_ref, nv_ref, rt2_ref, hn_hbm, wg_ref, wu_ref, wd_ref, o2_hbm,
                xbuf, obuf, wg_bf, wu_bf, wd_bf, gsem, ssem):
    b = pl.program_id(0)
    nblk = pl.num_programs(0)
    slot = b % 2
    tile = SUBLANES

    def gather_copy(tok, row, s):
        return pltpu.make_async_copy(hn_hbm.at[pl.ds(pl.multiple_of(tok * tile, tile), tile)],
                                     xbuf.at[s, pl.ds(pl.multiple_of(row * tile, tile), tile)], gsem.at[s])

    def scatter_copy(row, dst, s):
        return pltpu.make_async_copy(obuf.at[s, pl.ds(pl.multiple_of(row * tile, tile), tile)],
                                     o2_hbm.at[pl.ds(pl.multiple_of(dst * tile, tile), tile)], ssem.at[s])

    def start_gather(blk, s):
        def body(r, c):
            gather_copy(jnp.right_shift(rt2_ref[blk * MOE_BLOCK + r], 1), r, s).start()
            return c
        lax.fori_loop(0, nv_ref[blk], body, 0)

    def wait_gather(blk, s):
        def body(r, c):
            gather_copy(0, 0, s).wait()
            return c
        lax.fori_loop(0, nv_ref[blk], body, 0)

    def start_scatter(blk, s):
        def body(r, c):
            scatter_copy(r, rt2_ref[blk * MOE_BLOCK + r], s).start()
            return c
        lax.fori_loop(0, nv_ref[blk], body, 0)

    def wait_scatter(blk, s):
        def body(r, c):
            scatter_copy(0, 0, s).wait()
            return c
        lax.fori_loop(0, nv_ref[blk], body, 0)

    @pl.when(b == 0)
    def _():
        xbuf[...] = jnp.zeros_like(xbuf)
        start_gather(0, 0)

    @pl.when(b + 1 < nblk)
    def _():
        start_gather(b + 1, 1 - slot)

    @pl.when(b >= 2)
    def _():
        wait_scatter(b - 2, slot)

    @pl.when(nv_ref[b] > 0)
    def _():
        @pl.when((b == 0) | (be_ref[b] != be_ref[jnp.maximum(b - 1, 0)]))
        def _():
            wg_bf[...] = wg_ref[0].astype(_MXU)
            wu_bf[...] = wu_ref[0].astype(_MXU)
            wd_bf[...] = wd_ref[0].astype(_MXU)

        wait_gather(b, slot)
        x = _tile_rows_to_tokens(xbuf.at[slot], 0, MOE_BLOCK, SUBLANES).astype(_MXU)
        hb = jax.nn.silu(_dot(x, wg_bf[...])) * _dot(x, wu_bf[...])
        o = _dot(hb, wd_bf[...])
        for j in range(SUBLANES):
            obuf[slot, pl.ds(j, MOE_BLOCK, stride=SUBLANES), :] = o[:, j * LANES:(j + 1) * LANES]
        start_scatter(b, slot)

    @pl.when(b == nblk - 1)
    def _():
        @pl.when(b >= 1)
        def _():
            wait_scatter(b - 1, 1 - slot)
        wait_scatter(b, slot)


def _final_kernel(h_ref, o2_ref, rprev_ref, g_ref, y_ref):
    n = h_ref.shape[0]
    y_ref[...] = _rms_norm(_mixer_input((h_ref, o2_ref, rprev_ref), True, n), g_ref[...])


def _layer_weights(l, w):
    nstep = w['nstep']
    sgu_w = w['sgu_w'][l]
    sgu_b = w['sgu_b'][l]
    w_rt = jnp.zeros((D_MODEL, LANES), _F32)
    w_rt = w_rt.at[:, 0:N_GROUPS].set(w['w_group'][l]).at[:, EXP_LANE0:EXP_LANE0 + N_EXPERTS].set(w['w_router'][l])
    b_rt = jnp.zeros((1, LANES), _F32)
    b_rt = b_rt.at[0, 0:N_GROUPS].set(w['b_group'][l]).at[0, EXP_LANE0:EXP_LANE0 + N_EXPERTS].set(w['b_router'][l])
    common_head = (w['mix_norm_g'][l][None], w['w_in'][l].astype(_MXU), w['pool_w'][l].astype(_MXU),
                   w['pool_scale'][l][None], w['sgu_norm_g'][l][None], w['sgu_norm_b'][l][None])
    common_tail = (w['conv_w'][l], w['conv_b'][l][None], w['conv_norm_g'][l][None], w['conv_norm_b'][l][None],
                   w['w_branch_pool'][l].astype(_MXU), w['w_branch_sgu'][l].astype(_MXU),
                   w['w_branch_conv'][l].astype(_MXU), w['w_out'][l].astype(_MXU),
                   w['ffn_norm_g'][l][None], w_rt.astype(_MXU), b_rt)
    sgu_p = (jnp.tril(sgu_w).astype(_MXU), jnp.repeat(sgu_b.T, SGU_HC, axis=1))
    corner = jnp.tril(sgu_w[:, :nstep, :nstep])
    sgu_s = (jnp.repeat(corner.transpose(1, 2, 0).reshape(nstep * nstep, SGU_HEADS), SGU_HC, axis=1),
             jnp.repeat(sgu_b[:, :nstep].T, SGU_HC, axis=1))
    return common_head + sgu_p + common_tail, common_head + sgu_s + common_tail


def _weight_specs(ops):
    return [_const_spec(o.shape) for o in ops]


def _ltri(n):
    r = lax.broadcasted_iota(jnp.int32, (n, n), 0)
    c = lax.broadcasted_iota(jnp.int32, (n, n), 1)
    return (c < r).astype(_MXU)


def _mixer_prompt(inputs, sample, ops, *, nb, seq, n_all):
    combine = len(inputs) == 3
    nt = seq // TQ
    n_main = nb * nt
    n_steps = n_all // TQ
    h_s, hn_s, route_s, cnt_s = sample
    row = lambda s: (s, 0)
    main_row = lambda s: (jnp.minimum(s, n_main - 1), 0)
    tail_row = lambda s: (jnp.maximum(s - n_main, 0), 0)
    seq_blk = lambda s: (jnp.minimum(s // nt, nb - 1), 0, 0)
    if combine:
        in_specs = [pl.BlockSpec((TQ, D_MODEL), main_row), pl.BlockSpec((TQ * 2 * SUBLANES, LANES), main_row),
                    pl.BlockSpec((TQ, ROUTE_COLS), main_row)]
    else:
        in_specs = [pl.BlockSpec((TQ, D_MODEL), main_row)]
    in_specs += [pl.BlockSpec((TQ, D_MODEL), tail_row), pl.BlockSpec((TQ * SUBLANES, LANES), tail_row),
                 pl.BlockSpec((TQ, ROUTE_COLS), tail_row), _const_spec(cnt_s.shape)]
    ltri = _ltri(TQ)
    out_shape = (jax.ShapeDtypeStruct((n_all, D_MODEL), _F32),
                 jax.ShapeDtypeStruct((n_all * SUBLANES, LANES), _F32),
                 jax.ShapeDtypeStruct((n_all, ROUTE_COLS), _F32),
                 jax.ShapeDtypeStruct((1, LANES), _F32),
                 jax.ShapeDtypeStruct((nb, POOL_HIST, W_BR), _F32),
                 jax.ShapeDtypeStruct((nb, CONV_HIST, W_BR), _F32))
    out_specs = (pl.BlockSpec((TQ, D_MODEL), row), pl.BlockSpec((TQ * SUBLANES, LANES), row),
                 pl.BlockSpec((TQ, ROUTE_COLS), row), pl.BlockSpec((1, LANES), lambda s: (0, 0)),
                 pl.BlockSpec((1, POOL_HIST, W_BR), seq_blk), pl.BlockSpec((1, CONV_HIST, W_BR), seq_blk))
    return pl.pallas_call(
        functools.partial(_mixer_prompt_kernel, combine=combine, nt=nt, n_main=n_main),
        grid=(n_steps,),
        in_specs=in_specs + _weight_specs(ops) + [_const_spec(ltri.shape)],
        out_specs=out_specs,
        out_shape=out_shape,
        scratch_shapes=[pltpu.VMEM((POOL_PAD + TQ, W_BR), _F32), pltpu.VMEM((CONV_PAD + TQ, W_BR), _F32),
                        pltpu.VMEM((TQ, W_BR), _F32), pltpu.VMEM((1, LANES), _F32)],
        compiler_params=pltpu.CompilerParams(dimension_semantics=("arbitrary",), vmem_limit_bytes=VMEM_LIMIT),
        name="mixer_prompt",
    )(*inputs, h_s, hn_s, route_s, cnt_s, *ops, ltri)


def _mixer_sample(inputs, sp, sc, ops, *, nseq, nstep, n_prompt):
    combine = len(inputs) == 3
    n = nseq * nstep
    blk = n_prompt // n
    row = lambda i: (blk, 0)
    zero = lambda i: (0, 0)
    if combine:
        in_specs = [pl.BlockSpec((n, D_MODEL), row), pl.BlockSpec((n * 2 * SUBLANES, LANES), row),
                    pl.BlockSpec((n, ROUTE_COLS), row)]
    else:
        in_specs = [pl.BlockSpec((n, D_MODEL), zero)]
    in_specs += [_const_spec(sp.shape), _const_spec(sc.shape)]
    ltri = _ltri(n)
    out_shape = (jax.ShapeDtypeStruct((n, D_MODEL), _F32), jax.ShapeDtypeStruct((n * SUBLANES, LANES), _F32),
                 jax.ShapeDtypeStruct((n, ROUTE_COLS), _F32), jax.ShapeDtypeStruct((1, LANES), _F32),
                 jax.ShapeDtypeStruct((n, W_BR), _F32), jax.ShapeDtypeStruct((n, W_BR), _F32),
                 jax.ShapeDtypeStruct((n, W_BR), _F32))
    out_specs = (pl.BlockSpec((n, D_MODEL), zero), pl.BlockSpec((n * SUBLANES, LANES), zero),
                 pl.BlockSpec((n, ROUTE_COLS), zero), pl.BlockSpec((1, LANES), zero),
                 pl.BlockSpec((n, W_BR), zero), pl.BlockSpec((n, W_BR), zero), pl.BlockSpec((n, W_BR), zero))
    return pl.pallas_call(
        functools.partial(_mixer_sample_kernel, combine=combine, nseq=nseq, nstep=nstep),
        grid=(1,),
        in_specs=in_specs + _weight_specs(ops) + [_const_spec(ltri.shape)],
        out_specs=out_specs,
        out_shape=out_shape,
        scratch_shapes=[pltpu.VMEM((n, W_BR), _F32), pltpu.VMEM((1, LANES), _F32)],
        compiler_params=pltpu.CompilerParams(dimension_semantics=("arbitrary",), vmem_limit_bytes=VMEM_LIMIT),
        name="mixer_sample",
    )(*inputs, sp, sc, *ops, ltri)


def _plan(route, cnt, *, n_blocks):
    n_all = route.shape[0]
    tile = 512
    nblk_lanes = -(-n_blocks // LANES) * LANES
    return pl.pallas_call(
        functools.partial(_plan_kernel, nblk_lanes=nblk_lanes),
        grid=(n_all // tile,),
        in_specs=[pl.BlockSpec((tile, ROUTE_COLS), lambda i: (i, 0)), pl.BlockSpec((1, LANES), lambda i: (0, 0))],
        out_specs=(pl.BlockSpec((tile, 2), lambda i: (i, 0)), pl.BlockSpec((SUBLANES, nblk_lanes), lambda i: (0, 0))),
        out_shape=(jax.ShapeDtypeStruct((n_all, 2), jnp.int32), jax.ShapeDtypeStruct((SUBLANES, nblk_lanes), jnp.int32)),
        compiler_params=pltpu.CompilerParams(dimension_semantics=("arbitrary",)),
        name="moe_plan",
    )(route, cnt)


def _rows(pos0, pos1, *, n_rows):
    n_tok = pos0.shape[0]
    smem = pl.BlockSpec(memory_space=pltpu.SMEM)
    return pl.pallas_call(
        functools.partial(_rows_kernel, n_tok=n_tok, n_rows=n_rows),
        in_specs=[smem, smem],
        out_specs=smem,
        out_shape=jax.ShapeDtypeStruct((n_rows,), jnp.int32),
        name="moe_rows",
    )(pos0, pos1)


def _moe(be, nv, rt2, hn_all, wg, wu, wd, *, n_blocks):
    n_all = hn_all.shape[0] // SUBLANES
    wspec_in = pl.BlockSpec((1, D_MODEL, D_EXPERT), lambda b, be, nv, rt: (be[b], 0, 0))
    wspec_out = pl.BlockSpec((1, D_EXPERT, D_MODEL), lambda b, be, nv, rt: (be[b], 0, 0))
    grid_spec = pltpu.PrefetchScalarGridSpec(
        num_scalar_prefetch=3,
        grid=(n_blocks,),
        in_specs=[pl.BlockSpec(memory_space=pl.ANY), wspec_in, wspec_in, wspec_out],
        out_specs=pl.BlockSpec(memory_space=pl.ANY),
        scratch_shapes=[pltpu.VMEM((2, MOE_BLOCK * SUBLANES, LANES), _F32),
                        pltpu.VMEM((2, MOE_BLOCK * SUBLANES, LANES), _F32),
                        pltpu.VMEM((D_MODEL, D_EXPERT), _MXU), pltpu.VMEM((D_MODEL, D_EXPERT), _MXU),
                        pltpu.VMEM((D_EXPERT, D_MODEL), _MXU),
                        pltpu.SemaphoreType.DMA((2,)), pltpu.SemaphoreType.DMA((2,))])
    return pl.pallas_call(
        _moe_kernel,
        grid_spec=grid_spec,
        out_shape=jax.ShapeDtypeStruct((2 * n_all * SUBLANES, LANES), _F32),
        compiler_params=pltpu.CompilerParams(dimension_semantics=("arbitrary",), vmem_limit_bytes=VMEM_LIMIT),
        name="moe_experts",
    )(be, nv, rt2, hn_all, wg, wu, wd)


def _final(h_all, o2, route, g, *, row0, n, tile):
    blk0 = row0 // tile
    row = lambda i: (blk0 + i, 0)
    return pl.pallas_call(
        _final_kernel,
        grid=(n // tile,),
        in_specs=[pl.BlockSpec((tile, D_MODEL), row), pl.BlockSpec((tile * 2 * SUBLANES, LANES), row),
                  pl.BlockSpec((tile, ROUTE_COLS), row), pl.BlockSpec((1, D_MODEL), lambda i: (0, 0))],
        out_specs=pl.BlockSpec((tile, D_MODEL), lambda i: (i, 0)),
        out_shape=jax.ShapeDtypeStruct((n, D_MODEL), _F32),
        compiler_params=pltpu.CompilerParams(dimension_semantics=("arbitrary",)),
        name="final_norm",
    )(h_all, o2, route, g)


def kernel(x_prompt, x_sample, state_pool, state_conv, mix_norm_g, w_in, pool_w, pool_scale, sgu_norm_g, sgu_norm_b, sgu_w, sgu_b, conv_w, conv_b, conv_norm_g, conv_norm_b, w_branch_pool, w_branch_sgu, w_branch_conv, w_out, ffn_norm_g, w_group, b_group, w_router, b_router, w_gate_e, w_up_e, w_down_e, final_norm_g):
    nb, seq, _ = x_prompt.shape
    nseq, nstep, _ = x_sample.shape
    n_prompt = nb * seq
    n_sample = nseq * nstep
    n_all = n_prompt + n_sample
    assert seq % TQ == 0 and TQ % CHUNK == 0 and nstep < CHUNK and n_sample % TQ == 0
    assert n_prompt % n_sample == 0 and n_all % 512 == 0 and n_sample % MOE_BLOCK == 0
    n_blocks = (2 * n_all) // MOE_BLOCK + N_EXPERTS
    n_rows = n_blocks * MOE_BLOCK

    w = dict(nstep=nstep, mix_norm_g=mix_norm_g, w_in=w_in, pool_w=pool_w, pool_scale=pool_scale,
             sgu_norm_g=sgu_norm_g, sgu_norm_b=sgu_norm_b, sgu_w=sgu_w, sgu_b=sgu_b, conv_w=conv_w, conv_b=conv_b,
             conv_norm_g=conv_norm_g, conv_norm_b=conv_norm_b, w_branch_pool=w_branch_pool,
             w_branch_sgu=w_branch_sgu, w_branch_conv=w_branch_conv, w_out=w_out, ffn_norm_g=ffn_norm_g,
             w_group=w_group, b_group=b_group, w_router=w_router, b_router=b_router)

    xp = x_prompt.reshape(n_prompt, D_MODEL)
    xs = x_sample.transpose(1, 0, 2).reshape(n_sample, D_MODEL)
    sp_t = state_pool.transpose(0, 2, 1, 3)
    sc_t = state_conv.transpose(0, 2, 1, 3)

    pool_p, pool_s, conv_p, conv_s, chunk_v = [], [], [], [], []
    prev = None
    for l in range(DEPTH):
        ops_p, ops_s = _layer_weights(l, w)
        in_p = (xp,) if prev is None else prev
        in_s = (xs,) if prev is None else prev
        h_s, hn_s, route_s, cnt_s, pnew, cnew, vn = _mixer_sample(in_s, sp_t[l], sc_t[l], ops_s, nseq=nseq,
                                                                  nstep=nstep, n_prompt=n_prompt)
        h, hn, route, cnt, pstate, cstate = _mixer_prompt(in_p, (h_s, hn_s, route_s, cnt_s), ops_p,
                                                          nb=nb, seq=seq, n_all=n_all)
        pos, binfo = _plan(route, cnt, n_blocks=n_blocks)
        rt2 = _rows(pos[:, 0], pos[:, 1], n_rows=n_rows)
        o2 = _moe(binfo[0, :n_blocks], binfo[1, :n_blocks], rt2, hn, w_gate_e[l], w_up_e[l], w_down_e[l],
                  n_blocks=n_blocks)
        prev = (h, o2, route)
        to_seq = lambda a: a.reshape(nstep, nseq, W_BR).transpose(1, 0, 2)
        pool_p.append(pstate)
        conv_p.append(cstate)
        pool_s.append(jnp.concatenate([state_pool[l][:, nstep:], to_seq(pnew)], axis=1))
        conv_s.append(jnp.concatenate([state_conv[l][:, nstep:], to_seq(cnew)], axis=1))
        chunk_v.append(to_seq(vn))

    g = final_norm_g[None]
    y_prompt = _final(*prev, g, row0=0, n=n_prompt, tile=512).reshape(nb, seq, D_MODEL)
    y_sample = _final(*prev, g, row0=n_prompt, n=n_sample, tile=n_sample)
    y_sample = y_sample.reshape(nstep, nseq, D_MODEL).transpose(1, 0, 2)
    return (y_prompt, y_sample, jnp.stack(pool_p), jnp.stack(pool_s), jnp.stack(conv_p), jnp.stack(conv_s),
            jnp.stack(chunk_v))
```

```python
import functools

import jax
import jax.numpy as jnp
from jax import lax
from jax.experimental import pallas as pl
from jax.experimental.pallas import tpu as pltpu

D_MODEL = 1024
DEPTH = 4
PAST_LEN = 16384
W_BR = 512
POOL_WINDOWS = (2, 4, 8, 16)
POOL_GC = 128
POOL_HIST = 15
SGU_HEADS = 4
SGU_HC = 128
CHUNK = 128
CONV_WIDTH = 31
CONV_HIST = 30
N_GROUPS = 4
EPG = 8
N_EXPERTS = 32
D_EXPERT = 512
MOE_BLOCK = 128
EPS = 1e-6

LANES = 128
SUBLANES = 8
VMEM_LIMIT = 56 * 1024 * 1024

TQ = 256
ROUTE_COLS = 8
EXP_LANE0 = N_GROUPS
CONV_RB = 64
POOL_PAD = 16
CONV_PAD = 32

_MXU = jnp.bfloat16
_F32 = jnp.float32


def _dot(a, b):
    return jnp.dot(a.astype(_MXU), b.astype(_MXU), preferred_element_type=_F32)


def _rms_norm(x, g):
    return x * lax.rsqrt(jnp.mean(x * x, axis=-1, keepdims=True) + EPS) * g


def _layer_norm(x, g, b):
    mu = jnp.mean(x, axis=-1, keepdims=True)
    xc = x - mu
    var = jnp.mean(xc * xc, axis=-1, keepdims=True)
    return xc * lax.rsqrt(var + EPS) * g + b


def _const_spec(shape):
    nd = len(shape)
    return pl.BlockSpec(shape, lambda *_: (0,) * nd, pipeline_mode=pl.Buffered(1))


def _tile_rows_to_tokens(ref, start, n, stride):
    return jnp.concatenate([ref[pl.ds(start + j, n, stride=stride), :] for j in range(SUBLANES)], axis=-1)


def _mixer_input(in_refs, combine, n):
    if not combine:
        return in_refs[0][...]
    h_ref, o2_ref, rprev_ref = in_refs
    g0 = rprev_ref[:, 4:5]
    g1 = rprev_ref[:, 5:6]
    o0 = _tile_rows_to_tokens(o2_ref, 0, n, 2 * SUBLANES)
    o1 = _tile_rows_to_tokens(o2_ref, SUBLANES, n, 2 * SUBLANES)
    return h_ref[...] + (g0 * o0 + g1 * o1)


def _route(hn, wrt_ref, brt_ref, ltri_ref, cnt_acc, route_ref):
    n = hn.shape[0]
    logits = _dot(hn, wrt_ref[...]) + brt_ref[...]
    lane = lax.broadcasted_iota(jnp.int32, (n, LANES), 1).astype(_F32)
    neg = jnp.float32(-jnp.inf)
    is_grp = lane < N_GROUPS
    gl = jnp.where(is_grp, logits, neg)
    gmax = jnp.max(gl, axis=-1, keepdims=True)
    grp = jnp.min(jnp.where(gl == gmax, lane, float(LANES)), axis=-1, keepdims=True)
    gsum = jnp.sum(jnp.where(is_grp, jnp.exp(logits - gmax), 0.0), axis=-1, keepdims=True)
    grp_w = 1.0 / gsum
    lo = EXP_LANE0 + grp * EPG
    el = jnp.where((lane >= lo) & (lane < lo + EPG), logits, neg)
    v0 = jnp.max(el, axis=-1, keepdims=True)
    i0 = jnp.min(jnp.where(el == v0, lane, float(LANES)), axis=-1, keepdims=True)
    el1 = jnp.where(lane == i0, neg, el)
    v1 = jnp.max(el1, axis=-1, keepdims=True)
    i1 = jnp.min(jnp.where(el1 == v1, lane, float(LANES)), axis=-1, keepdims=True)
    ex = jnp.exp(v1 - v0)
    den = 1.0 + ex
    g0 = grp_w / den
    g1 = grp_w * ex / den
    oh0 = lane == i0
    oh1 = lane == i1
    oh = jnp.where(oh0 | oh1, 1.0, 0.0)
    before = _dot(ltri_ref[...], oh) + cnt_acc[...]
    r0 = jnp.sum(jnp.where(oh0, before, 0.0), axis=-1, keepdims=True)
    r1 = jnp.sum(jnp.where(oh1, before, 0.0), axis=-1, keepdims=True)
    cnt_acc[...] = cnt_acc[...] + jnp.sum(oh, axis=0, keepdims=True)
    route_ref[:, 0:1] = i0 - EXP_LANE0
    route_ref[:, 1:2] = i1 - EXP_LANE0
    route_ref[:, 2:3] = r0
    route_ref[:, 3:4] = r1
    route_ref[:, 4:5] = g0
    route_ref[:, 5:6] = g1
    route_ref[:, 6:8] = jnp.zeros((n, 2), _F32)


def _mixer_tail(x, xn, ya, yb, yc, win_ref, wbp_ref, wbs_ref, wbc_ref, wout_ref, gffn_ref,
                wrt_ref, brt_ref, ltri_ref, cnt_acc, hout_ref, hn_ref, route_ref):
    n = x.shape[0]
    o5 = 5 * W_BR
    ga = jax.nn.sigmoid(_dot(xn, win_ref[:, o5:o5 + D_MODEL]))
    merged = ga * _dot(ya, wbp_ref[...])
    gb = jax.nn.sigmoid(_dot(xn, win_ref[:, o5 + D_MODEL:o5 + 2 * D_MODEL]))
    merged = merged + gb * _dot(yb, wbs_ref[...])
    gc = jax.nn.sigmoid(_dot(xn, win_ref[:, o5 + 2 * D_MODEL:o5 + 3 * D_MODEL]))
    merged = merged + gc * _dot(yc, wbc_ref[...])
    h = x + _dot(merged, wout_ref[...])
    hout_ref[...] = h
    hn = _rms_norm(h, gffn_ref[...])
    for j in range(SUBLANES):
        hn_ref[pl.ds(j, n, stride=SUBLANES), :] = hn[:, j * LANES:(j + 1) * LANES]
    _route(hn, wrt_ref, brt_ref, ltri_ref, cnt_acc, route_ref)


def _mixer_prompt_kernel(*refs, combine, nt, n_main):
    n_in = 3 if combine else 1
    in_refs = refs[:n_in]
    hs_ref, hns_ref, routes_ref = refs[n_in:n_in + 3]
    hout_ref, hn_ref, route_ref = refs[-10:-7]
    step = pl.program_id(0)

    @pl.when(step >= n_main)
    def _():
        hout_ref[...] = hs_ref[...]
        hn_ref[...] = hns_ref[...]
        route_ref[...] = routes_ref[...]

    @pl.when(step < n_main)
    def _():
        _mixer_prompt_step(in_refs, refs[n_in + 3:], combine, nt)


def _mixer_prompt_step(in_refs, refs, combine, nt):
    (cnt_in_ref, gmix_ref, win_ref, poolw_ref, pscale_ref, sgug_ref, sgub_ref, sguw_ref, sgubias_ref,
     convw_ref, convb_ref, cng_ref, cnb_ref, wbp_ref, wbs_ref, wbc_ref, wout_ref,
     gffn_ref, wrt_ref, brt_ref, ltri_ref,
     hout_ref, hn_ref, route_ref, cnt_ref, pstate_ref, cstate_ref,
     pbuf, cbuf, ybuf, cnt_acc) = refs
    step = pl.program_id(0)
    i = lax.rem(step, nt)

    @pl.when(step == 0)
    def _():
        cnt_acc[...] = cnt_in_ref[...]

    @pl.when(i == 0)
    def _():
        pbuf[0:POOL_PAD, :] = jnp.zeros((POOL_PAD, W_BR), _F32)
        cbuf[0:CONV_PAD, :] = jnp.zeros((CONV_PAD, W_BR), _F32)

    x = _mixer_input(in_refs, combine, TQ)
    xn = _rms_norm(x, gmix_ref[...]).astype(_MXU)

    p = _dot(xn, win_ref[:, 0:W_BR])
    pbuf[POOL_PAD:POOL_PAD + TQ, :] = p
    pos = i * TQ + lax.broadcasted_iota(jnp.int32, (TQ, 1), 0)
    ya_parts = []
    for gi, w in enumerate(POOL_WINDOWS):
        sl = slice(gi * POOL_GC, (gi + 1) * POOL_GC)
        acc = p[:, sl]
        for j in range(1, w):
            acc = acc + pbuf[POOL_PAD - j:POOL_PAD - j + TQ, sl]
        cnt = jnp.minimum(pos + 1, w).astype(_F32)
        d = acc / cnt - p[:, sl]
        ya_parts.append(_dot(d, poolw_ref[gi]))
    ya = jnp.concatenate(ya_parts, axis=-1) * pscale_ref[...]

    @pl.when(i == nt - 1)
    def _():
        pstate_ref[0] = pbuf[POOL_PAD + TQ - POOL_HIST:POOL_PAD + TQ, :]

    pbuf[0:POOL_PAD, :] = pbuf[TQ:TQ + POOL_PAD, :]

    u = _dot(xn, win_ref[:, W_BR:2 * W_BR])
    v = _dot(xn, win_ref[:, 2 * W_BR:3 * W_BR])
    vn = _layer_norm(v, sgug_ref[...], sgub_ref[...])
    rows = []
    for c in range(TQ // CHUNK):
        heads = [_dot(sguw_ref[hd], vn[c * CHUNK:(c + 1) * CHUNK, hd * SGU_HC:(hd + 1) * SGU_HC])
                 for hd in range(SGU_HEADS)]
        rows.append(jnp.concatenate(heads, axis=-1) + sgubias_ref[...])
    yb = u * jnp.concatenate(rows, axis=0)

    ca = _dot(xn, win_ref[:, 3 * W_BR:4 * W_BR])
    cg = _dot(xn, win_ref[:, 4 * W_BR:5 * W_BR])
    cbuf[CONV_PAD:CONV_PAD + TQ, :] = ca * jax.nn.sigmoid(cg)
    for lb in range(W_BR // LANES):
        sl = slice(lb * LANES, (lb + 1) * LANES)
        for rb in range(TQ // CONV_RB):
            r0 = CONV_PAD + rb * CONV_RB
            acc = jnp.broadcast_to(convb_ref[:, sl], (CONV_RB, LANES))
            for j in range(CONV_WIDTH):
                acc = acc + cbuf[r0 - j:r0 - j + CONV_RB, sl] * convw_ref[CONV_HIST - j:CONV_WIDTH - j, sl]
            ybuf[rb * CONV_RB:(rb + 1) * CONV_RB, sl] = acc
    yc = jax.nn.silu(_layer_norm(ybuf[...], cng_ref[...], cnb_ref[...]))

    @pl.when(i == nt - 1)
    def _():
        cstate_ref[0] = cbuf[CONV_PAD + TQ - CONV_HIST:CONV_PAD + TQ, :]

    cbuf[0:CONV_PAD, :] = cbuf[TQ:TQ + CONV_PAD, :]

    _mixer_tail(x, xn, ya, yb, yc, win_ref, wbp_ref, wbs_ref, wbc_ref, wout_ref, gffn_ref,
                wrt_ref, brt_ref, ltri_ref, cnt_acc, hout_ref, hn_ref, route_ref)
    cnt_ref[...] = cnt_acc[...]


def _mixer_sample_kernel(*refs, combine, nseq, nstep):
    n_in = 3 if combine else 1
    in_refs = refs[:n_in]
    (sp_ref, sc_ref, gmix_ref, win_ref, poolw_ref, pscale_ref, sgug_ref, sgub_ref, sguw_ref, sgubias_ref,
     convw_ref, convb_ref, cng_ref, cnb_ref, wbp_ref, wbs_ref, wbc_ref, wout_ref,
     gffn_ref, wrt_ref, brt_ref, ltri_ref,
     hout_ref, hn_ref, route_ref, cnt_ref, pnew_ref, cnew_ref, vn_ref,
     ybuf, cnt_acc) = refs[n_in:]
    n = nseq * nstep
    cnt_acc[...] = jnp.zeros_like(cnt_acc)
    x = _mixer_input(in_refs, combine, n)
    xn = _rms_norm(x, gmix_ref[...]).astype(_MXU)

    def slab(a, t):
        return a[t * nseq:(t + 1) * nseq]

    p = _dot(xn, win_ref[:, 0:W_BR])
    pnew_ref[...] = p
    ya_rows = []
    for t in range(nstep):
        parts = []
        for gi, w in enumerate(POOL_WINDOWS):
            sl = slice(gi * POOL_GC, (gi + 1) * POOL_GC)
            acc = slab(p, t)[:, sl]
            for j in range(1, w):
                k = POOL_HIST + t - j
                acc = acc + (sp_ref[k, :, sl] if k < POOL_HIST else slab(p, k - POOL_HIST)[:, sl])
            cnt = float(min(PAST_LEN + t + 1, w))
            d = acc / cnt - slab(p, t)[:, sl]
            parts.append(_dot(d, poolw_ref[gi]))
        ya_rows.append(jnp.concatenate(parts, axis=-1))
    ya = jnp.concatenate(ya_rows, axis=0) * pscale_ref[...]

    u = _dot(xn, win_ref[:, W_BR:2 * W_BR])
    v = _dot(xn, win_ref[:, 2 * W_BR:3 * W_BR])
    vn = _layer_norm(v, sgug_ref[...], sgub_ref[...])
    vn_ref[...] = vn
    s_rows = []
    for t in range(nstep):
        acc = jnp.broadcast_to(sgubias_ref[t:t + 1, :], (nseq, W_BR))
        for s in range(t + 1):
            acc = acc + sguw_ref[t * nstep + s:t * nstep + s + 1, :] * slab(vn, s)
        s_rows.append(acc)
    yb = u * jnp.concatenate(s_rows, axis=0)

    ca = _dot(xn, win_ref[:, 3 * W_BR:4 * W_BR])
    cg = _dot(xn, win_ref[:, 4 * W_BR:5 * W_BR])
    cnew_ref[...] = ca * jax.nn.sigmoid(cg)
    for t in range(nstep):
        for lb in range(W_BR // LANES):
            sl = slice(lb * LANES, (lb + 1) * LANES)
            acc = jnp.broadcast_to(convb_ref[:, sl], (nseq, LANES))
            for kw in range(CONV_WIDTH):
                k = t + kw
                row = sc_ref[k, :, sl] if k < CONV_HIST else cnew_ref[(k - CONV_HIST) * nseq:(k - CONV_HIST + 1) * nseq, sl]
                acc = acc + row * convw_ref[kw:kw + 1, sl]
            ybuf[t * nseq:(t + 1) * nseq, sl] = acc
    yc = jax.nn.silu(_layer_norm(ybuf[...], cng_ref[...], cnb_ref[...]))

    _mixer_tail(x, xn, ya, yb, yc, win_ref, wbp_ref, wbs_ref, wbc_ref, wout_ref, gffn_ref,
                wrt_ref, brt_ref, ltri_ref, cnt_acc, hout_ref, hn_ref, route_ref)
    cnt_ref[...] = cnt_acc[...]


def _lane_to_sublane(row):
    sub = lax.broadcasted_iota(jnp.int32, (LANES, LANES), 0)
    lane = lax.broadcasted_iota(jnp.int32, (LANES, LANES), 1)
    return jnp.sum(jnp.where(sub == lane, jnp.broadcast_to(row, (LANES, LANES)), 0.0), axis=1, keepdims=True)


def _plan_kernel(route_ref, cnt_ref, pos_ref, binfo_ref, *, nblk_lanes):
    cnt = cnt_ref[...]
    padded = jnp.floor((cnt + (MOE_BLOCK - 1)) * (1.0 / MOE_BLOCK)) * MOE_BLOCK
    lane1 = lax.broadcasted_iota(jnp.int32, (SUBLANES, LANES), 1)
    pends = jnp.broadcast_to(padded, (SUBLANES, LANES))
    sh = 1
    while sh < LANES:
        pends = pends + jnp.where(lane1 >= sh, pltpu.roll(pends, sh, axis=1), 0.0)
        sh *= 2
    pends = pends[0:1]
    pstart = pends - padded

    n = route_ref.shape[0]
    lane = lax.broadcasted_iota(jnp.int32, (n, LANES), 1).astype(_F32)
    for k in range(2):
        e_lane = route_ref[:, k:k + 1] + EXP_LANE0
        base = jnp.sum(jnp.where(lane == e_lane, pstart, 0.0), axis=-1, keepdims=True)
        pos_ref[:, k:k + 1] = (base + route_ref[:, 2 + k:3 + k]).astype(jnp.int32)

    @pl.when(pl.program_id(0) == 0)
    def _():
        pends_c = _lane_to_sublane(pends)
        pstart_c = _lane_to_sublane(pstart)
        cnt_c = _lane_to_sublane(cnt)
        sub = lax.broadcasted_iota(jnp.int32, (LANES, LANES), 0)
        is_exp = (sub >= EXP_LANE0) & (sub < EXP_LANE0 + N_EXPERTS)
        binfo_ref[...] = jnp.zeros_like(binfo_ref)
        for q in range(nblk_lanes // LANES):
            bstart = ((lax.broadcasted_iota(jnp.int32, (LANES, LANES), 1) + q * LANES) * MOE_BLOCK).astype(_F32)
            done = jnp.where(is_exp & (pends_c <= bstart), 1.0, 0.0)
            bexp = jnp.minimum(jnp.sum(done, axis=0, keepdims=True), float(N_EXPERTS - 1))
            inside = is_exp & (pstart_c <= bstart) & (bstart < pends_c)
            valid = jnp.clip(pstart_c + cnt_c - bstart, 0.0, float(MOE_BLOCK))
            nval = jnp.sum(jnp.where(inside, valid, 0.0), axis=0, keepdims=True)
            binfo_ref[0:1, q * LANES:(q + 1) * LANES] = bexp.astype(jnp.int32)
            binfo_ref[1:2, q * LANES:(q + 1) * LANES] = nval.astype(jnp.int32)


def _rows_kernel(pos0_ref, pos1_ref, src_pad_hbm, dst_pad_hbm, src_ref, dst_ref, sem, *, n_tok):
    unroll = 8
    fill_src = pltpu.make_async_copy(src_pad_hbm, src_ref, sem.at[0])
    fill_dst = pltpu.make_async_copy(dst_pad_hbm, dst_ref, sem.at[1])
    fill_src.start()
    fill_dst.start()
    fill_src.wait()
    fill_dst.wait()

    def put(i, c):
        for k in range(unroll):
            t = i * unroll + k
            p0 = pos0_ref[t]
            p1 = pos1_ref[t]
            src_ref[p0] = t
            src_ref[p1] = t
            dst_ref[p0] = 2 * t
            dst_ref[p1] = 2 * t + 1
        return c
    lax.fori_loop(0, n_tok // unroll, put, 0)


def _moe_kernel(be_ref, nv_ref, src_ref, dst_ref, hn_hbm, wg_ref, wu_ref, wd_ref, o2_hbm,
                xbuf, obuf, wg_bf, wu_bf, wd_bf, gsem, ssem, *, n_tok):
    b = pl.program_id(0)
    nblk = pl.num_programs(0)
    slot = b % 2
    tile = SUBLANES
    spare0 = 2 * n_tok * tile

    def gather_copy(tok, row, s):
        return pltpu.make_async_copy(hn_hbm.at[pl.ds(pl.multiple_of(tok * tile, tile), tile)],
                                     xbuf.at[s, pl.ds(row * tile, tile)], gsem.at[s])

    def scatter_copy(row, dst, s):
        return pltpu.make_async_copy(obuf.at[s, pl.ds(row * tile, tile)],
                                     o2_hbm.at[pl.ds(pl.multiple_of(dst * tile, tile), tile)], ssem.at[s])

    def spare_copy(half):
        return pltpu.make_async_copy(obuf.at[1], o2_hbm.at[pl.ds(spare0 + half * MOE_BLOCK * tile, MOE_BLOCK * tile)],
                                     ssem.at[1])

    def start_gather(blk, s):
        for r in range(MOE_BLOCK):
            gather_copy(src_ref[blk * MOE_BLOCK + r], r, s).start(priority=r % 2)

    def wait_gather(s):
        for r in range(MOE_BLOCK):
            gather_copy(0, 0, s).wait()

    def start_scatter(blk, s):
        for r in range(MOE_BLOCK):
            scatter_copy(r, dst_ref[blk * MOE_BLOCK + r], s).start(priority=r % 2)

    def wait_scatter(s):
        for r in range(MOE_BLOCK):
            scatter_copy(0, 0, s).wait()

    def used(blk):
        return nv_ref[jnp.clip(blk, 0, nblk - 1)] > 0

    @pl.when(b == 0)
    def _():
        obuf[1] = jnp.zeros(obuf.shape[1:], _F32)
        spare_copy(0).start()
        spare_copy(1).start()
        spare_copy(0).wait()
        spare_copy(1).wait()

        @pl.when(used(0))
        def _():
            start_gather(0, 0)

    @pl.when((b >= 2) & used(b - 2))
    def _():
        wait_scatter(slot)

    @pl.when(used(b))
    def _():
        @pl.when((b == 0) | (be_ref[b] != be_ref[jnp.maximum(b - 1, 0)]))
        def _():
            wg_bf[...] = wg_ref[...].astype(_MXU)
            wu_bf[...] = wu_ref[...].astype(_MXU)
            wd_bf[...] = wd_ref[...].astype(_MXU)

        wait_gather(slot)
        x = _tile_rows_to_tokens(xbuf.at[slot], 0, MOE_BLOCK, SUBLANES).astype(_MXU)
        start_gather(jnp.minimum(b + 1, nblk - 1), 1 - slot)
        hb = jax.nn.silu(_dot(x, wg_bf[...])) * _dot(x, wu_bf[...])
        o = _dot(hb, wd_bf[...])
        for j in range(SUBLANES):
            obuf[slot, pl.ds(j, MOE_BLOCK, stride=SUBLANES), :] = o[:, j * LANES:(j + 1) * LANES]
        start_scatter(b, slot)

    @pl.when((b >= 1) & used(b - 1) & jnp.logical_not(used(b)))
    def _():
        wait_gather(slot)

    @pl.when(b == nblk - 1)
    def _():
        @pl.when((b >= 1) & used(b - 1))
        def _():
            wait_scatter(1 - slot)

        @pl.when(used(b))
        def _():
            wait_gather(1 - slot)
            wait_scatter(slot)


def _final_kernel(h_ref, o2_ref, rprev_ref, g_ref, y_ref):
    n = h_ref.shape[0]
    y_ref[...] = _rms_norm(_mixer_input((h_ref, o2_ref, rprev_ref), True, n), g_ref[...])


def _stacked_weights(w):
    nstep = w['nstep']
    depth = w['w_in'].shape[0]
    sgu_w = w['sgu_w']
    sgu_b = w['sgu_b']
    vec = lambda a: a[:, None, :]
    pad = jnp.zeros((depth, D_MODEL, LANES - N_GROUPS - N_EXPERTS), _F32)
    w_rt = jnp.concatenate([w['w_group'], w['w_router'], pad], axis=-1)
    b_rt = jnp.concatenate([w['b_group'], w['b_router'], pad[:, 0, :]], axis=-1)
    head = (vec(w['mix_norm_g']), w['w_in'].astype(_MXU), w['pool_w'].astype(_MXU),
            vec(w['pool_scale']), vec(w['sgu_norm_g']), vec(w['sgu_norm_b']))
    tail = (w['conv_w'], vec(w['conv_b']), vec(w['conv_norm_g']), vec(w['conv_norm_b']),
            w['w_branch_pool'].astype(_MXU), w['w_branch_sgu'].astype(_MXU),
            w['w_branch_conv'].astype(_MXU), w['w_out'].astype(_MXU),
            vec(w['ffn_norm_g']), w_rt.astype(_MXU), vec(b_rt))
    sgu_p = (jnp.tril(sgu_w).astype(_MXU), jnp.repeat(sgu_b.transpose(0, 2, 1), SGU_HC, axis=2))
    corner = jnp.tril(sgu_w[:, :, :nstep, :nstep]).transpose(0, 2, 3, 1)
    sgu_s = (jnp.repeat(corner.reshape(depth, nstep * nstep, SGU_HEADS), SGU_HC, axis=2),
             jnp.repeat(sgu_b[:, :, :nstep].transpose(0, 2, 1), SGU_HC, axis=2))
    return head + sgu_p + tail, head + sgu_s + tail


def _weight_specs(ops, l):
    def spec(o):
        nd = o.ndim - 1
        return pl.BlockSpec((None,) + o.shape[1:], lambda *_: (l,) + (0,) * nd, pipeline_mode=pl.Buffered(1))
    return [spec(o) for o in ops]


def _ltri(n):
    r = lax.broadcasted_iota(jnp.int32, (n, n), 0)
    c = lax.broadcasted_iota(jnp.int32, (n, n), 1)
    return (c < r).astype(_MXU)


def _mixer_prompt(inputs, sample, ops, l, *, nb, seq, n_all):
    combine = len(inputs) == 3
    nt = seq // TQ
    n_main = nb * nt
    n_steps = n_all // TQ
    h_s, hn_s, route_s, cnt_s = sample
    row = lambda s: (s, 0)
    main_row = lambda s: (jnp.minimum(s, n_main - 1), 0)
    tail_row = lambda s: (jnp.maximum(s - n_main, 0), 0)
    seq_blk = lambda s: (jnp.minimum(s // nt, nb - 1), 0, 0)
    if combine:
        in_specs = [pl.BlockSpec((TQ, D_MODEL), main_row), pl.BlockSpec((TQ * 2 * SUBLANES, LANES), main_row),
                    pl.BlockSpec((TQ, ROUTE_COLS), main_row)]
    else:
        in_specs = [pl.BlockSpec((TQ, D_MODEL), main_row)]
    in_specs += [pl.BlockSpec((TQ, D_MODEL), tail_row), pl.BlockSpec((TQ * SUBLANES, LANES), tail_row),
                 pl.BlockSpec((TQ, ROUTE_COLS), tail_row), _const_spec(cnt_s.shape)]
    ltri = _ltri(TQ)
    out_shape = (jax.ShapeDtypeStruct((n_all, D_MODEL), _F32),
                 jax.ShapeDtypeStruct((n_all * SUBLANES, LANES), _F32),
                 jax.ShapeDtypeStruct((n_all, ROUTE_COLS), _F32),
                 jax.ShapeDtypeStruct((1, LANES), _F32),
                 jax.ShapeDtypeStruct((nb, POOL_HIST, W_BR), _F32),
                 jax.ShapeDtypeStruct((nb, CONV_HIST, W_BR), _F32))
    out_specs = (pl.BlockSpec((TQ, D_MODEL), row), pl.BlockSpec((TQ * SUBLANES, LANES), row),
                 pl.BlockSpec((TQ, ROUTE_COLS), row), pl.BlockSpec((1, LANES), lambda s: (0, 0)),
                 pl.BlockSpec((1, POOL_HIST, W_BR), seq_blk), pl.BlockSpec((1, CONV_HIST, W_BR), seq_blk))
    return pl.pallas_call(
        functools.partial(_mixer_prompt_kernel, combine=combine, nt=nt, n_main=n_main),
        grid=(n_steps,),
        in_specs=in_specs + _weight_specs(ops, l) + [_const_spec(ltri.shape)],
        out_specs=out_specs,
        out_shape=out_shape,
        scratch_shapes=[pltpu.VMEM((POOL_PAD + TQ, W_BR), _F32), pltpu.VMEM((CONV_PAD + TQ, W_BR), _F32),
                        pltpu.VMEM((TQ, W_BR), _F32), pltpu.VMEM((1, LANES), _F32)],
        compiler_params=pltpu.CompilerParams(dimension_semantics=("arbitrary",), vmem_limit_bytes=VMEM_LIMIT),
        name="mixer_prompt",
    )(*inputs, h_s, hn_s, route_s, cnt_s, *ops, ltri)


def _mixer_sample(inputs, sp, sc, ops, l, *, nseq, nstep, n_prompt):
    combine = len(inputs) == 3
    n = nseq * nstep
    blk = n_prompt // n
    row = lambda i: (blk, 0)
    zero = lambda i: (0, 0)
    if combine:
        in_specs = [pl.BlockSpec((n, D_MODEL), row), pl.BlockSpec((n * 2 * SUBLANES, LANES), row),
                    pl.BlockSpec((n, ROUTE_COLS), row)]
    else:
        in_specs = [pl.BlockSpec((n, D_MODEL), zero)]
    in_specs += _weight_specs((sp, sc), l)
    ltri = _ltri(n)
    out_shape = (jax.ShapeDtypeStruct((n, D_MODEL), _F32), jax.ShapeDtypeStruct((n * SUBLANES, LANES), _F32),
                 jax.ShapeDtypeStruct((n, ROUTE_COLS), _F32), jax.ShapeDtypeStruct((1, LANES), _F32),
                 jax.ShapeDtypeStruct((n, W_BR), _F32), jax.ShapeDtypeStruct((n, W_BR), _F32),
                 jax.ShapeDtypeStruct((n, W_BR), _F32))
    out_specs = (pl.BlockSpec((n, D_MODEL), zero), pl.BlockSpec((n * SUBLANES, LANES), zero),
                 pl.BlockSpec((n, ROUTE_COLS), zero), pl.BlockSpec((1, LANES), zero),
                 pl.BlockSpec((n, W_BR), zero), pl.BlockSpec((n, W_BR), zero), pl.BlockSpec((n, W_BR), zero))
    return pl.pallas_call(
        functools.partial(_mixer_sample_kernel, combine=combine, nseq=nseq, nstep=nstep),
        grid=(1,),
        in_specs=in_specs + _weight_specs(ops, l) + [_const_spec(ltri.shape)],
        out_specs=out_specs,
        out_shape=out_shape,
        scratch_shapes=[pltpu.VMEM((n, W_BR), _F32), pltpu.VMEM((1, LANES), _F32)],
        compiler_params=pltpu.CompilerParams(dimension_semantics=("arbitrary",), vmem_limit_bytes=VMEM_LIMIT),
        name="mixer_sample",
    )(*inputs, sp, sc, *ops, ltri)


def _plan(route, cnt, *, n_blocks):
    n_all = route.shape[0]
    tile = 512
    nblk_lanes = -(-n_blocks // LANES) * LANES
    return pl.pallas_call(
        functools.partial(_plan_kernel, nblk_lanes=nblk_lanes),
        grid=(n_all // tile,),
        in_specs=[pl.BlockSpec((tile, ROUTE_COLS), lambda i: (i, 0)), pl.BlockSpec((1, LANES), lambda i: (0, 0))],
        out_specs=(pl.BlockSpec((tile, 2), lambda i: (i, 0)), pl.BlockSpec((SUBLANES, nblk_lanes), lambda i: (0, 0))),
        out_shape=(jax.ShapeDtypeStruct((n_all, 2), jnp.int32), jax.ShapeDtypeStruct((SUBLANES, nblk_lanes), jnp.int32)),
        compiler_params=pltpu.CompilerParams(dimension_semantics=("arbitrary",)),
        name="moe_plan",
    )(route, cnt)


def _rows(pos0, pos1, *, n_rows):
    n_tok = pos0.shape[0]
    src_pad = jnp.full((n_rows,), n_tok - 1, jnp.int32)
    dst_pad = 2 * n_tok + jnp.bitwise_and(jnp.arange(n_rows, dtype=jnp.int32), 2 * MOE_BLOCK - 1)
    smem = pl.BlockSpec(memory_space=pltpu.SMEM)
    hbm = pl.BlockSpec(memory_space=pl.ANY)
    return pl.pallas_call(
        functools.partial(_rows_kernel, n_tok=n_tok),
        in_specs=[smem, smem, hbm, hbm],
        out_specs=(smem, smem),
        out_shape=(jax.ShapeDtypeStruct((n_rows,), jnp.int32), jax.ShapeDtypeStruct((n_rows,), jnp.int32)),
        scratch_shapes=[pltpu.SemaphoreType.DMA((2,))],
        name="moe_rows",
    )(pos0, pos1, src_pad, dst_pad)


def _moe(be, nv, src, dst, hn_all, wg, wu, wd, l, *, n_blocks):
    n_all = hn_all.shape[0] // SUBLANES
    n_tiles = 2 * n_all + 2 * MOE_BLOCK
    wspec_in = pl.BlockSpec((None, None, D_MODEL, D_EXPERT), lambda b, be, *_: (l, be[b], 0, 0))
    wspec_out = pl.BlockSpec((None, None, D_EXPERT, D_MODEL), lambda b, be, *_: (l, be[b], 0, 0))
    grid_spec = pltpu.PrefetchScalarGridSpec(
        num_scalar_prefetch=4,
        grid=(n_blocks,),
        in_specs=[pl.BlockSpec(memory_space=pl.ANY), wspec_in, wspec_in, wspec_out],
        out_specs=pl.BlockSpec(memory_space=pl.ANY),
        scratch_shapes=[pltpu.VMEM((2, MOE_BLOCK * SUBLANES, LANES), _F32),
                        pltpu.VMEM((2, MOE_BLOCK * SUBLANES, LANES), _F32),
                        pltpu.VMEM((D_MODEL, D_EXPERT), _MXU), pltpu.VMEM((D_MODEL, D_EXPERT), _MXU),
                        pltpu.VMEM((D_EXPERT, D_MODEL), _MXU),
                        pltpu.SemaphoreType.DMA((2,)), pltpu.SemaphoreType.DMA((2,))])
    return pl.pallas_call(
        functools.partial(_moe_kernel, n_tok=n_all),
        grid_spec=grid_spec,
        out_shape=jax.ShapeDtypeStruct((n_tiles * SUBLANES, LANES), _F32),
        compiler_params=pltpu.CompilerParams(dimension_semantics=("arbitrary",), vmem_limit_bytes=VMEM_LIMIT),
        name="moe_experts",
    )(be, nv, src, dst, hn_all, wg, wu, wd)


def _final(h_all, o2, route, g, *, row0, n, tile):
    blk0 = row0 // tile
    row = lambda i: (blk0 + i, 0)
    return pl.pallas_call(
        _final_kernel,
        grid=(n // tile,),
        in_specs=[pl.BlockSpec((tile, D_MODEL), row), pl.BlockSpec((tile * 2 * SUBLANES, LANES), row),
                  pl.BlockSpec((tile, ROUTE_COLS), row), pl.BlockSpec((1, D_MODEL), lambda i: (0, 0))],
        out_specs=pl.BlockSpec((tile, D_MODEL), lambda i: (i, 0)),
        out_shape=jax.ShapeDtypeStruct((n, D_MODEL), _F32),
        compiler_params=pltpu.CompilerParams(dimension_semantics=("arbitrary",)),
        name="final_norm",
    )(h_all, o2, route, g)


def kernel(x_prompt, x_sample, state_pool, state_conv, mix_norm_g, w_in, pool_w, pool_scale, sgu_norm_g, sgu_norm_b, sgu_w, sgu_b, conv_w, conv_b, conv_norm_g, conv_norm_b, w_branch_pool, w_branch_sgu, w_branch_conv, w_out, ffn_norm_g, w_group, b_group, w_router, b_router, w_gate_e, w_up_e, w_down_e, final_norm_g):
    nb, seq, _ = x_prompt.shape
    nseq, nstep, _ = x_sample.shape
    n_prompt = nb * seq
    n_sample = nseq * nstep
    n_all = n_prompt + n_sample
    assert seq % TQ == 0 and TQ % CHUNK == 0 and nstep < CHUNK and n_sample % TQ == 0
    assert n_prompt % n_sample == 0 and n_all % 512 == 0 and n_sample % MOE_BLOCK == 0
    n_blocks = (2 * n_all) // MOE_BLOCK + N_EXPERTS
    n_rows = n_blocks * MOE_BLOCK

    w = dict(nstep=nstep, mix_norm_g=mix_norm_g, w_in=w_in, pool_w=pool_w, pool_scale=pool_scale,
             sgu_norm_g=sgu_norm_g, sgu_norm_b=sgu_norm_b, sgu_w=sgu_w, sgu_b=sgu_b, conv_w=conv_w, conv_b=conv_b,
             conv_norm_g=conv_norm_g, conv_norm_b=conv_norm_b, w_branch_pool=w_branch_pool,
             w_branch_sgu=w_branch_sgu, w_branch_conv=w_branch_conv, w_out=w_out, ffn_norm_g=ffn_norm_g,
             w_group=w_group, b_group=b_group, w_router=w_router, b_router=b_router)

    xp = x_prompt.reshape(n_prompt, D_MODEL)
    xs = x_sample.transpose(1, 0, 2).reshape(n_sample, D_MODEL)
    sp_t = state_pool.transpose(0, 2, 1, 3)
    sc_t = state_conv.transpose(0, 2, 1, 3)

    pool_p, pool_s, conv_p, conv_s, chunk_v = [], [], [], [], []
    prev = None
    ops_p, ops_s = _stacked_weights(w)
    for l in range(DEPTH):
        in_p = (xp,) if prev is None else prev
        in_s = (xs,) if prev is None else prev
        h_s, hn_s, route_s, cnt_s, pnew, cnew, vn = _mixer_sample(in_s, sp_t, sc_t, ops_s, l, nseq=nseq,
                                                                  nstep=nstep, n_prompt=n_prompt)
        h, hn, route, cnt, pstate, cstate = _mixer_prompt(in_p, (h_s, hn_s, route_s, cnt_s), ops_p, l,
                                                          nb=nb, seq=seq, n_all=n_all)
        pos, binfo = _plan(route, cnt, n_blocks=n_blocks)
        src, dst = _rows(pos[:, 0], pos[:, 1], n_rows=n_rows)
        o2 = _moe(binfo[0, :n_blocks], binfo[1, :n_blocks], src, dst, hn, w_gate_e, w_up_e, w_down_e, l,
                  n_blocks=n_blocks)
        prev = (h, o2, route)
        to_seq = lambda a: a.reshape(nstep, nseq, W_BR).transpose(1, 0, 2)
        pool_p.append(pstate)
        conv_p.append(cstate)
        pool_s.append(jnp.concatenate([state_pool[l][:, nstep:], to_seq(pnew)], axis=1))
        conv_s.append(jnp.concatenate([state_conv[l][:, nstep:], to_seq(cnew)], axis=1))
        chunk_v.append(to_seq(vn))

    g = final_norm_g[None]
    y_prompt = _final(*prev, g, row0=0, n=n_prompt, tile=512).reshape(nb, seq, D_MODEL)
    y_sample = _final(*prev, g, row0=n_prompt, n=n_sample, tile=n_sample)
    y_sample = y_sample.reshape(nstep, nseq, D_MODEL).transpose(1, 0, 2)
    return (y_prompt, y_sample, jnp.stack(pool_p), jnp.stack(pool_s), jnp.stack(conv_p), jnp.stack(conv_s),
            jnp.stack(chunk_v))
```

```python
import functools

import jax
import jax.numpy as jnp
from jax import lax
from jax.experimental import pallas as pl
from jax.experimental.pallas import tpu as pltpu

D_MODEL = 1024
DEPTH = 4
PAST_LEN = 16384
W_BR = 512
POOL_WINDOWS = (2, 4, 8, 16)
POOL_GC = 128
POOL_HIST = 15
SGU_HEADS = 4
SGU_HC = 128
CHUNK = 128
CONV_WIDTH = 31
CONV_HIST = 30
N_GROUPS = 4
EPG = 8
N_EXPERTS = 32
D_EXPERT = 512
MOE_BLOCK = 128
EPS = 1e-6

LANES = 128
SUBLANES = 8
VMEM_LIMIT = 56 * 1024 * 1024

TQ = 256
ROUTE_COLS = 8
EXP_LANE0 = N_GROUPS
CONV_RB = 64
PACK_ROWS = D_MODEL // 2 // LANES
XG_STRIDE = MOE_BLOCK + SUBLANES
POOL_PAD = 16
CONV_PAD = 32

_MXU = jnp.bfloat16
_F32 = jnp.float32


def _dot(a, b):
    return jnp.dot(a.astype(_MXU), b.astype(_MXU), preferred_element_type=_F32)


def _rms_norm(x, g):
    return x * lax.rsqrt(jnp.mean(x * x, axis=-1, keepdims=True) + EPS) * g


def _layer_norm(x, g, b):
    mu = jnp.mean(x, axis=-1, keepdims=True)
    xc = x - mu
    var = jnp.mean(xc * xc, axis=-1, keepdims=True)
    return xc * lax.rsqrt(var + EPS) * g + b


def _const_spec(shape):
    nd = len(shape)
    return pl.BlockSpec(shape, lambda *_: (0,) * nd, pipeline_mode=pl.Buffered(1))


def _tile_rows_to_tokens(ref, start, n, stride):
    return jnp.concatenate([ref[pl.ds(start + j, n, stride=stride), :] for j in range(SUBLANES)], axis=-1)


def _mixer_input(in_refs, combine, n):
    if not combine:
        return in_refs[0][...]
    h_ref, o2_ref, rprev_ref = in_refs
    g0 = rprev_ref[:, 4:5]
    g1 = rprev_ref[:, 5:6]
    o0 = _tile_rows_to_tokens(o2_ref, 0, n, 2 * SUBLANES)
    o1 = _tile_rows_to_tokens(o2_ref, SUBLANES, n, 2 * SUBLANES)
    return h_ref[...] + (g0 * o0 + g1 * o1)


def _route(hn, wrt_ref, brt_ref, ltri_ref, cnt_acc, route_ref):
    n = hn.shape[0]
    logits = _dot(hn, wrt_ref[...]) + brt_ref[...]
    lane = lax.broadcasted_iota(jnp.int32, (n, LANES), 1).astype(_F32)
    neg = jnp.float32(-jnp.inf)
    is_grp = lane < N_GROUPS
    gl = jnp.where(is_grp, logits, neg)
    gmax = jnp.max(gl, axis=-1, keepdims=True)
    grp = jnp.min(jnp.where(gl == gmax, lane, float(LANES)), axis=-1, keepdims=True)
    gsum = jnp.sum(jnp.where(is_grp, jnp.exp(logits - gmax), 0.0), axis=-1, keepdims=True)
    grp_w = 1.0 / gsum
    lo = EXP_LANE0 + grp * EPG
    el = jnp.where((lane >= lo) & (lane < lo + EPG), logits, neg)
    v0 = jnp.max(el, axis=-1, keepdims=True)
    i0 = jnp.min(jnp.where(el == v0, lane, float(LANES)), axis=-1, keepdims=True)
    el1 = jnp.where(lane == i0, neg, el)
    v1 = jnp.max(el1, axis=-1, keepdims=True)
    i1 = jnp.min(jnp.where(el1 == v1, lane, float(LANES)), axis=-1, keepdims=True)
    ex = jnp.exp(v1 - v0)
    den = 1.0 + ex
    g0 = grp_w / den
    g1 = grp_w * ex / den
    oh0 = lane == i0
    oh1 = lane == i1
    oh = jnp.where(oh0 | oh1, 1.0, 0.0)
    before = _dot(ltri_ref[...], oh) + cnt_acc[...]
    r0 = jnp.sum(jnp.where(oh0, before, 0.0), axis=-1, keepdims=True)
    r1 = jnp.sum(jnp.where(oh1, before, 0.0), axis=-1, keepdims=True)
    cnt_acc[...] = cnt_acc[...] + jnp.sum(oh, axis=0, keepdims=True)
    route_ref[:, 0:1] = i0 - EXP_LANE0
    route_ref[:, 1:2] = i1 - EXP_LANE0
    route_ref[:, 2:3] = r0
    route_ref[:, 3:4] = r1
    route_ref[:, 4:5] = g0
    route_ref[:, 5:6] = g1
    route_ref[:, 6:8] = jnp.zeros((n, 2), _F32)


def _mixer_tail(x, xn, ya, yb, yc, win_ref, wbp_ref, wbs_ref, wbc_ref, wout_ref, gffn_ref,
                wrt_ref, brt_ref, ltri_ref, cnt_acc, hout_ref, hn_ref, route_ref):
    n = x.shape[0]
    o5 = 5 * W_BR
    ga = jax.nn.sigmoid(_dot(xn, win_ref[:, o5:o5 + D_MODEL]))
    merged = ga * _dot(ya, wbp_ref[...])
    gb = jax.nn.sigmoid(_dot(xn, win_ref[:, o5 + D_MODEL:o5 + 2 * D_MODEL]))
    merged = merged + gb * _dot(yb, wbs_ref[...])
    gc = jax.nn.sigmoid(_dot(xn, win_ref[:, o5 + 2 * D_MODEL:o5 + 3 * D_MODEL]))
    merged = merged + gc * _dot(yc, wbc_ref[...])
    h = x + _dot(merged, wout_ref[...])
    hout_ref[...] = h
    hn = _rms_norm(h, gffn_ref[...])
    half = D_MODEL // 2
    lo = lax.bitcast_convert_type(hn[:, :half].astype(_MXU).astype(_F32), jnp.uint32)
    hi = lax.bitcast_convert_type(hn[:, half:].astype(_MXU).astype(_F32), jnp.uint32)
    words = jnp.bitwise_or(jnp.bitwise_and(hi, jnp.uint32(0xFFFF0000)), jnp.right_shift(lo, jnp.uint32(16)))
    for j in range(PACK_ROWS):
        hn_ref[pl.ds(j, n, stride=PACK_ROWS), :] = words[:, j * LANES:(j + 1) * LANES]
    _route(hn, wrt_ref, brt_ref, ltri_ref, cnt_acc, route_ref)


def _mixer_prompt_kernel(*refs, combine, nt, n_main):
    n_in = 3 if combine else 1
    in_refs = refs[:n_in]
    hs_ref, hns_ref, routes_ref = refs[n_in:n_in + 3]
    hout_ref, hn_ref, route_ref = refs[-10:-7]
    step = pl.program_id(0)

    @pl.when(step >= n_main)
    def _():
        hout_ref[...] = hs_ref[...]
        hn_ref[...] = hns_ref[...]
        route_ref[...] = routes_ref[...]

    @pl.when(step < n_main)
    def _():
        _mixer_prompt_step(in_refs, refs[n_in + 3:], combine, nt)


def _mixer_prompt_step(in_refs, refs, combine, nt):
    (cnt_in_ref, gmix_ref, win_ref, poolw_ref, pscale_ref, sgug_ref, sgub_ref, sguw_ref, sgubias_ref,
     convw_ref, convb_ref, cng_ref, cnb_ref, wbp_ref, wbs_ref, wbc_ref, wout_ref,
     gffn_ref, wrt_ref, brt_ref, ltri_ref,
     hout_ref, hn_ref, route_ref, cnt_ref, pstate_ref, cstate_ref,
     pbuf, cbuf, ybuf, cnt_acc) = refs
    step = pl.program_id(0)
    i = lax.rem(step, nt)

    @pl.when(step == 0)
    def _():
        cnt_acc[...] = cnt_in_ref[...]

    @pl.when(i == 0)
    def _():
        pbuf[0:POOL_PAD, :] = jnp.zeros((POOL_PAD, W_BR), _F32)
        cbuf[0:CONV_PAD, :] = jnp.zeros((CONV_PAD, W_BR), _F32)

    x = _mixer_input(in_refs, combine, TQ)
    xn = _rms_norm(x, gmix_ref[...]).astype(_MXU)

    p = _dot(xn, win_ref[:, 0:W_BR])
    pbuf[POOL_PAD:POOL_PAD + TQ, :] = p
    pos = i * TQ + lax.broadcasted_iota(jnp.int32, (TQ, 1), 0)
    ya_parts = []
    for gi, w in enumerate(POOL_WINDOWS):
        sl = slice(gi * POOL_GC, (gi + 1) * POOL_GC)
        acc = p[:, sl]
        for j in range(1, w):
            acc = acc + pbuf[POOL_PAD - j:POOL_PAD - j + TQ, sl]
        cnt = jnp.minimum(pos + 1, w).astype(_F32)
        d = acc / cnt - p[:, sl]
        ya_parts.append(_dot(d, poolw_ref[gi]))
    ya = jnp.concatenate(ya_parts, axis=-1) * pscale_ref[...]

    @pl.when(i == nt - 1)
    def _():
        pstate_ref[0] = pbuf[POOL_PAD + TQ - POOL_HIST:POOL_PAD + TQ, :]

    pbuf[0:POOL_PAD, :] = pbuf[TQ:TQ + POOL_PAD, :]

    u = _dot(xn, win_ref[:, W_BR:2 * W_BR])
    v = _dot(xn, win_ref[:, 2 * W_BR:3 * W_BR])
    vn = _layer_norm(v, sgug_ref[...], sgub_ref[...])
    rows = []
    for c in range(TQ // CHUNK):
        heads = [_dot(sguw_ref[hd], vn[c * CHUNK:(c + 1) * CHUNK, hd * SGU_HC:(hd + 1) * SGU_HC])
                 for hd in range(SGU_HEADS)]
        rows.append(jnp.concatenate(heads, axis=-1) + sgubias_ref[...])
    yb = u * jnp.concatenate(rows, axis=0)

    ca = _dot(xn, win_ref[:, 3 * W_BR:4 * W_BR])
    cg = _dot(xn, win_ref[:, 4 * W_BR:5 * W_BR])
    cbuf[CONV_PAD:CONV_PAD + TQ, :] = ca * jax.nn.sigmoid(cg)
    for lb in range(W_BR // LANES):
        sl = slice(lb * LANES, (lb + 1) * LANES)
        for rb in range(TQ // CONV_RB):
            r0 = CONV_PAD + rb * CONV_RB
            acc = jnp.broadcast_to(convb_ref[:, sl], (CONV_RB, LANES))
            for j in range(CONV_WIDTH):
                acc = acc + cbuf[r0 - j:r0 - j + CONV_RB, sl] * convw_ref[CONV_HIST - j:CONV_WIDTH - j, sl]
            ybuf[rb * CONV_RB:(rb + 1) * CONV_RB, sl] = acc
    yc = jax.nn.silu(_layer_norm(ybuf[...], cng_ref[...], cnb_ref[...]))

    @pl.when(i == nt - 1)
    def _():
        cstate_ref[0] = cbuf[CONV_PAD + TQ - CONV_HIST:CONV_PAD + TQ, :]

    cbuf[0:CONV_PAD, :] = cbuf[TQ:TQ + CONV_PAD, :]

    _mixer_tail(x, xn, ya, yb, yc, win_ref, wbp_ref, wbs_ref, wbc_ref, wout_ref, gffn_ref,
                wrt_ref, brt_ref, ltri_ref, cnt_acc, hout_ref, hn_ref, route_ref)
    cnt_ref[...] = cnt_acc[...]


def _mixer_sample_kernel(*refs, combine, nseq, nstep):
    n_in = 3 if combine else 1
    in_refs = refs[:n_in]
    (sp_ref, sc_ref, gmix_ref, win_ref, poolw_ref, pscale_ref, sgug_ref, sgub_ref, sguw_ref, sgubias_ref,
     convw_ref, convb_ref, cng_ref, cnb_ref, wbp_ref, wbs_ref, wbc_ref, wout_ref,
     gffn_ref, wrt_ref, brt_ref, ltri_ref,
     hout_ref, hn_ref, route_ref, cnt_ref, pnew_ref, cnew_ref, vn_ref,
     ybuf, cnt_acc) = refs[n_in:]
    n = nseq * nstep
    cnt_acc[...] = jnp.zeros_like(cnt_acc)
    x = _mixer_input(in_refs, combine, n)
    xn = _rms_norm(x, gmix_ref[...]).astype(_MXU)

    def slab(a, t):
        return a[t * nseq:(t + 1) * nseq]

    p = _dot(xn, win_ref[:, 0:W_BR])
    pnew_ref[...] = p
    ya_rows = []
    for t in range(nstep):
        parts = []
        for gi, w in enumerate(POOL_WINDOWS):
            sl = slice(gi * POOL_GC, (gi + 1) * POOL_GC)
            acc = slab(p, t)[:, sl]
            for j in range(1, w):
                k = POOL_HIST + t - j
                acc = acc + (sp_ref[k, :, sl] if k < POOL_HIST else slab(p, k - POOL_HIST)[:, sl])
            cnt = float(min(PAST_LEN + t + 1, w))
            d = acc / cnt - slab(p, t)[:, sl]
            parts.append(_dot(d, poolw_ref[gi]))
        ya_rows.append(jnp.concatenate(parts, axis=-1))
    ya = jnp.concatenate(ya_rows, axis=0) * pscale_ref[...]

    u = _dot(xn, win_ref[:, W_BR:2 * W_BR])
    v = _dot(xn, win_ref[:, 2 * W_BR:3 * W_BR])
    vn = _layer_norm(v, sgug_ref[...], sgub_ref[...])
    vn_ref[...] = vn
    s_rows = []
    for t in range(nstep):
        acc = jnp.broadcast_to(sgubias_ref[t:t + 1, :], (nseq, W_BR))
        for s in range(t + 1):
            acc = acc + sguw_ref[t * nstep + s:t * nstep + s + 1, :] * slab(vn, s)
        s_rows.append(acc)
    yb = u * jnp.concatenate(s_rows, axis=0)

    ca = _dot(xn, win_ref[:, 3 * W_BR:4 * W_BR])
    cg = _dot(xn, win_ref[:, 4 * W_BR:5 * W_BR])
    cnew_ref[...] = ca * jax.nn.sigmoid(cg)
    for t in range(nstep):
        for lb in range(W_BR // LANES):
            sl = slice(lb * LANES, (lb + 1) * LANES)
            acc = jnp.broadcast_to(convb_ref[:, sl], (nseq, LANES))
            for kw in range(CONV_WIDTH):
                k = t + kw
                row = sc_ref[k, :, sl] if k < CONV_HIST else cnew_ref[(k - CONV_HIST) * nseq:(k - CONV_HIST + 1) * nseq, sl]
                acc = acc + row * convw_ref[kw:kw + 1, sl]
            ybuf[t * nseq:(t + 1) * nseq, sl] = acc
    yc = jax.nn.silu(_layer_norm(ybuf[...], cng_ref[...], cnb_ref[...]))

    _mixer_tail(x, xn, ya, yb, yc, win_ref, wbp_ref, wbs_ref, wbc_ref, wout_ref, gffn_ref,
                wrt_ref, brt_ref, ltri_ref, cnt_acc, hout_ref, hn_ref, route_ref)
    cnt_ref[...] = cnt_acc[...]


def _lane_to_sublane(row):
    sub = lax.broadcasted_iota(jnp.int32, (LANES, LANES), 0)
    lane = lax.broadcasted_iota(jnp.int32, (LANES, LANES), 1)
    return jnp.sum(jnp.where(sub == lane, jnp.broadcast_to(row, (LANES, LANES)), 0.0), axis=1, keepdims=True)


def _plan_kernel(route_ref, cnt_ref, pos_ref, binfo_ref, *, nblk_lanes):
    cnt = cnt_ref[...]
    padded = jnp.floor((cnt + (MOE_BLOCK - 1)) * (1.0 / MOE_BLOCK)) * MOE_BLOCK
    lane1 = lax.broadcasted_iota(jnp.int32, (SUBLANES, LANES), 1)
    pends = jnp.broadcast_to(padded, (SUBLANES, LANES))
    sh = 1
    while sh < LANES:
        pends = pends + jnp.where(lane1 >= sh, pltpu.roll(pends, sh, axis=1), 0.0)
        sh *= 2
    pends = pends[0:1]
    pstart = pends - padded

    n = route_ref.shape[0]
    lane = lax.broadcasted_iota(jnp.int32, (n, LANES), 1).astype(_F32)
    for k in range(2):
        e_lane = route_ref[:, k:k + 1] + EXP_LANE0
        base = jnp.sum(jnp.where(lane == e_lane, pstart, 0.0), axis=-1, keepdims=True)
        pos_ref[:, k:k + 1] = (base + route_ref[:, 2 + k:3 + k]).astype(jnp.int32)

    @pl.when(pl.program_id(0) == 0)
    def _():
        pends_c = _lane_to_sublane(pends)
        pstart_c = _lane_to_sublane(pstart)
        cnt_c = _lane_to_sublane(cnt)
        sub = lax.broadcasted_iota(jnp.int32, (LANES, LANES), 0)
        is_exp = (sub >= EXP_LANE0) & (sub < EXP_LANE0 + N_EXPERTS)
        binfo_ref[...] = jnp.zeros_like(binfo_ref)
        for q in range(nblk_lanes // LANES):
            bstart = ((lax.broadcasted_iota(jnp.int32, (LANES, LANES), 1) + q * LANES) * MOE_BLOCK).astype(_F32)
            done = jnp.where(is_exp & (pends_c <= bstart), 1.0, 0.0)
            bexp = jnp.minimum(jnp.sum(done, axis=0, keepdims=True), float(N_EXPERTS - 1))
            inside = is_exp & (pstart_c <= bstart) & (bstart < pends_c)
            valid = jnp.clip(pstart_c + cnt_c - bstart, 0.0, float(MOE_BLOCK))
            nval = jnp.sum(jnp.where(inside, valid, 0.0), axis=0, keepdims=True)
            binfo_ref[0:1, q * LANES:(q + 1) * LANES] = bexp.astype(jnp.int32)
            binfo_ref[1:2, q * LANES:(q + 1) * LANES] = nval.astype(jnp.int32)


def _rows_kernel(pos0_ref, pos1_ref, src_pad_hbm, dst_pad_hbm, src_ref, dst_ref, sem, *, n_tok):
    unroll = 8
    fill_src = pltpu.make_async_copy(src_pad_hbm, src_ref, sem.at[0])
    fill_dst = pltpu.make_async_copy(dst_pad_hbm, dst_ref, sem.at[1])
    fill_src.start()
    fill_dst.start()
    fill_src.wait()
    fill_dst.wait()

    def put(i, c):
        for k in range(unroll):
            t = i * unroll + k
            p0 = pos0_ref[t]
            p1 = pos1_ref[t]
            src_ref[p0] = t
            src_ref[p1] = t
            dst_ref[p0] = 2 * t
            dst_ref[p1] = 2 * t + 1
        return c
    lax.fori_loop(0, n_tok // unroll, put, 0)


def _moe_kernel(be_ref, nv_ref, src_ref, dst_ref, hn_ref, wg_ref, wu_ref, wd_ref, o2_hbm,
                xg, obuf, wg_bf, wu_bf, wd_bf, ssem, *, n_tok):
    b = pl.program_id(0)
    nblk = pl.num_programs(0)
    slot = b % 2
    tile = SUBLANES
    spare0 = 2 * n_tok * tile

    def scatter_copy(row, dst, s):
        return pltpu.make_async_copy(obuf.at[s, pl.ds(row * tile, tile)],
                                     o2_hbm.at[pl.ds(pl.multiple_of(dst * tile, tile), tile)], ssem.at[s])

    def spare_copy(half):
        return pltpu.make_async_copy(obuf.at[1], o2_hbm.at[pl.ds(spare0 + half * MOE_BLOCK * tile, MOE_BLOCK * tile)],
                                     ssem.at[1])

    def gather_block(blk):
        for r in range(MOE_BLOCK):
            tok = src_ref[blk * MOE_BLOCK + r]
            xg[pl.ds(r, PACK_ROWS, stride=XG_STRIDE), :] = hn_ref[pl.ds(pl.multiple_of(tok * PACK_ROWS, PACK_ROWS),
                                                                         PACK_ROWS), :]
        lo, hi = [], []
        for j in range(PACK_ROWS):
            words = xg[j * XG_STRIDE:j * XG_STRIDE + MOE_BLOCK, :]
            lo.append(lax.bitcast_convert_type(jnp.left_shift(words, jnp.uint32(16)), _F32))
            hi.append(lax.bitcast_convert_type(jnp.bitwise_and(words, jnp.uint32(0xFFFF0000)), _F32))
        return jnp.concatenate(lo + hi, axis=-1).astype(_MXU)

    def start_scatter(blk, s):
        for r in range(MOE_BLOCK):
            scatter_copy(r, dst_ref[blk * MOE_BLOCK + r], s).start(priority=r % 2)

    def wait_scatter(s):
        for r in range(MOE_BLOCK):
            scatter_copy(0, 0, s).wait()

    def used(blk):
        return nv_ref[jnp.clip(blk, 0, nblk - 1)] > 0

    @pl.when(b == 0)
    def _():
        obuf[1] = jnp.zeros(obuf.shape[1:], _F32)
        spare_copy(0).start()
        spare_copy(1).start()
        spare_copy(0).wait()
        spare_copy(1).wait()

    @pl.when((b >= 2) & used(b - 2))
    def _():
        wait_scatter(slot)

    @pl.when(used(b))
    def _():
        @pl.when((b == 0) | (be_ref[b] != be_ref[jnp.maximum(b - 1, 0)]))
        def _():
            wg_bf[...] = wg_ref[...].astype(_MXU)
            wu_bf[...] = wu_ref[...].astype(_MXU)
            wd_bf[...] = wd_ref[...].astype(_MXU)

        x = gather_block(b)
        hb = jax.nn.silu(_dot(x, wg_bf[...])) * _dot(x, wu_bf[...])
        o = _dot(hb, wd_bf[...])
        for j in range(SUBLANES):
            obuf[slot, pl.ds(j, MOE_BLOCK, stride=SUBLANES), :] = o[:, j * LANES:(j + 1) * LANES]
        start_scatter(b, slot)

    @pl.when(b == nblk - 1)
    def _():
        @pl.when((b >= 1) & used(b - 1))
        def _():
            wait_scatter(1 - slot)

        @pl.when(used(b))
        def _():
            wait_scatter(slot)


def _final_kernel(h_ref, o2_ref, rprev_ref, g_ref, y_ref):
    n = h_ref.shape[0]
    y_ref[...] = _rms_norm(_mixer_input((h_ref, o2_ref, rprev_ref), True, n), g_ref[...])


def _stacked_weights(w):
    nstep = w['nstep']
    depth = w['w_in'].shape[0]
    sgu_w = w['sgu_w']
    sgu_b = w['sgu_b']
    vec = lambda a: a[:, None, :]
    pad = jnp.zeros((depth, D_MODEL, LANES - N_GROUPS - N_EXPERTS), _F32)
    w_rt = jnp.concatenate([w['w_group'], w['w_router'], pad], axis=-1)
    b_rt = jnp.concatenate([w['b_group'], w['b_router'], pad[:, 0, :]], axis=-1)
    head = (vec(w['mix_norm_g']), w['w_in'].astype(_MXU), w['pool_w'].astype(_MXU),
            vec(w['pool_scale']), vec(w['sgu_norm_g']), vec(w['sgu_norm_b']))
    tail = (w['conv_w'], vec(w['conv_b']), vec(w['conv_norm_g']), vec(w['conv_norm_b']),
            w['w_branch_pool'].astype(_MXU), w['w_branch_sgu'].astype(_MXU),
            w['w_branch_conv'].astype(_MXU), w['w_out'].astype(_MXU),
            vec(w['ffn_norm_g']), w_rt.astype(_MXU), vec(b_rt))
    sgu_p = (jnp.tril(sgu_w).astype(_MXU), jnp.repeat(sgu_b.transpose(0, 2, 1), SGU_HC, axis=2))
    corner = jnp.tril(sgu_w[:, :, :nstep, :nstep]).transpose(0, 2, 3, 1)
    sgu_s = (jnp.repeat(corner.reshape(depth, nstep * nstep, SGU_HEADS), SGU_HC, axis=2),
             jnp.repeat(sgu_b[:, :, :nstep].transpose(0, 2, 1), SGU_HC, axis=2))
    return head + sgu_p + tail, head + sgu_s + tail


def _weight_specs(ops, l):
    def spec(o):
        nd = o.ndim - 1
        return pl.BlockSpec((None,) + o.shape[1:], lambda *_: (l,) + (0,) * nd, pipeline_mode=pl.Buffered(1))
    return [spec(o) for o in ops]


def _ltri(n):
    r = lax.broadcasted_iota(jnp.int32, (n, n), 0)
    c = lax.broadcasted_iota(jnp.int32, (n, n), 1)
    return (c < r).astype(_MXU)


def _mixer_prompt(inputs, sample, ops, l, *, nb, seq, n_all):
    combine = len(inputs) == 3
    nt = seq // TQ
    n_main = nb * nt
    n_steps = n_all // TQ
    h_s, hn_s, route_s, cnt_s = sample
    row = lambda s: (s, 0)
    main_row = lambda s: (jnp.minimum(s, n_main - 1), 0)
    tail_row = lambda s: (jnp.maximum(s - n_main, 0), 0)
    seq_blk = lambda s: (jnp.minimum(s // nt, nb - 1), 0, 0)
    if combine:
        in_specs = [pl.BlockSpec((TQ, D_MODEL), main_row), pl.BlockSpec((TQ * 2 * SUBLANES, LANES), main_row),
                    pl.BlockSpec((TQ, ROUTE_COLS), main_row)]
    else:
        in_specs = [pl.BlockSpec((TQ, D_MODEL), main_row)]
    in_specs += [pl.BlockSpec((TQ, D_MODEL), tail_row), pl.BlockSpec((TQ * PACK_ROWS, LANES), tail_row),
                 pl.BlockSpec((TQ, ROUTE_COLS), tail_row), _const_spec(cnt_s.shape)]
    ltri = _ltri(TQ)
    out_shape = (jax.ShapeDtypeStruct((n_all, D_MODEL), _F32),
                 jax.ShapeDtypeStruct((n_all * PACK_ROWS, LANES), jnp.uint32),
                 jax.ShapeDtypeStruct((n_all, ROUTE_COLS), _F32),
                 jax.ShapeDtypeStruct((1, LANES), _F32),
                 jax.ShapeDtypeStruct((nb, POOL_HIST, W_BR), _F32),
                 jax.ShapeDtypeStruct((nb, CONV_HIST, W_BR), _F32))
    out_specs = (pl.BlockSpec((TQ, D_MODEL), row), pl.BlockSpec((TQ * PACK_ROWS, LANES), row),
                 pl.BlockSpec((TQ, ROUTE_COLS), row), pl.BlockSpec((1, LANES), lambda s: (0, 0)),
                 pl.BlockSpec((1, POOL_HIST, W_BR), seq_blk), pl.BlockSpec((1, CONV_HIST, W_BR), seq_blk))
    return pl.pallas_call(
        functools.partial(_mixer_prompt_kernel, combine=combine, nt=nt, n_main=n_main),
        grid=(n_steps,),
        in_specs=in_specs + _weight_specs(ops, l) + [_const_spec(ltri.shape)],
        out_specs=out_specs,
        out_shape=out_shape,
        scratch_shapes=[pltpu.VMEM((POOL_PAD + TQ, W_BR), _F32), pltpu.VMEM((CONV_PAD + TQ, W_BR), _F32),
                        pltpu.VMEM((TQ, W_BR), _F32), pltpu.VMEM((1, LANES), _F32)],
        compiler_params=pltpu.CompilerParams(dimension_semantics=("arbitrary",), vmem_limit_bytes=VMEM_LIMIT),
        name="mixer_prompt",
    )(*inputs, h_s, hn_s, route_s, cnt_s, *ops, ltri)


def _mixer_sample(inputs, sp, sc, ops, l, *, nseq, nstep, n_prompt):
    combine = len(inputs) == 3
    n = nseq * nstep
    blk = n_prompt // n
    row = lambda i: (blk, 0)
    zero = lambda i: (0, 0)
    if combine:
        in_specs = [pl.BlockSpec((n, D_MODEL), row), pl.BlockSpec((n * 2 * SUBLANES, LANES), row),
                    pl.BlockSpec((n, ROUTE_COLS), row)]
    else:
        in_specs = [pl.BlockSpec((n, D_MODEL), zero)]
    in_specs += _weight_specs((sp, sc), l)
    ltri = _ltri(n)
    out_shape = (jax.ShapeDtypeStruct((n, D_MODEL), _F32), jax.ShapeDtypeStruct((n * PACK_ROWS, LANES), jnp.uint32),
                 jax.ShapeDtypeStruct((n, ROUTE_COLS), _F32), jax.ShapeDtypeStruct((1, LANES), _F32),
                 jax.ShapeDtypeStruct((n, W_BR), _F32), jax.ShapeDtypeStruct((n, W_BR), _F32),
                 jax.ShapeDtypeStruct((n, W_BR), _F32))
    out_specs = (pl.BlockSpec((n, D_MODEL), zero), pl.BlockSpec((n * PACK_ROWS, LANES), zero),
                 pl.BlockSpec((n, ROUTE_COLS), zero), pl.BlockSpec((1, LANES), zero),
                 pl.BlockSpec((n, W_BR), zero), pl.BlockSpec((n, W_BR), zero), pl.BlockSpec((n, W_BR), zero))
    return pl.pallas_call(
        functools.partial(_mixer_sample_kernel, combine=combine, nseq=nseq, nstep=nstep),
        grid=(1,),
        in_specs=in_specs + _weight_specs(ops, l) + [_const_spec(ltri.shape)],
        out_specs=out_specs,
        out_shape=out_shape,
        scratch_shapes=[pltpu.VMEM((n, W_BR), _F32), pltpu.VMEM((1, LANES), _F32)],
        compiler_params=pltpu.CompilerParams(dimension_semantics=("arbitrary",), vmem_limit_bytes=VMEM_LIMIT),
        name="mixer_sample",
    )(*inputs, sp, sc, *ops, ltri)


def _plan(route, cnt, *, n_blocks):
    n_all = route.shape[0]
    tile = 512
    nblk_lanes = -(-n_blocks // LANES) * LANES
    return pl.pallas_call(
        functools.partial(_plan_kernel, nblk_lanes=nblk_lanes),
        grid=(n_all // tile,),
        in_specs=[pl.BlockSpec((tile, ROUTE_COLS), lambda i: (i, 0)), pl.BlockSpec((1, LANES), lambda i: (0, 0))],
        out_specs=(pl.BlockSpec((tile, 2), lambda i: (i, 0)), pl.BlockSpec((SUBLANES, nblk_lanes), lambda i: (0, 0))),
        out_shape=(jax.ShapeDtypeStruct((n_all, 2), jnp.int32), jax.ShapeDtypeStruct((SUBLANES, nblk_lanes), jnp.int32)),
        compiler_params=pltpu.CompilerParams(dimension_semantics=("arbitrary",)),
        name="moe_plan",
    )(route, cnt)


def _rows(pos0, pos1, *, n_rows):
    n_tok = pos0.shape[0]
    src_pad = jnp.full((n_rows,), n_tok - 1, jnp.int32)
    dst_pad = 2 * n_tok + jnp.bitwise_and(jnp.arange(n_rows, dtype=jnp.int32), 2 * MOE_BLOCK - 1)
    smem = pl.BlockSpec(memory_space=pltpu.SMEM)
    hbm = pl.BlockSpec(memory_space=pl.ANY)
    return pl.pallas_call(
        functools.partial(_rows_kernel, n_tok=n_tok),
        in_specs=[smem, smem, hbm, hbm],
        out_specs=(smem, smem),
        out_shape=(jax.ShapeDtypeStruct((n_rows,), jnp.int32), jax.ShapeDtypeStruct((n_rows,), jnp.int32)),
        scratch_shapes=[pltpu.SemaphoreType.DMA((2,))],
        name="moe_rows",
    )(pos0, pos1, src_pad, dst_pad)


def _moe(be, nv, src, dst, hn_all, wg, wu, wd, l, *, n_blocks):
    n_all = hn_all.shape[0] // PACK_ROWS
    n_tiles = 2 * n_all + 2 * MOE_BLOCK
    wspec_in = pl.BlockSpec((None, None, D_MODEL, D_EXPERT), lambda b, be, *_: (l, be[b], 0, 0))
    wspec_out = pl.BlockSpec((None, None, D_EXPERT, D_MODEL), lambda b, be, *_: (l, be[b], 0, 0))
    grid_spec = pltpu.PrefetchScalarGridSpec(
        num_scalar_prefetch=4,
        grid=(n_blocks,),
        in_specs=[_const_spec(hn_all.shape), wspec_in, wspec_in, wspec_out],
        out_specs=pl.BlockSpec(memory_space=pl.ANY),
        scratch_shapes=[pltpu.VMEM((PACK_ROWS * XG_STRIDE, LANES), jnp.uint32),
                        pltpu.VMEM((2, MOE_BLOCK * SUBLANES, LANES), _F32),
                        pltpu.VMEM((D_MODEL, D_EXPERT), _MXU), pltpu.VMEM((D_MODEL, D_EXPERT), _MXU),
                        pltpu.VMEM((D_EXPERT, D_MODEL), _MXU),
                        pltpu.SemaphoreType.DMA((2,))])
    return pl.pallas_call(
        functools.partial(_moe_kernel, n_tok=n_all),
        grid_spec=grid_spec,
        out_shape=jax.ShapeDtypeStruct((n_tiles * SUBLANES, LANES), _F32),
        compiler_params=pltpu.CompilerParams(dimension_semantics=("arbitrary",), vmem_limit_bytes=VMEM_LIMIT),
        name="moe_experts",
    )(be, nv, src, dst, hn_all, wg, wu, wd)


def _final(h_all, o2, route, g, *, row0, n, tile):
    blk0 = row0 // tile
    row = lambda i: (blk0 + i, 0)
    return pl.pallas_call(
        _final_kernel,
        grid=(n // tile,),
        in_specs=[pl.BlockSpec((tile, D_MODEL), row), pl.BlockSpec((tile * 2 * SUBLANES, LANES), row),
                  pl.BlockSpec((tile, ROUTE_COLS), row), pl.BlockSpec((1, D_MODEL), lambda i: (0, 0))],
        out_specs=pl.BlockSpec((tile, D_MODEL), lambda i: (i, 0)),
        out_shape=jax.ShapeDtypeStruct((n, D_MODEL), _F32),
        compiler_params=pltpu.CompilerParams(dimension_semantics=("arbitrary",)),
        name="final_norm",
    )(h_all, o2, route, g)


def kernel(x_prompt, x_sample, state_pool, state_conv, mix_norm_g, w_in, pool_w, pool_scale, sgu_norm_g, sgu_norm_b, sgu_w, sgu_b, conv_w, conv_b, conv_norm_g, conv_norm_b, w_branch_pool, w_branch_sgu, w_branch_conv, w_out, ffn_norm_g, w_group, b_group, w_router, b_router, w_gate_e, w_up_e, w_down_e, final_norm_g):
    nb, seq, _ = x_prompt.shape
    nseq, nstep, _ = x_sample.shape
    n_prompt = nb * seq
    n_sample = nseq * nstep
    n_all = n_prompt + n_sample
    assert seq % TQ == 0 and TQ % CHUNK == 0 and nstep < CHUNK and n_sample % TQ == 0
    assert n_prompt % n_sample == 0 and n_all % 512 == 0 and n_sample % MOE_BLOCK == 0
    n_blocks = (2 * n_all) // MOE_BLOCK + N_EXPERTS
    n_rows = n_blocks * MOE_BLOCK

    w = dict(nstep=nstep, mix_norm_g=mix_norm_g, w_in=w_in, pool_w=pool_w, pool_scale=pool_scale,
             sgu_norm_g=sgu_norm_g, sgu_norm_b=sgu_norm_b, sgu_w=sgu_w, sgu_b=sgu_b, conv_w=conv_w, conv_b=conv_b,
             conv_norm_g=conv_norm_g, conv_norm_b=conv_norm_b, w_branch_pool=w_branch_pool,
             w_branch_sgu=w_branch_sgu, w_branch_conv=w_branch_conv, w_out=w_out, ffn_norm_g=ffn_norm_g,
             w_group=w_group, b_group=b_group, w_router=w_router, b_router=b_router)

    xp = x_prompt.reshape(n_prompt, D_MODEL)
    xs = x_sample.transpose(1, 0, 2).reshape(n_sample, D_MODEL)
    sp_t = state_pool.transpose(0, 2, 1, 3)
    sc_t = state_conv.transpose(0, 2, 1, 3)

    pool_p, pool_s, conv_p, conv_s, chunk_v = [], [], [], [], []
    prev = None
    ops_p, ops_s = _stacked_weights(w)
    for l in range(DEPTH):
        in_p = (xp,) if prev is None else prev
        in_s = (xs,) if prev is None else prev
        h_s, hn_s, route_s, cnt_s, pnew, cnew, vn = _mixer_sample(in_s, sp_t, sc_t, ops_s, l, nseq=nseq,
                                                                  nstep=nstep, n_prompt=n_prompt)
        h, hn, route, cnt, pstate, cstate = _mixer_prompt(in_p, (h_s, hn_s, route_s, cnt_s), ops_p, l,
                                                          nb=nb, seq=seq, n_all=n_all)
        pos, binfo = _plan(route, cnt, n_blocks=n_blocks)
        src, dst = _rows(pos[:, 0], pos[:, 1], n_rows=n_rows)
        o2 = _moe(binfo[0, :n_blocks], binfo[1, :n_blocks], src, dst, hn, w_gate_e, w_up_e, w_down_e, l,
                  n_blocks=n_blocks)
        prev = (h, o2, route)
        to_seq = lambda a: a.reshape(nstep, nseq, W_BR).transpose(1, 0, 2)
        pool_p.append(pstate)
        conv_p.append(cstate)
        pool_s.append(jnp.concatenate([state_pool[l][:, nstep:], to_seq(pnew)], axis=1))
        conv_s.append(jnp.concatenate([state_conv[l][:, nstep:], to_seq(cnew)], axis=1))
        chunk_v.append(to_seq(vn))

    g = final_norm_g[None]
    y_prompt = _final(*prev, g, row0=0, n=n_prompt, tile=512).reshape(nb, seq, D_MODEL)
    y_sample = _final(*prev, g, row0=n_prompt, n=n_sample, tile=n_sample)
    y_sample = y_sample.reshape(nstep, nseq, D_MODEL).transpose(1, 0, 2)
    return (y_prompt, y_sample, jnp.stack(pool_p), jnp.stack(pool_s), jnp.stack(conv_p), jnp.stack(conv_s),
            jnp.stack(chunk_v))
```

```python
import functools

import jax
import jax.numpy as jnp
from jax import lax
from jax.experimental import pallas as pl
from jax.experimental.pallas import tpu as pltpu

D_MODEL = 1024
DEPTH = 4
PAST_LEN = 16384
W_BR = 512
POOL_WINDOWS = (2, 4, 8, 16)
POOL_GC = 128
POOL_HIST = 15
SGU_HEADS = 4
SGU_HC = 128
CHUNK = 128
CONV_WIDTH = 31
CONV_HIST = 30
N_GROUPS = 4
EPG = 8
N_EXPERTS = 32
D_EXPERT = 512
MOE_BLOCK = 128
EPS = 1e-6

LANES = 128
SUBLANES = 8
VMEM_LIMIT = 56 * 1024 * 1024

TQ = 256
ROUTE_COLS = 8
EXP_LANE0 = N_GROUPS
CONV_RB = 64
PACK_ROWS = D_MODEL // 2 // LANES
XG_STRIDE = MOE_BLOCK + SUBLANES
POOL_PAD = 16
CONV_PAD = 32

_MXU = jnp.bfloat16
_F32 = jnp.float32


def _dot(a, b):
    return jnp.dot(a.astype(_MXU), b.astype(_MXU), preferred_element_type=_F32)


def _rms_norm(x, g):
    return x * lax.rsqrt(jnp.mean(x * x, axis=-1, keepdims=True) + EPS) * g


def _layer_norm(x, g, b):
    mu = jnp.mean(x, axis=-1, keepdims=True)
    xc = x - mu
    var = jnp.mean(xc * xc, axis=-1, keepdims=True)
    return xc * lax.rsqrt(var + EPS) * g + b


def _const_spec(shape):
    nd = len(shape)
    return pl.BlockSpec(shape, lambda *_: (0,) * nd, pipeline_mode=pl.Buffered(1))


def _tile_rows_to_tokens(ref, start, n, stride):
    return jnp.concatenate([ref[pl.ds(start + j, n, stride=stride), :] for j in range(SUBLANES)], axis=-1)


def _mixer_input(in_refs, combine, n):
    if not combine:
        return in_refs[0][...]
    h_ref, o2a_ref, o2b_ref, rprev_ref = in_refs
    g0 = rprev_ref[:, 4:5]
    g1 = rprev_ref[:, 5:6]
    o0 = _tile_rows_to_tokens(o2a_ref, 0, n, SUBLANES)
    o1 = _tile_rows_to_tokens(o2b_ref, 0, n, SUBLANES)
    return h_ref[...] + (g0 * o0 + g1 * o1)


def _route(hn, wrt_ref, brt_ref, ltri_ref, cnt_acc, route_ref):
    n = hn.shape[0]
    logits = _dot(hn, wrt_ref[...]) + brt_ref[...]
    lane = lax.broadcasted_iota(jnp.int32, (n, LANES), 1).astype(_F32)
    neg = jnp.float32(-jnp.inf)
    is_grp = lane < N_GROUPS
    gl = jnp.where(is_grp, logits, neg)
    gmax = jnp.max(gl, axis=-1, keepdims=True)
    grp = jnp.min(jnp.where(gl == gmax, lane, float(LANES)), axis=-1, keepdims=True)
    gsum = jnp.sum(jnp.where(is_grp, jnp.exp(logits - gmax), 0.0), axis=-1, keepdims=True)
    grp_w = 1.0 / gsum
    lo = EXP_LANE0 + grp * EPG
    el = jnp.where((lane >= lo) & (lane < lo + EPG), logits, neg)
    v0 = jnp.max(el, axis=-1, keepdims=True)
    i0 = jnp.min(jnp.where(el == v0, lane, float(LANES)), axis=-1, keepdims=True)
    el1 = jnp.where(lane == i0, neg, el)
    v1 = jnp.max(el1, axis=-1, keepdims=True)
    i1 = jnp.min(jnp.where(el1 == v1, lane, float(LANES)), axis=-1, keepdims=True)
    ex = jnp.exp(v1 - v0)
    den = 1.0 + ex
    g0 = grp_w / den
    g1 = grp_w * ex / den
    oh0 = lane == i0
    oh1 = lane == i1
    oh = jnp.where(oh0 | oh1, 1.0, 0.0)
    before = _dot(ltri_ref[...], oh) + cnt_acc[...]
    r0 = jnp.sum(jnp.where(oh0, before, 0.0), axis=-1, keepdims=True)
    r1 = jnp.sum(jnp.where(oh1, before, 0.0), axis=-1, keepdims=True)
    cnt_acc[...] = cnt_acc[...] + jnp.sum(oh, axis=0, keepdims=True)
    route_ref[:, 0:1] = i0 - EXP_LANE0
    route_ref[:, 1:2] = i1 - EXP_LANE0
    route_ref[:, 2:3] = r0
    route_ref[:, 3:4] = r1
    route_ref[:, 4:5] = g0
    route_ref[:, 5:6] = g1
    route_ref[:, 6:8] = jnp.zeros((n, 2), _F32)


def _gate_slab(xn, win_ref, k, nslab):
    width = 3 * D_MODEL // nslab
    lo = 5 * W_BR + k * width
    return _dot(xn, win_ref[:, lo:lo + width])


def _mixer_tail(x, xn, ya, yb, yc, gate_pre, wbp_ref, wbs_ref, wbc_ref, wout_ref, gffn_ref,
                wrt_ref, brt_ref, ltri_ref, cnt_acc, hout_ref, hn_ref, route_ref):
    n = x.shape[0]
    gates = jax.nn.sigmoid(gate_pre)
    merged = gates[:, 0:D_MODEL] * _dot(ya, wbp_ref[...])
    merged = merged + gates[:, D_MODEL:2 * D_MODEL] * _dot(yb, wbs_ref[...])
    merged = merged + gates[:, 2 * D_MODEL:3 * D_MODEL] * _dot(yc, wbc_ref[...])
    h = x + _dot(merged, wout_ref[...])
    hout_ref[...] = h
    hn = _rms_norm(h, gffn_ref[...])
    half = D_MODEL // 2
    lo = lax.bitcast_convert_type(hn[:, :half].astype(_MXU).astype(_F32), jnp.uint32)
    hi = lax.bitcast_convert_type(hn[:, half:].astype(_MXU).astype(_F32), jnp.uint32)
    words = jnp.bitwise_or(jnp.bitwise_and(hi, jnp.uint32(0xFFFF0000)), jnp.right_shift(lo, jnp.uint32(16)))
    for j in range(PACK_ROWS):
        hn_ref[pl.ds(j, n, stride=PACK_ROWS), :] = words[:, j * LANES:(j + 1) * LANES]
    _route(hn, wrt_ref, brt_ref, ltri_ref, cnt_acc, route_ref)


def _mixer_prompt_kernel(*refs, combine, nt, n_main):
    n_in = 4 if combine else 1
    in_refs = refs[:n_in]
    hs_ref, hns_ref, routes_ref = refs[n_in:n_in + 3]
    hout_ref, hn_ref, route_ref = refs[-10:-7]
    step = pl.program_id(0)

    @pl.when(step >= n_main)
    def _():
        hout_ref[...] = hs_ref[...]
        hn_ref[...] = hns_ref[...]
        route_ref[...] = routes_ref[...]

    @pl.when(step < n_main)
    def _():
        _mixer_prompt_step(in_refs, refs[n_in + 3:], combine, nt)


def _mixer_prompt_step(in_refs, refs, combine, nt):
    (cnt_in_ref, gmix_ref, win_ref, poolw_ref, pscale_ref, sgug_ref, sgub_ref, sguw_ref, sgubias_ref,
     convw_ref, convb_ref, cng_ref, cnb_ref, wbp_ref, wbs_ref, wbc_ref, wout_ref,
     gffn_ref, wrt_ref, brt_ref, ltri_ref,
     hout_ref, hn_ref, route_ref, cnt_ref, pstate_ref, cstate_ref,
     pbuf, cbuf, ybuf, cnt_acc) = refs
    step = pl.program_id(0)
    i = lax.rem(step, nt)

    @pl.when(step == 0)
    def _():
        cnt_acc[...] = cnt_in_ref[...]

    @pl.when(i == 0)
    def _():
        pbuf[0:POOL_PAD, :] = jnp.zeros((POOL_PAD, W_BR), _F32)
        cbuf[0, 0:CONV_PAD, :] = jnp.zeros((CONV_PAD, W_BR), _F32)

    x = _mixer_input(in_refs, combine, TQ)
    xn = _rms_norm(x, gmix_ref[...]).astype(_MXU)

    ca = _dot(xn, win_ref[:, 3 * W_BR:4 * W_BR])
    cg = _dot(xn, win_ref[:, 4 * W_BR:5 * W_BR])
    cbuf[0, CONV_PAD:CONV_PAD + TQ, :] = ca * jax.nn.sigmoid(cg)
    for s in range(1, SUBLANES):
        cbuf[s, SUBLANES:CONV_PAD + TQ, :] = cbuf[0, SUBLANES - s:CONV_PAD + TQ - s, :]
    n_lb = W_BR // LANES
    other = [lambda: _dot(xn, win_ref[:, 0:W_BR]), lambda: _dot(xn, win_ref[:, W_BR:2 * W_BR]),
             lambda: _dot(xn, win_ref[:, 2 * W_BR:3 * W_BR]), lambda: _gate_slab(xn, win_ref, 0, 1)]
    done = []
    for lb in range(n_lb):
        sl = slice(lb * LANES, (lb + 1) * LANES)
        for rb in range(TQ // CONV_RB):
            r0 = CONV_PAD + rb * CONV_RB
            acc = jnp.broadcast_to(convb_ref[:, sl], (CONV_RB, LANES))
            for j in range(CONV_WIDTH):
                q, s = divmod(j, SUBLANES)
                rows = cbuf[s, r0 - q * SUBLANES:r0 - q * SUBLANES + CONV_RB, sl]
                acc = acc + rows * convw_ref[CONV_HIST - j:CONV_WIDTH - j, sl]
            ybuf[rb * CONV_RB:(rb + 1) * CONV_RB, sl] = acc
        done.append(other[lb]())
    p, u, v, gate_pre = done
    yc = jax.nn.silu(_layer_norm(ybuf[...], cng_ref[...], cnb_ref[...]))
    cstate_ref[0] = cbuf[0, CONV_PAD + TQ - CONV_HIST:CONV_PAD + TQ, :]
    cbuf[0, 0:CONV_PAD, :] = cbuf[0, TQ:TQ + CONV_PAD, :]

    pbuf[POOL_PAD:POOL_PAD + TQ, :] = p
    pos = i * TQ + lax.broadcasted_iota(jnp.int32, (TQ, 1), 0)
    ya_parts = []
    for gi, w in enumerate(POOL_WINDOWS):
        sl = slice(gi * POOL_GC, (gi + 1) * POOL_GC)
        acc = p[:, sl]
        for j in range(1, w):
            acc = acc + pbuf[POOL_PAD - j:POOL_PAD - j + TQ, sl]
        cnt = jnp.minimum(pos + 1, w).astype(_F32)
        d = acc / cnt - p[:, sl]
        ya_parts.append(_dot(d, poolw_ref[gi]))
    ya = jnp.concatenate(ya_parts, axis=-1) * pscale_ref[...]
    pstate_ref[0] = pbuf[POOL_PAD + TQ - POOL_HIST:POOL_PAD + TQ, :]
    pbuf[0:POOL_PAD, :] = pbuf[TQ:TQ + POOL_PAD, :]

    vn = _layer_norm(v, sgug_ref[...], sgub_ref[...])
    rows = []
    for c in range(TQ // CHUNK):
        heads = [_dot(sguw_ref[hd], vn[c * CHUNK:(c + 1) * CHUNK, hd * SGU_HC:(hd + 1) * SGU_HC])
                 for hd in range(SGU_HEADS)]
        rows.append(jnp.concatenate(heads, axis=-1) + sgubias_ref[...])
    yb = u * jnp.concatenate(rows, axis=0)

    _mixer_tail(x, xn, ya, yb, yc, gate_pre, wbp_ref, wbs_ref, wbc_ref, wout_ref,
                gffn_ref, wrt_ref, brt_ref, ltri_ref, cnt_acc, hout_ref, hn_ref, route_ref)
    cnt_ref[...] = cnt_acc[...]


def _mixer_sample_kernel(*refs, combine, nseq, nstep):
    n_in = 4 if combine else 1
    in_refs = refs[:n_in]
    (sp_ref, sc_ref, gmix_ref, win_ref, poolw_ref, pscale_ref, sgug_ref, sgub_ref, sguw_ref, sgubias_ref,
     convw_ref, convb_ref, cng_ref, cnb_ref, wbp_ref, wbs_ref, wbc_ref, wout_ref,
     gffn_ref, wrt_ref, brt_ref, ltri_ref,
     hout_ref, hn_ref, route_ref, cnt_ref, pnew_ref, cnew_ref, vn_ref,
     ybuf, cnt_acc) = refs[n_in:]
    n = nseq * nstep
    cnt_acc[...] = jnp.zeros_like(cnt_acc)
    x = _mixer_input(in_refs, combine, n)
    xn = _rms_norm(x, gmix_ref[...]).astype(_MXU)

    def slab(a, t):
        return a[t * nseq:(t + 1) * nseq]

    p = _dot(xn, win_ref[:, 0:W_BR])
    pnew_ref[...] = p
    ya_rows = []
    for t in range(nstep):
        parts = []
        for gi, w in enumerate(POOL_WINDOWS):
            sl = slice(gi * POOL_GC, (gi + 1) * POOL_GC)
            acc = slab(p, t)[:, sl]
            for j in range(1, w):
                k = POOL_HIST + t - j
                acc = acc + (sp_ref[k, :, sl] if k < POOL_HIST else slab(p, k - POOL_HIST)[:, sl])
            cnt = float(min(PAST_LEN + t + 1, w))
            d = acc / cnt - slab(p, t)[:, sl]
            parts.append(_dot(d, poolw_ref[gi]))
        ya_rows.append(jnp.concatenate(parts, axis=-1))
    ya = jnp.concatenate(ya_rows, axis=0) * pscale_ref[...]

    u = _dot(xn, win_ref[:, W_BR:2 * W_BR])
    v = _dot(xn, win_ref[:, 2 * W_BR:3 * W_BR])
    vn = _layer_norm(v, sgug_ref[...], sgub_ref[...])
    vn_ref[...] = vn
    s_rows = []
    for t in range(nstep):
        acc = jnp.broadcast_to(sgubias_ref[t:t + 1, :], (nseq, W_BR))
        for s in range(t + 1):
            acc = acc + sguw_ref[t * nstep + s:t * nstep + s + 1, :] * slab(vn, s)
        s_rows.append(acc)
    yb = u * jnp.concatenate(s_rows, axis=0)

    ca = _dot(xn, win_ref[:, 3 * W_BR:4 * W_BR])
    cg = _dot(xn, win_ref[:, 4 * W_BR:5 * W_BR])
    cnew_ref[...] = ca * jax.nn.sigmoid(cg)
    for t in range(nstep):
        for lb in range(W_BR // LANES):
            sl = slice(lb * LANES, (lb + 1) * LANES)
            acc = jnp.broadcast_to(convb_ref[:, sl], (nseq, LANES))
            for kw in range(CONV_WIDTH):
                k = t + kw
                row = sc_ref[k, :, sl] if k < CONV_HIST else cnew_ref[(k - CONV_HIST) * nseq:(k - CONV_HIST + 1) * nseq, sl]
                acc = acc + row * convw_ref[kw:kw + 1, sl]
            ybuf[t * nseq:(t + 1) * nseq, sl] = acc
    yc = jax.nn.silu(_layer_norm(ybuf[...], cng_ref[...], cnb_ref[...]))

    _mixer_tail(x, xn, ya, yb, yc, _gate_slab(xn, win_ref, 0, 1), wbp_ref, wbs_ref, wbc_ref, wout_ref,
                gffn_ref, wrt_ref, brt_ref, ltri_ref, cnt_acc, hout_ref, hn_ref, route_ref)
    cnt_ref[...] = cnt_acc[...]


def _lane_to_sublane(row):
    sub = lax.broadcasted_iota(jnp.int32, (LANES, LANES), 0)
    lane = lax.broadcasted_iota(jnp.int32, (LANES, LANES), 1)
    return jnp.sum(jnp.where(sub == lane, jnp.broadcast_to(row, (LANES, LANES)), 0.0), axis=1, keepdims=True)


def _plan_kernel(route_ref, cnt_ref, pos_ref, binfo_ref, *, nblk_lanes):
    cnt = cnt_ref[...]
    padded = jnp.floor((cnt + (MOE_BLOCK - 1)) * (1.0 / MOE_BLOCK)) * MOE_BLOCK
    lane1 = lax.broadcasted_iota(jnp.int32, (SUBLANES, LANES), 1)
    pends = jnp.broadcast_to(padded, (SUBLANES, LANES))
    sh = 1
    while sh < LANES:
        pends = pends + jnp.where(lane1 >= sh, pltpu.roll(pends, sh, axis=1), 0.0)
        sh *= 2
    pends = pends[0:1]
    pstart = pends - padded

    n = route_ref.shape[0]
    lane = lax.broadcasted_iota(jnp.int32, (n, LANES), 1).astype(_F32)
    for k in range(2):
        e_lane = route_ref[:, k:k + 1] + EXP_LANE0
        base = jnp.sum(jnp.where(lane == e_lane, pstart, 0.0), axis=-1, keepdims=True)
        pos_ref[:, k:k + 1] = (base + route_ref[:, 2 + k:3 + k]).astype(jnp.int32)

    @pl.when(pl.program_id(0) == 0)
    def _():
        pends_c = _lane_to_sublane(pends)
        pstart_c = _lane_to_sublane(pstart)
        cnt_c = _lane_to_sublane(cnt)
        sub = lax.broadcasted_iota(jnp.int32, (LANES, LANES), 0)
        is_exp = (sub >= EXP_LANE0) & (sub < EXP_LANE0 + N_EXPERTS)
        binfo_ref[...] = jnp.zeros_like(binfo_ref)
        for q in range(nblk_lanes // LANES):
            bstart = ((lax.broadcasted_iota(jnp.int32, (LANES, LANES), 1) + q * LANES) * MOE_BLOCK).astype(_F32)
            done = jnp.where(is_exp & (pends_c <= bstart), 1.0, 0.0)
            bexp = jnp.minimum(jnp.sum(done, axis=0, keepdims=True), float(N_EXPERTS - 1))
            inside = is_exp & (pstart_c <= bstart) & (bstart < pends_c)
            valid = jnp.clip(pstart_c + cnt_c - bstart, 0.0, float(MOE_BLOCK))
            nval = jnp.sum(jnp.where(inside, valid, 0.0), axis=0, keepdims=True)
            binfo_ref[0:1, q * LANES:(q + 1) * LANES] = bexp.astype(jnp.int32)
            binfo_ref[1:2, q * LANES:(q + 1) * LANES] = nval.astype(jnp.int32)


def _rows_kernel(pos0_ref, pos1_ref, src_pad_hbm, dst_pad_hbm, src_ref, dst_ref, sem, *, n_tok):
    unroll = 8
    fill_src = pltpu.make_async_copy(src_pad_hbm, src_ref, sem.at[0])
    fill_dst = pltpu.make_async_copy(dst_pad_hbm, dst_ref, sem.at[1])
    fill_src.start()
    fill_dst.start()
    fill_src.wait()
    fill_dst.wait()

    def put(i, c):
        for k in range(unroll):
            t = i * unroll + k
            p0 = pos0_ref[t]
            p1 = pos1_ref[t]
            src_ref[p0] = t
            src_ref[p1] = t
            dst_ref[p0] = t
            dst_ref[p1] = n_tok + t
        return c
    lax.fori_loop(0, n_tok // unroll, put, 0)


def _moe_kernel(be_ref, nv_ref, src_ref, dst_ref, hn_ref, wg_ref, wu_ref, wd_ref, o2_hbm,
                xg, obuf, wg_bf, wu_bf, wd_bf, ssem, *, n_tok):
    b = pl.program_id(0)
    nblk = pl.num_programs(0)
    slot = b % 2
    tile = SUBLANES
    spare0 = 2 * n_tok * tile

    def scatter_copy(row, dst, s):
        return pltpu.make_async_copy(obuf.at[s, pl.ds(row * tile, tile)],
                                     o2_hbm.at[pl.ds(pl.multiple_of(dst * tile, tile), tile)], ssem.at[s])

    def spare_copy(half):
        return pltpu.make_async_copy(obuf.at[1], o2_hbm.at[pl.ds(spare0 + half * MOE_BLOCK * tile, MOE_BLOCK * tile)],
                                     ssem.at[1])

    def gather_block(blk):
        for r in range(MOE_BLOCK):
            tok = src_ref[blk * MOE_BLOCK + r]
            xg[pl.ds(r, PACK_ROWS, stride=XG_STRIDE), :] = hn_ref[pl.ds(pl.multiple_of(tok * PACK_ROWS, PACK_ROWS),
                                                                         PACK_ROWS), :]
        lo, hi = [], []
        for j in range(PACK_ROWS):
            words = xg[j * XG_STRIDE:j * XG_STRIDE + MOE_BLOCK, :]
            lo.append(lax.bitcast_convert_type(jnp.left_shift(words, jnp.uint32(16)), _F32))
            hi.append(lax.bitcast_convert_type(jnp.bitwise_and(words, jnp.uint32(0xFFFF0000)), _F32))
        return jnp.concatenate(lo + hi, axis=-1).astype(_MXU)

    def start_scatter(blk, s):
        for r in range(MOE_BLOCK):
            scatter_copy(r, dst_ref[blk * MOE_BLOCK + r], s).start(priority=r % 2)

    def wait_scatter(s):
        for r in range(MOE_BLOCK):
            scatter_copy(0, 0, s).wait()

    def used(blk):
        return nv_ref[jnp.clip(blk, 0, nblk - 1)] > 0

    @pl.when(b == 0)
    def _():
        obuf[1] = jnp.zeros(obuf.shape[1:], _F32)
        spare_copy(0).start()
        spare_copy(1).start()
        spare_copy(0).wait()
        spare_copy(1).wait()

    @pl.when((b >= 2) & used(b - 2))
    def _():
        wait_scatter(slot)

    @pl.when(used(b))
    def _():
        @pl.when((b == 0) | (be_ref[b] != be_ref[jnp.maximum(b - 1, 0)]))
        def _():
            wg_bf[...] = wg_ref[...].astype(_MXU)
            wu_bf[...] = wu_ref[...].astype(_MXU)
            wd_bf[...] = wd_ref[...].astype(_MXU)

        x = gather_block(b)
        hb = jax.nn.silu(_dot(x, wg_bf[...])) * _dot(x, wu_bf[...])
        o = _dot(hb, wd_bf[...])
        for j in range(SUBLANES):
            obuf[slot, pl.ds(j, MOE_BLOCK, stride=SUBLANES), :] = o[:, j * LANES:(j + 1) * LANES]
        start_scatter(b, slot)

    @pl.when(b == nblk - 1)
    def _():
        @pl.when((b >= 1) & used(b - 1))
        def _():
            wait_scatter(1 - slot)

        @pl.when(used(b))
        def _():
            wait_scatter(slot)


def _final_kernel(h_ref, o2a_ref, o2b_ref, rprev_ref, g_ref, y_ref):
    n = h_ref.shape[0]
    y_ref[...] = _rms_norm(_mixer_input((h_ref, o2a_ref, o2b_ref, rprev_ref), True, n), g_ref[...])


def _stacked_weights(w):
    nstep = w['nstep']
    depth = w['w_in'].shape[0]
    sgu_w = w['sgu_w']
    sgu_b = w['sgu_b']
    vec = lambda a: a[:, None, :]
    pad = jnp.zeros((depth, D_MODEL, LANES - N_GROUPS - N_EXPERTS), _F32)
    w_rt = jnp.concatenate([w['w_group'], w['w_router'], pad], axis=-1)
    b_rt = jnp.concatenate([w['b_group'], w['b_router'], pad[:, 0, :]], axis=-1)
    head = (vec(w['mix_norm_g']), w['w_in'].astype(_MXU), w['pool_w'].astype(_MXU),
            vec(w['pool_scale']), vec(w['sgu_norm_g']), vec(w['sgu_norm_b']))
    tail = (w['conv_w'], vec(w['conv_b']), vec(w['conv_norm_g']), vec(w['conv_norm_b']),
            w['w_branch_pool'].astype(_MXU), w['w_branch_sgu'].astype(_MXU),
            w['w_branch_conv'].astype(_MXU), w['w_out'].astype(_MXU),
            vec(w['ffn_norm_g']), w_rt.astype(_MXU), vec(b_rt))
    sgu_p = (jnp.tril(sgu_w).astype(_MXU), jnp.repeat(sgu_b.transpose(0, 2, 1), SGU_HC, axis=2))
    corner = jnp.tril(sgu_w[:, :, :nstep, :nstep]).transpose(0, 2, 3, 1)
    sgu_s = (jnp.repeat(corner.reshape(depth, nstep * nstep, SGU_HEADS), SGU_HC, axis=2),
             jnp.repeat(sgu_b[:, :, :nstep].transpose(0, 2, 1), SGU_HC, axis=2))
    return head + sgu_p + tail, head + sgu_s + tail


def _weight_specs(ops, l):
    def spec(o):
        nd = o.ndim - 1
        return pl.BlockSpec((None,) + o.shape[1:], lambda *_: (l,) + (0,) * nd, pipeline_mode=pl.Buffered(1))
    return [spec(o) for o in ops]


def _ltri(n):
    r = lax.broadcasted_iota(jnp.int32, (n, n), 0)
    c = lax.broadcasted_iota(jnp.int32, (n, n), 1)
    return (c < r).astype(_MXU)


def _prev_layer_specs(tile, index, n_all):
    def slot1(*ids):
        i, z = index(*ids)
        return i + n_all // tile, z
    return [pl.BlockSpec((tile, D_MODEL), index), pl.BlockSpec((tile * SUBLANES, LANES), index),
            pl.BlockSpec((tile * SUBLANES, LANES), slot1), pl.BlockSpec((tile, ROUTE_COLS), index)]


def _prev_layer_args(prev):
    h, o2, route = prev
    return (h, o2, o2, route)


def _mixer_prompt(prev, x, sample, ops, l, *, nb, seq, n_all):
    combine = prev is not None
    nt = seq // TQ
    n_main = nb * nt
    n_steps = n_all // TQ
    h_s, hn_s, route_s, cnt_s = sample
    row = lambda s: (s, 0)
    main_row = lambda s: (jnp.minimum(s, n_main - 1), 0)
    tail_row = lambda s: (jnp.maximum(s - n_main, 0), 0)
    seq_blk = lambda s: (jnp.minimum(s // nt, nb - 1), 0, 0)
    if combine:
        inputs = _prev_layer_args(prev)
        in_specs = _prev_layer_specs(TQ, main_row, n_all)
    else:
        inputs = (x,)
        in_specs = [pl.BlockSpec((TQ, D_MODEL), main_row)]
    in_specs += [pl.BlockSpec((TQ, D_MODEL), tail_row), pl.BlockSpec((TQ * PACK_ROWS, LANES), tail_row),
                 pl.BlockSpec((TQ, ROUTE_COLS), tail_row), _const_spec(cnt_s.shape)]
    ltri = _ltri(TQ)
    out_shape = (jax.ShapeDtypeStruct((n_all, D_MODEL), _F32),
                 jax.ShapeDtypeStruct((n_all * PACK_ROWS, LANES), jnp.uint32),
                 jax.ShapeDtypeStruct((n_all, ROUTE_COLS), _F32),
                 jax.ShapeDtypeStruct((1, LANES), _F32),
                 jax.ShapeDtypeStruct((nb, POOL_HIST, W_BR), _F32),
                 jax.ShapeDtypeStruct((nb, CONV_HIST, W_BR), _F32))
    out_specs = (pl.BlockSpec((TQ, D_MODEL), row), pl.BlockSpec((TQ * PACK_ROWS, LANES), row),
                 pl.BlockSpec((TQ, ROUTE_COLS), row), pl.BlockSpec((1, LANES), lambda s: (0, 0)),
                 pl.BlockSpec((1, POOL_HIST, W_BR), seq_blk), pl.BlockSpec((1, CONV_HIST, W_BR), seq_blk))
    return pl.pallas_call(
        functools.partial(_mixer_prompt_kernel, combine=combine, nt=nt, n_main=n_main),
        grid=(n_steps,),
        in_specs=in_specs + _weight_specs(ops, l) + [_const_spec(ltri.shape)],
        out_specs=out_specs,
        out_shape=out_shape,
        scratch_shapes=[pltpu.VMEM((POOL_PAD + TQ, W_BR), _F32), pltpu.VMEM((SUBLANES, CONV_PAD + TQ, W_BR), _F32),
                        pltpu.VMEM((TQ, W_BR), _F32), pltpu.VMEM((1, LANES), _F32)],
        compiler_params=pltpu.CompilerParams(dimension_semantics=("arbitrary",), vmem_limit_bytes=VMEM_LIMIT),
        name="mixer_prompt",
    )(*inputs, h_s, hn_s, route_s, cnt_s, *ops, ltri)


def _mixer_sample(prev, x, sp, sc, ops, l, *, nseq, nstep, n_prompt):
    combine = prev is not None
    n = nseq * nstep
    blk = n_prompt // n
    row = lambda i: (blk, 0)
    zero = lambda i: (0, 0)
    if combine:
        inputs = _prev_layer_args(prev)
        in_specs = _prev_layer_specs(n, row, n_prompt + n)
    else:
        inputs = (x,)
        in_specs = [pl.BlockSpec((n, D_MODEL), zero)]
    in_specs += _weight_specs((sp, sc), l)
    ltri = _ltri(n)
    out_shape = (jax.ShapeDtypeStruct((n, D_MODEL), _F32), jax.ShapeDtypeStruct((n * PACK_ROWS, LANES), jnp.uint32),
                 jax.ShapeDtypeStruct((n, ROUTE_COLS), _F32), jax.ShapeDtypeStruct((1, LANES), _F32),
                 jax.ShapeDtypeStruct((n, W_BR), _F32), jax.ShapeDtypeStruct((n, W_BR), _F32),
                 jax.ShapeDtypeStruct((n, W_BR), _F32))
    out_specs = (pl.BlockSpec((n, D_MODEL), zero), pl.BlockSpec((n * PACK_ROWS, LANES), zero),
                 pl.BlockSpec((n, ROUTE_COLS), zero), pl.BlockSpec((1, LANES), zero),
                 pl.BlockSpec((n, W_BR), zero), pl.BlockSpec((n, W_BR), zero), pl.BlockSpec((n, W_BR), zero))
    return pl.pallas_call(
        functools.partial(_mixer_sample_kernel, combine=combine, nseq=nseq, nstep=nstep),
        grid=(1,),
        in_specs=in_specs + _weight_specs(ops, l) + [_const_spec(ltri.shape)],
        out_specs=out_specs,
        out_shape=out_shape,
        scratch_shapes=[pltpu.VMEM((n, W_BR), _F32), pltpu.VMEM((1, LANES), _F32)],
        compiler_params=pltpu.CompilerParams(dimension_semantics=("arbitrary",), vmem_limit_bytes=VMEM_LIMIT),
        name="mixer_sample",
    )(*inputs, sp, sc, *ops, ltri)


def _plan(route, cnt, *, n_blocks):
    n_all = route.shape[0]
    tile = 512
    nblk_lanes = -(-n_blocks // LANES) * LANES
    return pl.pallas_call(
        functools.partial(_plan_kernel, nblk_lanes=nblk_lanes),
        grid=(n_all // tile,),
        in_specs=[pl.BlockSpec((tile, ROUTE_COLS), lambda i: (i, 0)), pl.BlockSpec((1, LANES), lambda i: (0, 0))],
        out_specs=(pl.BlockSpec((tile, 2), lambda i: (i, 0)), pl.BlockSpec((SUBLANES, nblk_lanes), lambda i: (0, 0))),
        out_shape=(jax.ShapeDtypeStruct((n_all, 2), jnp.int32), jax.ShapeDtypeStruct((SUBLANES, nblk_lanes), jnp.int32)),
        compiler_params=pltpu.CompilerParams(dimension_semantics=("arbitrary",)),
        name="moe_plan",
    )(route, cnt)


def _rows(pos0, pos1, *, n_rows):
    n_tok = pos0.shape[0]
    src_pad = jnp.full((n_rows,), n_tok - 1, jnp.int32)
    dst_pad = 2 * n_tok + jnp.bitwise_and(jnp.arange(n_rows, dtype=jnp.int32), 2 * MOE_BLOCK - 1)
    smem = pl.BlockSpec(memory_space=pltpu.SMEM)
    hbm = pl.BlockSpec(memory_space=pl.ANY)
    return pl.pallas_call(
        functools.partial(_rows_kernel, n_tok=n_tok),
        in_specs=[smem, smem, hbm, hbm],
        out_specs=(smem, smem),
        out_shape=(jax.ShapeDtypeStruct((n_rows,), jnp.int32), jax.ShapeDtypeStruct((n_rows,), jnp.int32)),
        scratch_shapes=[pltpu.SemaphoreType.DMA((2,))],
        name="moe_rows",
    )(pos0, pos1, src_pad, dst_pad)


def _moe(be, nv, src, dst, hn_all, wg, wu, wd, l, *, n_blocks):
    n_all = hn_all.shape[0] // PACK_ROWS
    n_tiles = 2 * n_all + 2 * MOE_BLOCK
    wspec_in = pl.BlockSpec((None, None, D_MODEL, D_EXPERT), lambda b, be, *_: (l, be[b], 0, 0))
    wspec_out = pl.BlockSpec((None, None, D_EXPERT, D_MODEL), lambda b, be, *_: (l, be[b], 0, 0))
    grid_spec = pltpu.PrefetchScalarGridSpec(
        num_scalar_prefetch=4,
        grid=(n_blocks,),
        in_specs=[_const_spec(hn_all.shape), wspec_in, wspec_in, wspec_out],
        out_specs=pl.BlockSpec(memory_space=pl.ANY),
        scratch_shapes=[pltpu.VMEM((PACK_ROWS * XG_STRIDE, LANES), jnp.uint32),
                        pltpu.VMEM((2, MOE_BLOCK * SUBLANES, LANES), _F32),
                        pltpu.VMEM((D_MODEL, D_EXPERT), _MXU), pltpu.VMEM((D_MODEL, D_EXPERT), _MXU),
                        pltpu.VMEM((D_EXPERT, D_MODEL), _MXU),
                        pltpu.SemaphoreType.DMA((2,))])
    return pl.pallas_call(
        functools.partial(_moe_kernel, n_tok=n_all),
        grid_spec=grid_spec,
        out_shape=jax.ShapeDtypeStruct((n_tiles * SUBLANES, LANES), _F32),
        compiler_params=pltpu.CompilerParams(dimension_semantics=("arbitrary",), vmem_limit_bytes=VMEM_LIMIT),
        name="moe_experts",
    )(be, nv, src, dst, hn_all, wg, wu, wd)


def _final(prev, g, *, row0, n, tile):
    n_all = prev[0].shape[0]
    blk0 = row0 // tile
    row = lambda i: (blk0 + i, 0)
    return pl.pallas_call(
        _final_kernel,
        grid=(n // tile,),
        in_specs=_prev_layer_specs(tile, row, n_all) + [pl.BlockSpec((1, D_MODEL), lambda i: (0, 0))],
        out_specs=pl.BlockSpec((tile, D_MODEL), lambda i: (i, 0)),
        out_shape=jax.ShapeDtypeStruct((n, D_MODEL), _F32),
        compiler_params=pltpu.CompilerParams(dimension_semantics=("arbitrary",)),
        name="final_norm",
    )(*_prev_layer_args(prev), g)


def kernel(x_prompt, x_sample, state_pool, state_conv, mix_norm_g, w_in, pool_w, pool_scale, sgu_norm_g, sgu_norm_b, sgu_w, sgu_b, conv_w, conv_b, conv_norm_g, conv_norm_b, w_branch_pool, w_branch_sgu, w_branch_conv, w_out, ffn_norm_g, w_group, b_group, w_router, b_router, w_gate_e, w_up_e, w_down_e, final_norm_g):
    nb, seq, _ = x_prompt.shape
    nseq, nstep, _ = x_sample.shape
    n_prompt = nb * seq
    n_sample = nseq * nstep
    n_all = n_prompt + n_sample
    assert seq % TQ == 0 and TQ % CHUNK == 0 and nstep < CHUNK and n_sample % TQ == 0
    assert n_prompt % n_sample == 0 and n_all % 512 == 0 and n_sample % MOE_BLOCK == 0
    n_blocks = (2 * n_all) // MOE_BLOCK + N_EXPERTS
    n_rows = n_blocks * MOE_BLOCK

    w = dict(nstep=nstep, mix_norm_g=mix_norm_g, w_in=w_in, pool_w=pool_w, pool_scale=pool_scale,
             sgu_norm_g=sgu_norm_g, sgu_norm_b=sgu_norm_b, sgu_w=sgu_w, sgu_b=sgu_b, conv_w=conv_w, conv_b=conv_b,
             conv_norm_g=conv_norm_g, conv_norm_b=conv_norm_b, w_branch_pool=w_branch_pool,
             w_branch_sgu=w_branch_sgu, w_branch_conv=w_branch_conv, w_out=w_out, ffn_norm_g=ffn_norm_g,
             w_group=w_group, b_group=b_group, w_router=w_router, b_router=b_router)

    xp = x_prompt.reshape(n_prompt, D_MODEL)
    xs = x_sample.transpose(1, 0, 2).reshape(n_sample, D_MODEL)
    sp_t = state_pool.transpose(0, 2, 1, 3)
    sc_t = state_conv.transpose(0, 2, 1, 3)

    pool_p, pool_s, conv_p, conv_s, chunk_v = [], [], [], [], []
    prev = None
    ops_p, ops_s = _stacked_weights(w)
    for l in range(DEPTH):
        h_s, hn_s, route_s, cnt_s, pnew, cnew, vn = _mixer_sample(prev, xs, sp_t, sc_t, ops_s, l, nseq=nseq,
                                                                  nstep=nstep, n_prompt=n_prompt)
        h, hn, route, cnt, pstate, cstate = _mixer_prompt(prev, xp, (h_s, hn_s, route_s, cnt_s), ops_p, l,
                                                          nb=nb, seq=seq, n_all=n_all)
        pos, binfo = _plan(route, cnt, n_blocks=n_blocks)
        src, dst = _rows(pos[:, 0], pos[:, 1], n_rows=n_rows)
        o2 = _moe(binfo[0, :n_blocks], binfo[1, :n_blocks], src, dst, hn, w_gate_e, w_up_e, w_down_e, l,
                  n_blocks=n_blocks)
        prev = (h, o2, route)
        to_seq = lambda a: a.reshape(nstep, nseq, W_BR).transpose(1, 0, 2)
        pool_p.append(pstate)
        conv_p.append(cstate)
        pool_s.append(jnp.concatenate([state_pool[l][:, nstep:], to_seq(pnew)], axis=1))
        conv_s.append(jnp.concatenate([state_conv[l][:, nstep:], to_seq(cnew)], axis=1))
        chunk_v.append(to_seq(vn))

    g = final_norm_g[None]
    y_prompt = _final(prev, g, row0=0, n=n_prompt, tile=512).reshape(nb, seq, D_MODEL)
    y_sample = _final(prev, g, row0=n_prompt, n=n_sample, tile=n_sample)
    y_sample = y_sample.reshape(nstep, nseq, D_MODEL).transpose(1, 0, 2)
    return (y_prompt, y_sample, jnp.stack(pool_p), jnp.stack(pool_s), jnp.stack(conv_p), jnp.stack(conv_s),
            jnp.stack(chunk_v))
```

```python
import functools

import jax
import jax.numpy as jnp
from jax import lax
from jax.experimental import pallas as pl
from jax.experimental.pallas import tpu as pltpu

D_MODEL = 1024
DEPTH = 4
PAST_LEN = 16384
W_BR = 512
POOL_WINDOWS = (2, 4, 8, 16)
POOL_GC = 128
POOL_HIST = 15
SGU_HEADS = 4
SGU_HC = 128
CHUNK = 128
CONV_WIDTH = 31
CONV_HIST = 30
N_GROUPS = 4
EPG = 8
N_EXPERTS = 32
D_EXPERT = 512
MOE_BLOCK = 128
EPS = 1e-6

LANES = 128
SUBLANES = 8
VMEM_LIMIT = 56 * 1024 * 1024

TQ = 256
ROUTE_COLS = 8
EXP_LANE0 = N_GROUPS
CONV_RB = 64
PACK_ROWS = D_MODEL // 2 // LANES
XG_STRIDE = MOE_BLOCK + SUBLANES
SCATTER_PARTS = 4
POOL_PAD = 16
CONV_PAD = 32

_MXU = jnp.bfloat16
_F32 = jnp.float32


def _dot(a, b):
    return jnp.dot(a.astype(_MXU), b.astype(_MXU), preferred_element_type=_F32)


def _rms_norm(x, g):
    return x * lax.rsqrt(jnp.mean(x * x, axis=-1, keepdims=True) + EPS) * g


def _layer_norm(x, g, b):
    mu = jnp.mean(x, axis=-1, keepdims=True)
    xc = x - mu
    var = jnp.mean(xc * xc, axis=-1, keepdims=True)
    return xc * lax.rsqrt(var + EPS) * g + b


def _const_spec(shape):
    nd = len(shape)
    return pl.BlockSpec(shape, lambda *_: (0,) * nd, pipeline_mode=pl.Buffered(1))


def _tile_rows_to_tokens(ref, start, n, stride):
    return jnp.concatenate([ref[pl.ds(start + j, n, stride=stride), :] for j in range(SUBLANES)], axis=-1)


def _mixer_input(in_refs, combine, n):
    if not combine:
        return in_refs[0][...]
    h_ref, o2a_ref, o2b_ref, rprev_ref = in_refs
    g0 = rprev_ref[:, 4:5]
    g1 = rprev_ref[:, 5:6]
    o0 = _tile_rows_to_tokens(o2a_ref, 0, n, SUBLANES)
    o1 = _tile_rows_to_tokens(o2b_ref, 0, n, SUBLANES)
    return h_ref[...] + (g0 * o0 + g1 * o1)


def _route(hn, wrt_ref, brt_ref, ltri_ref, cnt_acc, route_ref):
    n = hn.shape[0]
    logits = _dot(hn, wrt_ref[...]) + brt_ref[...]
    lane = lax.broadcasted_iota(jnp.int32, (n, LANES), 1).astype(_F32)
    neg = jnp.float32(-jnp.inf)
    is_grp = lane < N_GROUPS
    gl = jnp.where(is_grp, logits, neg)
    gmax = jnp.max(gl, axis=-1, keepdims=True)
    grp = jnp.min(jnp.where(gl == gmax, lane, float(LANES)), axis=-1, keepdims=True)
    gsum = jnp.sum(jnp.where(is_grp, jnp.exp(logits - gmax), 0.0), axis=-1, keepdims=True)
    grp_w = 1.0 / gsum
    lo = EXP_LANE0 + grp * EPG
    el = jnp.where((lane >= lo) & (lane < lo + EPG), logits, neg)
    v0 = jnp.max(el, axis=-1, keepdims=True)
    i0 = jnp.min(jnp.where(el == v0, lane, float(LANES)), axis=-1, keepdims=True)
    el1 = jnp.where(lane == i0, neg, el)
    v1 = jnp.max(el1, axis=-1, keepdims=True)
    i1 = jnp.min(jnp.where(el1 == v1, lane, float(LANES)), axis=-1, keepdims=True)
    ex = jnp.exp(v1 - v0)
    den = 1.0 + ex
    g0 = grp_w / den
    g1 = grp_w * ex / den
    oh0 = lane == i0
    oh1 = lane == i1
    oh = jnp.where(oh0 | oh1, 1.0, 0.0)
    before = _dot(ltri_ref[...], oh) + cnt_acc[...]
    r0 = jnp.sum(jnp.where(oh0, before, 0.0), axis=-1, keepdims=True)
    r1 = jnp.sum(jnp.where(oh1, before, 0.0), axis=-1, keepdims=True)
    cnt_acc[...] = cnt_acc[...] + jnp.sum(oh, axis=0, keepdims=True)
    route_ref[:, 0:1] = i0 - EXP_LANE0
    route_ref[:, 1:2] = i1 - EXP_LANE0
    route_ref[:, 2:3] = r0
    route_ref[:, 3:4] = r1
    route_ref[:, 4:5] = g0
    route_ref[:, 5:6] = g1
    route_ref[:, 6:8] = jnp.zeros((n, 2), _F32)


def _gate_slab(xn, win_ref, k, nslab):
    width = 3 * D_MODEL // nslab
    lo = 5 * W_BR + k * width
    return _dot(xn, win_ref[:, lo:lo + width])


def _mixer_tail(x, xn, ya, yb, yc, gate_pre, wbp_ref, wbs_ref, wbc_ref, wout_ref, gffn_ref,
                wrt_ref, brt_ref, ltri_ref, cnt_acc, hout_ref, hn_ref, route_ref):
    n = x.shape[0]
    gates = jax.nn.sigmoid(gate_pre)
    merged = gates[:, 0:D_MODEL] * _dot(ya, wbp_ref[...])
    merged = merged + gates[:, D_MODEL:2 * D_MODEL] * _dot(yb, wbs_ref[...])
    merged = merged + gates[:, 2 * D_MODEL:3 * D_MODEL] * _dot(yc, wbc_ref[...])
    h = x + _dot(merged, wout_ref[...])
    hout_ref[...] = h
    hn = _rms_norm(h, gffn_ref[...])
    half = D_MODEL // 2
    lo = lax.bitcast_convert_type(hn[:, :half].astype(_MXU).astype(_F32), jnp.uint32)
    hi = lax.bitcast_convert_type(hn[:, half:].astype(_MXU).astype(_F32), jnp.uint32)
    words = jnp.bitwise_or(jnp.bitwise_and(hi, jnp.uint32(0xFFFF0000)), jnp.right_shift(lo, jnp.uint32(16)))
    for j in range(PACK_ROWS):
        hn_ref[pl.ds(j, n, stride=PACK_ROWS), :] = words[:, j * LANES:(j + 1) * LANES]
    _route(hn, wrt_ref, brt_ref, ltri_ref, cnt_acc, route_ref)


def _mixer_prompt_kernel(*refs, combine, nt, n_main):
    n_in = 4 if combine else 1
    in_refs = refs[:n_in]
    hs_ref, hns_ref, routes_ref = refs[n_in:n_in + 3]
    hout_ref, hn_ref, route_ref = refs[-10:-7]
    step = pl.program_id(0)

    @pl.when(step >= n_main)
    def _():
        hout_ref[...] = hs_ref[...]
        hn_ref[...] = hns_ref[...]
        route_ref[...] = routes_ref[...]

    @pl.when(step < n_main)
    def _():
        _mixer_prompt_step(in_refs, refs[n_in + 3:], combine, nt)


def _mixer_prompt_step(in_refs, refs, combine, nt):
    (cnt_in_ref, gmix_ref, win_ref, poolw_ref, pscale_ref, sgug_ref, sgub_ref, sguw_ref, sgubias_ref,
     convw_ref, convb_ref, cng_ref, cnb_ref, wbp_ref, wbs_ref, wbc_ref, wout_ref,
     gffn_ref, wrt_ref, brt_ref, ltri_ref,
     hout_ref, hn_ref, route_ref, cnt_ref, pstate_ref, cstate_ref,
     pbuf, cbuf, ybuf, cnt_acc) = refs
    step = pl.program_id(0)
    i = lax.rem(step, nt)

    @pl.when(step == 0)
    def _():
        cnt_acc[...] = cnt_in_ref[...]

    @pl.when(i == 0)
    def _():
        pbuf[0:POOL_PAD, :] = jnp.zeros((POOL_PAD, W_BR), _F32)
        cbuf[0, 0:CONV_PAD, :] = jnp.zeros((CONV_PAD, W_BR), _F32)

    x = _mixer_input(in_refs, combine, TQ)
    xn = _rms_norm(x, gmix_ref[...]).astype(_MXU)

    ca = _dot(xn, win_ref[:, 3 * W_BR:4 * W_BR])
    cg = _dot(xn, win_ref[:, 4 * W_BR:5 * W_BR])
    cbuf[0, CONV_PAD:CONV_PAD + TQ, :] = ca * jax.nn.sigmoid(cg)
    for s in range(1, SUBLANES):
        cbuf[s, SUBLANES:CONV_PAD + TQ, :] = cbuf[0, SUBLANES - s:CONV_PAD + TQ - s, :]
    n_lb = W_BR // LANES
    other = [lambda: _dot(xn, win_ref[:, 0:W_BR]), lambda: _dot(xn, win_ref[:, W_BR:2 * W_BR]),
             lambda: _dot(xn, win_ref[:, 2 * W_BR:3 * W_BR]), lambda: _gate_slab(xn, win_ref, 0, 1)]
    done = []
    for lb in range(n_lb):
        sl = slice(lb * LANES, (lb + 1) * LANES)
        for rb in range(TQ // CONV_RB):
            r0 = CONV_PAD + rb * CONV_RB
            acc = jnp.broadcast_to(convb_ref[:, sl], (CONV_RB, LANES))
            for j in range(CONV_WIDTH):
                q, s = divmod(j, SUBLANES)
                rows = cbuf[s, r0 - q * SUBLANES:r0 - q * SUBLANES + CONV_RB, sl]
                acc = acc + rows * convw_ref[CONV_HIST - j:CONV_WIDTH - j, sl]
            ybuf[rb * CONV_RB:(rb + 1) * CONV_RB, sl] = acc
        done.append(other[lb]())
    p, u, v, gate_pre = done
    yc = jax.nn.silu(_layer_norm(ybuf[...], cng_ref[...], cnb_ref[...]))
    cstate_ref[0] = cbuf[0, CONV_PAD + TQ - CONV_HIST:CONV_PAD + TQ, :]
    cbuf[0, 0:CONV_PAD, :] = cbuf[0, TQ:TQ + CONV_PAD, :]

    pbuf[POOL_PAD:POOL_PAD + TQ, :] = p
    pos = i * TQ + lax.broadcasted_iota(jnp.int32, (TQ, 1), 0)
    ya_parts = []
    for gi, w in enumerate(POOL_WINDOWS):
        sl = slice(gi * POOL_GC, (gi + 1) * POOL_GC)
        acc = p[:, sl]
        for j in range(1, w):
            acc = acc + pbuf[POOL_PAD - j:POOL_PAD - j + TQ, sl]
        cnt = jnp.minimum(pos + 1, w).astype(_F32)
        d = acc / cnt - p[:, sl]
        ya_parts.append(_dot(d, poolw_ref[gi]))
    ya = jnp.concatenate(ya_parts, axis=-1) * pscale_ref[...]
    pstate_ref[0] = pbuf[POOL_PAD + TQ - POOL_HIST:POOL_PAD + TQ, :]
    pbuf[0:POOL_PAD, :] = pbuf[TQ:TQ + POOL_PAD, :]

    vn = _layer_norm(v, sgug_ref[...], sgub_ref[...])
    rows = []
    for c in range(TQ // CHUNK):
        heads = [_dot(sguw_ref[hd], vn[c * CHUNK:(c + 1) * CHUNK, hd * SGU_HC:(hd + 1) * SGU_HC])
                 for hd in range(SGU_HEADS)]
        rows.append(jnp.concatenate(heads, axis=-1) + sgubias_ref[...])
    yb = u * jnp.concatenate(rows, axis=0)

    _mixer_tail(x, xn, ya, yb, yc, gate_pre, wbp_ref, wbs_ref, wbc_ref, wout_ref,
                gffn_ref, wrt_ref, brt_ref, ltri_ref, cnt_acc, hout_ref, hn_ref, route_ref)
    cnt_ref[...] = cnt_acc[...]


def _mixer_sample_kernel(*refs, combine, nseq, nstep):
    n_in = 4 if combine else 1
    in_refs = refs[:n_in]
    (sp_ref, sc_ref, gmix_ref, win_ref, poolw_ref, pscale_ref, sgug_ref, sgub_ref, sguw_ref, sgubias_ref,
     convw_ref, convb_ref, cng_ref, cnb_ref, wbp_ref, wbs_ref, wbc_ref, wout_ref,
     gffn_ref, wrt_ref, brt_ref, ltri_ref,
     hout_ref, hn_ref, route_ref, cnt_ref, pnew_ref, cnew_ref, vn_ref,
     ybuf, cnt_acc) = refs[n_in:]
    n = nseq * nstep
    cnt_acc[...] = jnp.zeros_like(cnt_acc)
    x = _mixer_input(in_refs, combine, n)
    xn = _rms_norm(x, gmix_ref[...]).astype(_MXU)

    def slab(a, t):
        return a[t * nseq:(t + 1) * nseq]

    p = _dot(xn, win_ref[:, 0:W_BR])
    pnew_ref[...] = p
    ya_rows = []
    for t in range(nstep):
        parts = []
        for gi, w in enumerate(POOL_WINDOWS):
            sl = slice(gi * POOL_GC, (gi + 1) * POOL_GC)
            acc = slab(p, t)[:, sl]
            for j in range(1, w):
                k = POOL_HIST + t - j
                acc = acc + (sp_ref[k, :, sl] if k < POOL_HIST else slab(p, k - POOL_HIST)[:, sl])
            cnt = float(min(PAST_LEN + t + 1, w))
            d = acc / cnt - slab(p, t)[:, sl]
            parts.append(_dot(d, poolw_ref[gi]))
        ya_rows.append(jnp.concatenate(parts, axis=-1))
    ya = jnp.concatenate(ya_rows, axis=0) * pscale_ref[...]

    u = _dot(xn, win_ref[:, W_BR:2 * W_BR])
    v = _dot(xn, win_ref[:, 2 * W_BR:3 * W_BR])
    vn = _layer_norm(v, sgug_ref[...], sgub_ref[...])
    vn_ref[...] = vn
    s_rows = []
    for t in range(nstep):
        acc = jnp.broadcast_to(sgubias_ref[t:t + 1, :], (nseq, W_BR))
        for s in range(t + 1):
            acc = acc + sguw_ref[t * nstep + s:t * nstep + s + 1, :] * slab(vn, s)
        s_rows.append(acc)
    yb = u * jnp.concatenate(s_rows, axis=0)

    ca = _dot(xn, win_ref[:, 3 * W_BR:4 * W_BR])
    cg = _dot(xn, win_ref[:, 4 * W_BR:5 * W_BR])
    cnew_ref[...] = ca * jax.nn.sigmoid(cg)
    for t in range(nstep):
        for lb in range(W_BR // LANES):
            sl = slice(lb * LANES, (lb + 1) * LANES)
            acc = jnp.broadcast_to(convb_ref[:, sl], (nseq, LANES))
            for kw in range(CONV_WIDTH):
                k = t + kw
                row = sc_ref[k, :, sl] if k < CONV_HIST else cnew_ref[(k - CONV_HIST) * nseq:(k - CONV_HIST + 1) * nseq, sl]
                acc = acc + row * convw_ref[kw:kw + 1, sl]
            ybuf[t * nseq:(t + 1) * nseq, sl] = acc
    yc = jax.nn.silu(_layer_norm(ybuf[...], cng_ref[...], cnb_ref[...]))

    _mixer_tail(x, xn, ya, yb, yc, _gate_slab(xn, win_ref, 0, 1), wbp_ref, wbs_ref, wbc_ref, wout_ref,
                gffn_ref, wrt_ref, brt_ref, ltri_ref, cnt_acc, hout_ref, hn_ref, route_ref)
    cnt_ref[...] = cnt_acc[...]


def _lane_to_sublane(row):
    sub = lax.broadcasted_iota(jnp.int32, (LANES, LANES), 0)
    lane = lax.broadcasted_iota(jnp.int32, (LANES, LANES), 1)
    return jnp.sum(jnp.where(sub == lane, jnp.broadcast_to(row, (LANES, LANES)), 0.0), axis=1, keepdims=True)


def _plan_kernel(route_ref, cnt_ref, pos_ref, binfo_ref, *, nblk_lanes):
    cnt = cnt_ref[...]
    padded = jnp.floor((cnt + (MOE_BLOCK - 1)) * (1.0 / MOE_BLOCK)) * MOE_BLOCK
    lane1 = lax.broadcasted_iota(jnp.int32, (SUBLANES, LANES), 1)
    pends = jnp.broadcast_to(padded, (SUBLANES, LANES))
    sh = 1
    while sh < LANES:
        pends = pends + jnp.where(lane1 >= sh, pltpu.roll(pends, sh, axis=1), 0.0)
        sh *= 2
    pends = pends[0:1]
    pstart = pends - padded

    n = route_ref.shape[0]
    lane = lax.broadcasted_iota(jnp.int32, (n, LANES), 1).astype(_F32)
    for k in range(2):
        e_lane = route_ref[:, k:k + 1] + EXP_LANE0
        base = jnp.sum(jnp.where(lane == e_lane, pstart, 0.0), axis=-1, keepdims=True)
        pos_ref[:, k:k + 1] = (base + route_ref[:, 2 + k:3 + k]).astype(jnp.int32)

    @pl.when(pl.program_id(0) == 0)
    def _():
        pends_c = _lane_to_sublane(pends)
        pstart_c = _lane_to_sublane(pstart)
        cnt_c = _lane_to_sublane(cnt)
        sub = lax.broadcasted_iota(jnp.int32, (LANES, LANES), 0)
        is_exp = (sub >= EXP_LANE0) & (sub < EXP_LANE0 + N_EXPERTS)
        binfo_ref[...] = jnp.zeros_like(binfo_ref)
        for q in range(nblk_lanes // LANES):
            bstart = ((lax.broadcasted_iota(jnp.int32, (LANES, LANES), 1) + q * LANES) * MOE_BLOCK).astype(_F32)
            done = jnp.where(is_exp & (pends_c <= bstart), 1.0, 0.0)
            bexp = jnp.minimum(jnp.sum(done, axis=0, keepdims=True), float(N_EXPERTS - 1))
            inside = is_exp & (pstart_c <= bstart) & (bstart < pends_c)
            valid = jnp.clip(pstart_c + cnt_c - bstart, 0.0, float(MOE_BLOCK))
            nval = jnp.sum(jnp.where(inside, valid, 0.0), axis=0, keepdims=True)
            binfo_ref[0:1, q * LANES:(q + 1) * LANES] = bexp.astype(jnp.int32)
            binfo_ref[1:2, q * LANES:(q + 1) * LANES] = nval.astype(jnp.int32)


def _rows_kernel(pos0_ref, pos1_ref, src_pad_hbm, dst_pad_hbm, src_ref, dst_ref, sem, *, n_tok):
    unroll = 8
    fill_src = pltpu.make_async_copy(src_pad_hbm, src_ref, sem.at[0])
    fill_dst = pltpu.make_async_copy(dst_pad_hbm, dst_ref, sem.at[1])
    fill_src.start()
    fill_dst.start()
    fill_src.wait()
    fill_dst.wait()

    def put(i, c):
        for k in range(unroll):
            t = i * unroll + k
            p0 = pos0_ref[t]
            p1 = pos1_ref[t]
            src_ref[p0] = t
            src_ref[p1] = t
            dst_ref[p0] = t
            dst_ref[p1] = n_tok + t
        return c
    lax.fori_loop(0, n_tok // unroll, put, 0)


def _moe_kernel(be_ref, nv_ref, src_ref, dst_ref, hn_ref, wg_ref, wu_ref, wd_ref, o2_hbm,
                xg, obuf, wg_bf, wu_bf, wd_bf, ssem, *, n_tok):
    b = pl.program_id(0)
    nblk = pl.num_programs(0)
    slot = b % 2
    tile = SUBLANES
    spare0 = 2 * n_tok * tile

    def scatter_copy(row, dst, s):
        return pltpu.make_async_copy(obuf.at[s, pl.ds(row * tile, tile)],
                                     o2_hbm.at[pl.ds(pl.multiple_of(dst * tile, tile), tile)], ssem.at[s])

    def spare_copy(half):
        return pltpu.make_async_copy(obuf.at[1], o2_hbm.at[pl.ds(spare0 + half * MOE_BLOCK * tile, MOE_BLOCK * tile)],
                                     ssem.at[1])

    def gather_rows(blk):
        for r in range(MOE_BLOCK):
            tok = src_ref[blk * MOE_BLOCK + r]
            xg[pl.ds(r, PACK_ROWS, stride=XG_STRIDE), :] = hn_ref[pl.ds(pl.multiple_of(tok * PACK_ROWS, PACK_ROWS),
                                                                         PACK_ROWS), :]

    def unpack_rows():
        lo, hi = [], []
        for j in range(PACK_ROWS):
            words = xg[j * XG_STRIDE:j * XG_STRIDE + MOE_BLOCK, :]
            lo.append(lax.bitcast_convert_type(jnp.left_shift(words, jnp.uint32(16)), _F32))
            hi.append(lax.bitcast_convert_type(jnp.bitwise_and(words, jnp.uint32(0xFFFF0000)), _F32))
        return jnp.concatenate(lo + hi, axis=-1).astype(_MXU)

    def start_scatter(blk, s, part=0, parts=1):
        n = MOE_BLOCK // parts
        for r in range(part * n, (part + 1) * n):
            scatter_copy(r, dst_ref[blk * MOE_BLOCK + r], s).start(priority=r % 2)

    def wait_scatter(s):
        for r in range(MOE_BLOCK):
            scatter_copy(0, 0, s).wait()

    def used(blk):
        return nv_ref[jnp.clip(blk, 0, nblk - 1)] > 0

    def scatter_previous(part):
        @pl.when(b >= 1)
        def _():
            start_scatter(b - 1, 1 - slot, part, SCATTER_PARTS)

    @pl.when(b == 0)
    def _():
        obuf[1] = jnp.zeros(obuf.shape[1:], _F32)
        spare_copy(0).start()
        spare_copy(1).start()
        spare_copy(0).wait()
        spare_copy(1).wait()

    @pl.when(used(b))
    def _():
        @pl.when((b == 0) | (be_ref[b] != be_ref[jnp.maximum(b - 1, 0)]))
        def _():
            wg_bf[...] = wg_ref[...].astype(_MXU)
            wu_bf[...] = wu_ref[...].astype(_MXU)
            wd_bf[...] = wd_ref[...].astype(_MXU)

        gather_rows(b)
        scatter_previous(0)
        x = unpack_rows()
        scatter_previous(1)
        gate = _dot(x, wg_bf[...])
        up = _dot(x, wu_bf[...])
        scatter_previous(2)
        o = _dot(jax.nn.silu(gate) * up, wd_bf[...])
        scatter_previous(3)

        @pl.when(b >= 2)
        def _():
            wait_scatter(slot)

        for j in range(SUBLANES):
            obuf[slot, pl.ds(j, MOE_BLOCK, stride=SUBLANES), :] = o[:, j * LANES:(j + 1) * LANES]

        @pl.when(b == nblk - 1)
        def _():
            start_scatter(b, slot)

            @pl.when(b >= 1)
            def _():
                wait_scatter(1 - slot)
            wait_scatter(slot)

    @pl.when((b >= 1) & used(b - 1) & jnp.logical_not(used(b)))
    def _():
        start_scatter(b - 1, 1 - slot)

        @pl.when(b >= 2)
        def _():
            wait_scatter(slot)
        wait_scatter(1 - slot)


def _final_kernel(h_ref, o2a_ref, o2b_ref, rprev_ref, g_ref, y_ref):
    n = h_ref.shape[0]
    y_ref[...] = _rms_norm(_mixer_input((h_ref, o2a_ref, o2b_ref, rprev_ref), True, n), g_ref[...])


def _stacked_weights(w):
    nstep = w['nstep']
    depth = w['w_in'].shape[0]
    sgu_w = w['sgu_w']
    sgu_b = w['sgu_b']
    vec = lambda a: a[:, None, :]
    pad = jnp.zeros((depth, D_MODEL, LANES - N_GROUPS - N_EXPERTS), _F32)
    w_rt = jnp.concatenate([w['w_group'], w['w_router'], pad], axis=-1)
    b_rt = jnp.concatenate([w['b_group'], w['b_router'], pad[:, 0, :]], axis=-1)
    head = (vec(w['mix_norm_g']), w['w_in'].astype(_MXU), w['pool_w'].astype(_MXU),
            vec(w['pool_scale']), vec(w['sgu_norm_g']), vec(w['sgu_norm_b']))
    tail = (w['conv_w'], vec(w['conv_b']), vec(w['conv_norm_g']), vec(w['conv_norm_b']),
            w['w_branch_pool'].astype(_MXU), w['w_branch_sgu'].astype(_MXU),
            w['w_branch_conv'].astype(_MXU), w['w_out'].astype(_MXU),
            vec(w['ffn_norm_g']), w_rt.astype(_MXU), vec(b_rt))
    sgu_p = (jnp.tril(sgu_w).astype(_MXU), jnp.repeat(sgu_b.transpose(0, 2, 1), SGU_HC, axis=2))
    corner = jnp.tril(sgu_w[:, :, :nstep, :nstep]).transpose(0, 2, 3, 1)
    sgu_s = (jnp.repeat(corner.reshape(depth, nstep * nstep, SGU_HEADS), SGU_HC, axis=2),
             jnp.repeat(sgu_b[:, :, :nstep].transpose(0, 2, 1), SGU_HC, axis=2))
    return head + sgu_p + tail, head + sgu_s + tail


def _weight_specs(ops, l):
    def spec(o):
        nd = o.ndim - 1
        return pl.BlockSpec((None,) + o.shape[1:], lambda *_: (l,) + (0,) * nd, pipeline_mode=pl.Buffered(1))
    return [spec(o) for o in ops]


def _ltri(n):
    r = lax.broadcasted_iota(jnp.int32, (n, n), 0)
    c = lax.broadcasted_iota(jnp.int32, (n, n), 1)
    return (c < r).astype(_MXU)


def _prev_layer_specs(tile, index, n_all):
    def slot1(*ids):
        i, z = index(*ids)
        return i + n_all // tile, z
    return [pl.BlockSpec((tile, D_MODEL), index), pl.BlockSpec((tile * SUBLANES, LANES), index),
            pl.BlockSpec((tile * SUBLANES, LANES), slot1), pl.BlockSpec((tile, ROUTE_COLS), index)]


def _prev_layer_args(prev):
    h, o2, route = prev
    return (h, o2, o2, route)


def _mixer_prompt(prev, x, sample, ops, l, *, nb, seq, n_all):
    combine = prev is not None
    nt = seq // TQ
    n_main = nb * nt
    n_steps = n_all // TQ
    h_s, hn_s, route_s, cnt_s = sample
    row = lambda s: (s, 0)
    main_row = lambda s: (jnp.minimum(s, n_main - 1), 0)
    tail_row = lambda s: (jnp.maximum(s - n_main, 0), 0)
    seq_blk = lambda s: (jnp.minimum(s // nt, nb - 1), 0, 0)
    if combine:
        inputs = _prev_layer_args(prev)
        in_specs = _prev_layer_specs(TQ, main_row, n_all)
    else:
        inputs = (x,)
        in_specs = [pl.BlockSpec((TQ, D_MODEL), main_row)]
    in_specs += [pl.BlockSpec((TQ, D_MODEL), tail_row), pl.BlockSpec((TQ * PACK_ROWS, LANES), tail_row),
                 pl.BlockSpec((TQ, ROUTE_COLS), tail_row), _const_spec(cnt_s.shape)]
    ltri = _ltri(TQ)
    out_shape = (jax.ShapeDtypeStruct((n_all, D_MODEL), _F32),
                 jax.ShapeDtypeStruct((n_all * PACK_ROWS, LANES), jnp.uint32),
                 jax.ShapeDtypeStruct((n_all, ROUTE_COLS), _F32),
                 jax.ShapeDtypeStruct((1, LANES), _F32),
                 jax.ShapeDtypeStruct((nb, POOL_HIST, W_BR), _F32),
                 jax.ShapeDtypeStruct((nb, CONV_HIST, W_BR), _F32))
    out_specs = (pl.BlockSpec((TQ, D_MODEL), row), pl.BlockSpec((TQ * PACK_ROWS, LANES), row),
                 pl.BlockSpec((TQ, ROUTE_COLS), row), pl.BlockSpec((1, LANES), lambda s: (0, 0)),
                 pl.BlockSpec((1, POOL_HIST, W_BR), seq_blk), pl.BlockSpec((1, CONV_HIST, W_BR), seq_blk))
    return pl.pallas_call(
        functools.partial(_mixer_prompt_kernel, combine=combine, nt=nt, n_main=n_main),
        grid=(n_steps,),
        in_specs=in_specs + _weight_specs(ops, l) + [_const_spec(ltri.shape)],
        out_specs=out_specs,
        out_shape=out_shape,
        scratch_shapes=[pltpu.VMEM((POOL_PAD + TQ, W_BR), _F32), pltpu.VMEM((SUBLANES, CONV_PAD + TQ, W_BR), _F32),
                        pltpu.VMEM((TQ, W_BR), _F32), pltpu.VMEM((1, LANES), _F32)],
        compiler_params=pltpu.CompilerParams(dimension_semantics=("arbitrary",), vmem_limit_bytes=VMEM_LIMIT),
        name="mixer_prompt",
    )(*inputs, h_s, hn_s, route_s, cnt_s, *ops, ltri)


def _mixer_sample(prev, x, sp, sc, ops, l, *, nseq, nstep, n_prompt):
    combine = prev is not None
    n = nseq * nstep
    blk = n_prompt // n
    row = lambda i: (blk, 0)
    zero = lambda i: (0, 0)
    if combine:
        inputs = _prev_layer_args(prev)
        in_specs = _prev_layer_specs(n, row, n_prompt + n)
    else:
        inputs = (x,)
        in_specs = [pl.BlockSpec((n, D_MODEL), zero)]
    in_specs += _weight_specs((sp, sc), l)
    ltri = _ltri(n)
    out_shape = (jax.ShapeDtypeStruct((n, D_MODEL), _F32), jax.ShapeDtypeStruct((n * PACK_ROWS, LANES), jnp.uint32),
                 jax.ShapeDtypeStruct((n, ROUTE_COLS), _F32), jax.ShapeDtypeStruct((1, LANES), _F32),
                 jax.ShapeDtypeStruct((n, W_BR), _F32), jax.ShapeDtypeStruct((n, W_BR), _F32),
                 jax.ShapeDtypeStruct((n, W_BR), _F32))
    out_specs = (pl.BlockSpec((n, D_MODEL), zero), pl.BlockSpec((n * PACK_ROWS, LANES), zero),
                 pl.BlockSpec((n, ROUTE_COLS), zero), pl.BlockSpec((1, LANES), zero),
                 pl.BlockSpec((n, W_BR), zero), pl.BlockSpec((n, W_BR), zero), pl.BlockSpec((n, W_BR), zero))
    return pl.pallas_call(
        functools.partial(_mixer_sample_kernel, combine=combine, nseq=nseq, nstep=nstep),
        grid=(1,),
        in_specs=in_specs + _weight_specs(ops, l) + [_const_spec(ltri.shape)],
        out_specs=out_specs,
        out_shape=out_shape,
        scratch_shapes=[pltpu.VMEM((n, W_BR), _F32), pltpu.VMEM((1, LANES), _F32)],
        compiler_params=pltpu.CompilerParams(dimension_semantics=("arbitrary",), vmem_limit_bytes=VMEM_LIMIT),
        name="mixer_sample",
    )(*inputs, sp, sc, *ops, ltri)


def _plan(route, cnt, *, n_blocks):
    n_all = route.shape[0]
    tile = 512
    nblk_lanes = -(-n_blocks // LANES) * LANES
    return pl.pallas_call(
        functools.partial(_plan_kernel, nblk_lanes=nblk_lanes),
        grid=(n_all // tile,),
        in_specs=[pl.BlockSpec((tile, ROUTE_COLS), lambda i: (i, 0)), pl.BlockSpec((1, LANES), lambda i: (0, 0))],
        out_specs=(pl.BlockSpec((tile, 2), lambda i: (i, 0)), pl.BlockSpec((SUBLANES, nblk_lanes), lambda i: (0, 0))),
        out_shape=(jax.ShapeDtypeStruct((n_all, 2), jnp.int32), jax.ShapeDtypeStruct((SUBLANES, nblk_lanes), jnp.int32)),
        compiler_params=pltpu.CompilerParams(dimension_semantics=("arbitrary",)),
        name="moe_plan",
    )(route, cnt)


def _rows(pos0, pos1, *, n_rows):
    n_tok = pos0.shape[0]
    src_pad = jnp.full((n_rows,), n_tok - 1, jnp.int32)
    dst_pad = 2 * n_tok + jnp.bitwise_and(jnp.arange(n_rows, dtype=jnp.int32), 2 * MOE_BLOCK - 1)
    smem = pl.BlockSpec(memory_space=pltpu.SMEM)
    hbm = pl.BlockSpec(memory_space=pl.ANY)
    return pl.pallas_call(
        functools.partial(_rows_kernel, n_tok=n_tok),
        in_specs=[smem, smem, hbm, hbm],
        out_specs=(smem, smem),
        out_shape=(jax.ShapeDtypeStruct((n_rows,), jnp.int32), jax.ShapeDtypeStruct((n_rows,), jnp.int32)),
        scratch_shapes=[pltpu.SemaphoreType.DMA((2,))],
        name="moe_rows",
    )(pos0, pos1, src_pad, dst_pad)


def _moe(be, nv, src, dst, hn_all, wg, wu, wd, l, *, n_blocks):
    n_all = hn_all.shape[0] // PACK_ROWS
    n_tiles = 2 * n_all + 2 * MOE_BLOCK
    wspec_in = pl.BlockSpec((None, None, D_MODEL, D_EXPERT), lambda b, be, *_: (l, be[b], 0, 0))
    wspec_out = pl.BlockSpec((None, None, D_EXPERT, D_MODEL), lambda b, be, *_: (l, be[b], 0, 0))
    grid_spec = pltpu.PrefetchScalarGridSpec(
        num_scalar_prefetch=4,
        grid=(n_blocks,),
        in_specs=[_const_spec(hn_all.shape), wspec_in, wspec_in, wspec_out],
        out_specs=pl.BlockSpec(memory_space=pl.ANY),
        scratch_shapes=[pltpu.VMEM((PACK_ROWS * XG_STRIDE, LANES), jnp.uint32),
                        pltpu.VMEM((2, MOE_BLOCK * SUBLANES, LANES), _F32),
                        pltpu.VMEM((D_MODEL, D_EXPERT), _MXU), pltpu.VMEM((D_MODEL, D_EXPERT), _MXU),
                        pltpu.VMEM((D_EXPERT, D_MODEL), _MXU),
                        pltpu.SemaphoreType.DMA((2,))])
    return pl.pallas_call(
        functools.partial(_moe_kernel, n_tok=n_all),
        grid_spec=grid_spec,
        out_shape=jax.ShapeDtypeStruct((n_tiles * SUBLANES, LANES), _F32),
        compiler_params=pltpu.CompilerParams(dimension_semantics=("arbitrary",), vmem_limit_bytes=VMEM_LIMIT),
        name="moe_experts",
    )(be, nv, src, dst, hn_all, wg, wu, wd)


def _final(prev, g, *, row0, n, tile):
    n_all = prev[0].shape[0]
    blk0 = row0 // tile
    row = lambda i: (blk0 + i, 0)
    return pl.pallas_call(
        _final_kernel,
        grid=(n // tile,),
        in_specs=_prev_layer_specs(tile, row, n_all) + [pl.BlockSpec((1, D_MODEL), lambda i: (0, 0))],
        out_specs=pl.BlockSpec((tile, D_MODEL), lambda i: (i, 0)),
        out_shape=jax.ShapeDtypeStruct((n, D_MODEL), _F32),
        compiler_params=pltpu.CompilerParams(dimension_semantics=("arbitrary",)),
        name="final_norm",
    )(*_prev_layer_args(prev), g)


def kernel(x_prompt, x_sample, state_pool, state_conv, mix_norm_g, w_in, pool_w, pool_scale, sgu_norm_g, sgu_norm_b, sgu_w, sgu_b, conv_w, conv_b, conv_norm_g, conv_norm_b, w_branch_pool, w_branch_sgu, w_branch_conv, w_out, ffn_norm_g, w_group, b_group, w_router, b_router, w_gate_e, w_up_e, w_down_e, final_norm_g):
    nb, seq, _ = x_prompt.shape
    nseq, nstep, _ = x_sample.shape
    n_prompt = nb * seq
    n_sample = nseq * nstep
    n_all = n_prompt + n_sample
    assert seq % TQ == 0 and TQ % CHUNK == 0 and nstep < CHUNK and n_sample % TQ == 0
    assert n_prompt % n_sample == 0 and n_all % 512 == 0 and n_sample % MOE_BLOCK == 0
    n_blocks = (2 * n_all) // MOE_BLOCK + N_EXPERTS
    n_rows = n_blocks * MOE_BLOCK

    w = dict(nstep=nstep, mix_norm_g=mix_norm_g, w_in=w_in, pool_w=pool_w, pool_scale=pool_scale,
             sgu_norm_g=sgu_norm_g, sgu_norm_b=sgu_norm_b, sgu_w=sgu_w, sgu_b=sgu_b, conv_w=conv_w, conv_b=conv_b,
             conv_norm_g=conv_norm_g, conv_norm_b=conv_norm_b, w_branch_pool=w_branch_pool,
             w_branch_sgu=w_branch_sgu, w_branch_conv=w_branch_conv, w_out=w_out, ffn_norm_g=ffn_norm_g,
             w_group=w_group, b_group=b_group, w_router=w_router, b_router=b_router)

    xp = x_prompt.reshape(n_prompt, D_MODEL)
    xs = x_sample.transpose(1, 0, 2).reshape(n_sample, D_MODEL)
    sp_t = state_pool.transpose(0, 2, 1, 3)
    sc_t = state_conv.transpose(0, 2, 1, 3)

    pool_p, pool_s, conv_p, conv_s, chunk_v = [], [], [], [], []
    prev = None
    ops_p, ops_s = _stacked_weights(w)
    for l in range(DEPTH):
        h_s, hn_s, route_s, cnt_s, pnew, cnew, vn = _mixer_sample(prev, xs, sp_t, sc_t, ops_s, l, nseq=nseq,
                                                                  nstep=nstep, n_prompt=n_prompt)
        h, hn, route, cnt, pstate, cstate = _mixer_prompt(prev, xp, (h_s, hn_s, route_s, cnt_s), ops_p, l,
                                                          nb=nb, seq=seq, n_all=n_all)
        pos, binfo = _plan(route, cnt, n_blocks=n_blocks)
        src, dst = _rows(pos[:, 0], pos[:, 1], n_rows=n_rows)
        o2 = _moe(binfo[0, :n_blocks], binfo[1, :n_blocks], src, dst, hn, w_gate_e, w_up_e, w_down_e, l,
                  n_blocks=n_blocks)
        prev = (h, o2, route)
        to_seq = lambda a: a.reshape(nstep, nseq, W_BR).transpose(1, 0, 2)
        pool_p.append(pstate)
        conv_p.append(cstate)
        pool_s.append(jnp.concatenate([state_pool[l][:, nstep:], to_seq(pnew)], axis=1))
        conv_s.append(jnp.concatenate([state_conv[l][:, nstep:], to_seq(cnew)], axis=1))
        chunk_v.append(to_seq(vn))

    g = final_norm_g[None]
    y_prompt = _final(prev, g, row0=0, n=n_prompt, tile=512).reshape(nb, seq, D_MODEL)
    y_sample = _final(prev, g, row0=n_prompt, n=n_sample, tile=n_sample)
    y_sample = y_sample.reshape(nstep, nseq, D_MODEL).transpose(1, 0, 2)
    return (y_prompt, y_sample, jnp.stack(pool_p), jnp.stack(pool_s), jnp.stack(conv_p), jnp.stack(conv_s),
            jnp.stack(chunk_v))
```

```python
import functools

import jax
import jax.numpy as jnp
from jax import lax
from jax.experimental import pallas as pl
from jax.experimental.pallas import tpu as pltpu

D_MODEL = 1024
DEPTH = 4
PAST_LEN = 16384
W_BR = 512
POOL_WINDOWS = (2, 4, 8, 16)
POOL_GC = 128
POOL_HIST = 15
SGU_HEADS = 4
SGU_HC = 128
CHUNK = 128
CONV_WIDTH = 31
CONV_HIST = 30
N_GROUPS = 4
EPG = 8
N_EXPERTS = 32
D_EXPERT = 512
MOE_BLOCK = 128
EPS = 1e-6

LANES = 128
SUBLANES = 8
VMEM_LIMIT = 56 * 1024 * 1024

TQ = 256
ROUTE_COLS = 8
EXP_LANE0 = N_GROUPS
CONV_RB = 64
PACK_ROWS = D_MODEL // 2 // LANES
XG_STRIDE = MOE_BLOCK + SUBLANES
PLAN_STEPS = 3
ROWS_UNROLL = 16
POOL_PAD = 16
CONV_PAD = 32

_MXU = jnp.bfloat16
_F32 = jnp.float32


def _dot(a, b):
    return jnp.dot(a.astype(_MXU), b.astype(_MXU), preferred_element_type=_F32)


def _rms_norm(x, g):
    return x * lax.rsqrt(jnp.mean(x * x, axis=-1, keepdims=True) + EPS) * g


def _layer_norm(x, g, b):
    mu = jnp.mean(x, axis=-1, keepdims=True)
    xc = x - mu
    var = jnp.mean(xc * xc, axis=-1, keepdims=True)
    return xc * lax.rsqrt(var + EPS) * g + b


def _const_spec(shape):
    nd = len(shape)
    return pl.BlockSpec(shape, lambda *_: (0,) * nd, pipeline_mode=pl.Buffered(1))


def _tile_rows_to_tokens(ref, start, n, stride):
    return jnp.concatenate([ref[pl.ds(start + j, n, stride=stride), :] for j in range(SUBLANES)], axis=-1)


def _mixer_input(in_refs, combine, n):
    if not combine:
        return in_refs[0][...]
    h_ref, o2a_ref, o2b_ref, rprev_ref = in_refs
    g0 = rprev_ref[:, 4:5]
    g1 = rprev_ref[:, 5:6]
    o0 = _tile_rows_to_tokens(o2a_ref, 0, n, SUBLANES)
    o1 = _tile_rows_to_tokens(o2b_ref, 0, n, SUBLANES)
    return h_ref[...] + (g0 * o0 + g1 * o1)


def _route(hn, wrt_ref, brt_ref, ltri_ref, cnt_acc, route_ref):
    n = hn.shape[0]
    logits = _dot(hn, wrt_ref[...]) + brt_ref[...]
    lane = lax.broadcasted_iota(jnp.int32, (n, LANES), 1).astype(_F32)
    neg = jnp.float32(-jnp.inf)
    is_grp = lane < N_GROUPS
    gl = jnp.where(is_grp, logits, neg)
    gmax = jnp.max(gl, axis=-1, keepdims=True)
    grp = jnp.min(jnp.where(gl == gmax, lane, float(LANES)), axis=-1, keepdims=True)
    gsum = jnp.sum(jnp.where(is_grp, jnp.exp(logits - gmax), 0.0), axis=-1, keepdims=True)
    grp_w = 1.0 / gsum
    lo = EXP_LANE0 + grp * EPG
    el = jnp.where((lane >= lo) & (lane < lo + EPG), logits, neg)
    v0 = jnp.max(el, axis=-1, keepdims=True)
    i0 = jnp.min(jnp.where(el == v0, lane, float(LANES)), axis=-1, keepdims=True)
    el1 = jnp.where(lane == i0, neg, el)
    v1 = jnp.max(el1, axis=-1, keepdims=True)
    i1 = jnp.min(jnp.where(el1 == v1, lane, float(LANES)), axis=-1, keepdims=True)
    ex = jnp.exp(v1 - v0)
    den = 1.0 + ex
    g0 = grp_w / den
    g1 = grp_w * ex / den
    oh0 = lane == i0
    oh1 = lane == i1
    oh = jnp.where(oh0 | oh1, 1.0, 0.0)
    before = _dot(ltri_ref[...], oh) + cnt_acc[...]
    r0 = jnp.sum(jnp.where(oh0, before, 0.0), axis=-1, keepdims=True)
    r1 = jnp.sum(jnp.where(oh1, before, 0.0), axis=-1, keepdims=True)
    cnt_acc[...] = cnt_acc[...] + jnp.sum(oh, axis=0, keepdims=True)
    route_ref[:, 0:1] = i0 - EXP_LANE0
    route_ref[:, 1:2] = i1 - EXP_LANE0
    route_ref[:, 2:3] = r0
    route_ref[:, 3:4] = r1
    route_ref[:, 4:5] = g0
    route_ref[:, 5:6] = g1
    route_ref[:, 6:8] = jnp.zeros((n, 2), _F32)


def _gate_slab(xn, win_ref, k, nslab):
    width = 3 * D_MODEL // nslab
    lo = 5 * W_BR + k * width
    return _dot(xn, win_ref[:, lo:lo + width])


def _mixer_tail(x, xn, ya, yb, yc, gate_pre, wbp_ref, wbs_ref, wbc_ref, wout_ref, gffn_ref,
                wrt_ref, brt_ref, ltri_ref, cnt_acc, hout_ref, hn_ref, route_ref):
    n = x.shape[0]
    gates = jax.nn.sigmoid(gate_pre)
    merged = gates[:, 0:D_MODEL] * _dot(ya, wbp_ref[...])
    merged = merged + gates[:, D_MODEL:2 * D_MODEL] * _dot(yb, wbs_ref[...])
    merged = merged + gates[:, 2 * D_MODEL:3 * D_MODEL] * _dot(yc, wbc_ref[...])
    h = x + _dot(merged, wout_ref[...])
    hout_ref[...] = h
    hn = _rms_norm(h, gffn_ref[...])
    half = D_MODEL // 2
    lo = lax.bitcast_convert_type(hn[:, :half].astype(_MXU).astype(_F32), jnp.uint32)
    hi = lax.bitcast_convert_type(hn[:, half:].astype(_MXU).astype(_F32), jnp.uint32)
    words = jnp.bitwise_or(jnp.bitwise_and(hi, jnp.uint32(0xFFFF0000)), jnp.right_shift(lo, jnp.uint32(16)))
    for j in range(PACK_ROWS):
        hn_ref[pl.ds(j, n, stride=PACK_ROWS), :] = words[:, j * LANES:(j + 1) * LANES]
    _route(hn, wrt_ref, brt_ref, ltri_ref, cnt_acc, route_ref)


def _mixer_prompt_kernel(*refs, combine, nt, n_main):
    n_in = 4 if combine else 1
    in_refs = refs[:n_in]
    next_refs = refs[n_in:2 * n_in]
    hs_ref, hns_ref, routes_ref = refs[2 * n_in:2 * n_in + 3]
    hout_ref, hn_ref, route_ref = refs[-12:-9]
    step = pl.program_id(0)

    @pl.when(step >= n_main)
    def _():
        hout_ref[...] = hs_ref[...]
        hn_ref[...] = hns_ref[...]
        route_ref[...] = routes_ref[...]

    @pl.when(step < n_main)
    def _():
        _mixer_prompt_step(in_refs, next_refs, refs[2 * n_in + 3:], combine, nt)


def _mixer_prompt_step(first_refs, next_refs, refs, combine, nt):
    (cnt_in_ref, gmix_ref, win_ref, poolw_ref, pscale_ref, sgug_ref, sgub_ref, sguw_ref, sgubias_ref,
     convw_ref, convb_ref, cng_ref, cnb_ref, wbp_ref, wbs_ref, wbc_ref, wout_ref,
     gffn_ref, wrt_ref, brt_ref, ltri_ref,
     hout_ref, hn_ref, route_ref, cnt_ref, pstate_ref, cstate_ref,
     pbuf, cbuf, ybuf, cnt_acc, xbuf, xnbuf) = refs
    step = pl.program_id(0)
    i = lax.rem(step, nt)
    par = lax.rem(step, 2)

    def head(in_refs, slot):
        rows = _mixer_input(in_refs, combine, TQ)
        xbuf[slot] = rows
        xnbuf[slot] = _rms_norm(rows, gmix_ref[...]).astype(_MXU)

    @pl.when(step == 0)
    def _():
        cnt_acc[...] = cnt_in_ref[...]
        head(first_refs, 0)

    @pl.when(i == 0)
    def _():
        pbuf[0:POOL_PAD, :] = jnp.zeros((POOL_PAD, W_BR), _F32)
        cbuf[0, 0:CONV_PAD, :] = jnp.zeros((CONV_PAD, W_BR), _F32)

    xn = xnbuf[par]

    ca = _dot(xn, win_ref[:, 3 * W_BR:4 * W_BR])
    cg = _dot(xn, win_ref[:, 4 * W_BR:5 * W_BR])
    cbuf[0, CONV_PAD:CONV_PAD + TQ, :] = ca * jax.nn.sigmoid(cg)
    for s in range(1, SUBLANES):
        cbuf[s, SUBLANES:CONV_PAD + TQ, :] = cbuf[0, SUBLANES - s:CONV_PAD + TQ - s, :]
    n_lb = W_BR // LANES
    other = [lambda: _dot(xn, win_ref[:, 0:W_BR]), lambda: _dot(xn, win_ref[:, W_BR:2 * W_BR]),
             lambda: _dot(xn, win_ref[:, 2 * W_BR:3 * W_BR]), lambda: _gate_slab(xn, win_ref, 0, 1)]
    done = []
    for lb in range(n_lb):
        sl = slice(lb * LANES, (lb + 1) * LANES)
        for rb in range(TQ // CONV_RB):
            r0 = CONV_PAD + rb * CONV_RB
            acc = jnp.broadcast_to(convb_ref[:, sl], (CONV_RB, LANES))
            for j in range(CONV_WIDTH):
                q, s = divmod(j, SUBLANES)
                rows = cbuf[s, r0 - q * SUBLANES:r0 - q * SUBLANES + CONV_RB, sl]
                acc = acc + rows * convw_ref[CONV_HIST - j:CONV_WIDTH - j, sl]
            ybuf[rb * CONV_RB:(rb + 1) * CONV_RB, sl] = acc
        done.append(other[lb]())
    p, u, v, gate_pre = done
    yc = jax.nn.silu(_layer_norm(ybuf[...], cng_ref[...], cnb_ref[...]))
    cstate_ref[0] = cbuf[0, CONV_PAD + TQ - CONV_HIST:CONV_PAD + TQ, :]
    cbuf[0, 0:CONV_PAD, :] = cbuf[0, TQ:TQ + CONV_PAD, :]

    pbuf[POOL_PAD:POOL_PAD + TQ, :] = p
    pos = i * TQ + lax.broadcasted_iota(jnp.int32, (TQ, 1), 0)
    ya_parts = []
    for gi, w in enumerate(POOL_WINDOWS):
        sl = slice(gi * POOL_GC, (gi + 1) * POOL_GC)
        acc = p[:, sl]
        for j in range(1, w):
            acc = acc + pbuf[POOL_PAD - j:POOL_PAD - j + TQ, sl]
        cnt = jnp.minimum(pos + 1, w).astype(_F32)
        d = acc / cnt - p[:, sl]
        ya_parts.append(_dot(d, poolw_ref[gi]))
    ya = jnp.concatenate(ya_parts, axis=-1) * pscale_ref[...]
    pstate_ref[0] = pbuf[POOL_PAD + TQ - POOL_HIST:POOL_PAD + TQ, :]
    pbuf[0:POOL_PAD, :] = pbuf[TQ:TQ + POOL_PAD, :]

    vn = _layer_norm(v, sgug_ref[...], sgub_ref[...])
    rows = []
    for c in range(TQ // CHUNK):
        heads = [_dot(sguw_ref[hd], vn[c * CHUNK:(c + 1) * CHUNK, hd * SGU_HC:(hd + 1) * SGU_HC])
                 for hd in range(SGU_HEADS)]
        rows.append(jnp.concatenate(heads, axis=-1) + sgubias_ref[...])
    yb = u * jnp.concatenate(rows, axis=0)

    _mixer_tail(xbuf[par], xn, ya, yb, yc, gate_pre, wbp_ref, wbs_ref, wbc_ref, wout_ref,
                gffn_ref, wrt_ref, brt_ref, ltri_ref, cnt_acc, hout_ref, hn_ref, route_ref)
    cnt_ref[...] = cnt_acc[...]
    head(next_refs, 1 - par)


def _mixer_sample_kernel(*refs, combine, nseq, nstep):
    n_in = 4 if combine else 1
    in_refs = refs[:n_in]
    (sp_ref, sc_ref, gmix_ref, win_ref, poolw_ref, pscale_ref, sgug_ref, sgub_ref, sguw_ref, sgubias_ref,
     convw_ref, convb_ref, cng_ref, cnb_ref, wbp_ref, wbs_ref, wbc_ref, wout_ref,
     gffn_ref, wrt_ref, brt_ref, ltri_ref,
     hout_ref, hn_ref, route_ref, cnt_ref, pnew_ref, cnew_ref, vn_ref,
     ybuf, cnt_acc) = refs[n_in:]
    n = nseq * nstep
    cnt_acc[...] = jnp.zeros_like(cnt_acc)
    x = _mixer_input(in_refs, combine, n)
    xn = _rms_norm(x, gmix_ref[...]).astype(_MXU)

    def slab(a, t):
        return a[t * nseq:(t + 1) * nseq]

    p = _dot(xn, win_ref[:, 0:W_BR])
    pnew_ref[...] = p
    ya_rows = []
    for t in range(nstep):
        parts = []
        for gi, w in enumerate(POOL_WINDOWS):
            sl = slice(gi * POOL_GC, (gi + 1) * POOL_GC)
            acc = slab(p, t)[:, sl]
            for j in range(1, w):
                k = POOL_HIST + t - j
                acc = acc + (sp_ref[k, :, sl] if k < POOL_HIST else slab(p, k - POOL_HIST)[:, sl])
            cnt = float(min(PAST_LEN + t + 1, w))
            d = acc / cnt - slab(p, t)[:, sl]
            parts.append(_dot(d, poolw_ref[gi]))
        ya_rows.append(jnp.concatenate(parts, axis=-1))
    ya = jnp.concatenate(ya_rows, axis=0) * pscale_ref[...]

    u = _dot(xn, win_ref[:, W_BR:2 * W_BR])
    v = _dot(xn, win_ref[:, 2 * W_BR:3 * W_BR])
    vn = _layer_norm(v, sgug_ref[...], sgub_ref[...])
    vn_ref[...] = vn
    s_rows = []
    for t in range(nstep):
        acc = jnp.broadcast_to(sgubias_ref[t:t + 1, :], (nseq, W_BR))
        for s in range(t + 1):
            acc = acc + sguw_ref[t * nstep + s:t * nstep + s + 1, :] * slab(vn, s)
        s_rows.append(acc)
    yb = u * jnp.concatenate(s_rows, axis=0)

    ca = _dot(xn, win_ref[:, 3 * W_BR:4 * W_BR])
    cg = _dot(xn, win_ref[:, 4 * W_BR:5 * W_BR])
    cnew_ref[...] = ca * jax.nn.sigmoid(cg)
    for t in range(nstep):
        for lb in range(W_BR // LANES):
            sl = slice(lb * LANES, (lb + 1) * LANES)
            acc = jnp.broadcast_to(convb_ref[:, sl], (nseq, LANES))
            for kw in range(CONV_WIDTH):
                k = t + kw
                row = sc_ref[k, :, sl] if k < CONV_HIST else cnew_ref[(k - CONV_HIST) * nseq:(k - CONV_HIST + 1) * nseq, sl]
                acc = acc + row * convw_ref[kw:kw + 1, sl]
            ybuf[t * nseq:(t + 1) * nseq, sl] = acc
    yc = jax.nn.silu(_layer_norm(ybuf[...], cng_ref[...], cnb_ref[...]))

    _mixer_tail(x, xn, ya, yb, yc, _gate_slab(xn, win_ref, 0, 1), wbp_ref, wbs_ref, wbc_ref, wout_ref,
                gffn_ref, wrt_ref, brt_ref, ltri_ref, cnt_acc, hout_ref, hn_ref, route_ref)
    cnt_ref[...] = cnt_acc[...]


def _lane_to_sublane(row):
    sub = lax.broadcasted_iota(jnp.int32, (LANES, LANES), 0)
    lane = lax.broadcasted_iota(jnp.int32, (LANES, LANES), 1)
    return jnp.sum(jnp.where(sub == lane, jnp.broadcast_to(row, (LANES, LANES)), 0.0), axis=1, keepdims=True)


def _plan_kernel(route_ref, cnt_ref, pos_ref, binfo_ref, *, nblk_lanes):
    cnt = cnt_ref[...]
    padded = jnp.floor((cnt + (MOE_BLOCK - 1)) * (1.0 / MOE_BLOCK)) * MOE_BLOCK
    lane1 = lax.broadcasted_iota(jnp.int32, (SUBLANES, LANES), 1)
    pends = jnp.broadcast_to(padded, (SUBLANES, LANES))
    sh = 1
    while sh < LANES:
        pends = pends + jnp.where(lane1 >= sh, pltpu.roll(pends, sh, axis=1), 0.0)
        sh *= 2
    pends = pends[0:1]
    pstart = pends - padded

    n = route_ref.shape[0]
    lane = lax.broadcasted_iota(jnp.int32, (n, LANES), 1).astype(_F32)
    for k in range(2):
        e_lane = route_ref[:, k:k + 1] + EXP_LANE0
        base = jnp.sum(jnp.where(lane == e_lane, pstart, 0.0), axis=-1, keepdims=True)
        pos_ref[:, k:k + 1] = (base + route_ref[:, 2 + k:3 + k]).astype(jnp.int32)

    @pl.when(pl.program_id(0) == 0)
    def _():
        pends_c = _lane_to_sublane(pends)
        pstart_c = _lane_to_sublane(pstart)
        cnt_c = _lane_to_sublane(cnt)
        sub = lax.broadcasted_iota(jnp.int32, (LANES, LANES), 0)
        is_exp = (sub >= EXP_LANE0) & (sub < EXP_LANE0 + N_EXPERTS)
        binfo_ref[...] = jnp.zeros_like(binfo_ref)
        for q in range(nblk_lanes // LANES):
            bstart = ((lax.broadcasted_iota(jnp.int32, (LANES, LANES), 1) + q * LANES) * MOE_BLOCK).astype(_F32)
            done = jnp.where(is_exp & (pends_c <= bstart), 1.0, 0.0)
            bexp = jnp.minimum(jnp.sum(done, axis=0, keepdims=True), float(N_EXPERTS - 1))
            inside = is_exp & (pstart_c <= bstart) & (bstart < pends_c)
            valid = jnp.clip(pstart_c + cnt_c - bstart, 0.0, float(MOE_BLOCK))
            nval = jnp.sum(jnp.where(inside, valid, 0.0), axis=0, keepdims=True)
            binfo_ref[0:1, q * LANES:(q + 1) * LANES] = bexp.astype(jnp.int32)
            binfo_ref[1:2, q * LANES:(q + 1) * LANES] = nval.astype(jnp.int32)


def _rows_kernel(pos0_ref, pos1_ref, src_pad_hbm, dst_pad_hbm, src_ref, dst_ref, sem, *, n_tok):
    unroll = ROWS_UNROLL
    fill_src = pltpu.make_async_copy(src_pad_hbm, src_ref, sem.at[0])
    fill_dst = pltpu.make_async_copy(dst_pad_hbm, dst_ref, sem.at[1])
    fill_src.start()
    fill_dst.start()
    fill_src.wait()
    fill_dst.wait()

    def put(i, c):
        for k in range(unroll):
            t = i * unroll + k
            p0 = pos0_ref[t]
            p1 = pos1_ref[t]
            src_ref[p0] = t
            src_ref[p1] = t
            dst_ref[p0] = t
            dst_ref[p1] = n_tok + t
        return c
    lax.fori_loop(0, n_tok // unroll, put, 0)


def _moe_kernel(be_ref, nv_ref, src_ref, dst_ref, hn_ref, wg_ref, wu_ref, wd_ref, o2_hbm,
                xg, xrows, obuf, wg_bf, wu_bf, wd_bf, ssem, *, n_tok):
    b = pl.program_id(0)
    nblk = pl.num_programs(0)
    slot = b % 2
    tile = SUBLANES
    spare0 = 2 * n_tok * tile

    def scatter_copy(row, dst, s):
        return pltpu.make_async_copy(obuf.at[s, pl.ds(row * tile, tile)],
                                     o2_hbm.at[pl.ds(pl.multiple_of(dst * tile, tile), tile)], ssem.at[s])

    def spare_copy(half):
        return pltpu.make_async_copy(obuf.at[1], o2_hbm.at[pl.ds(spare0 + half * MOE_BLOCK * tile, MOE_BLOCK * tile)],
                                     ssem.at[1])

    def gather_block(blk, s):
        for r in range(MOE_BLOCK):
            tok = src_ref[blk * MOE_BLOCK + r]
            xg[pl.ds(r, PACK_ROWS, stride=XG_STRIDE), :] = hn_ref[pl.ds(pl.multiple_of(tok * PACK_ROWS, PACK_ROWS),
                                                                         PACK_ROWS), :]
        half = D_MODEL // 2
        for j in range(PACK_ROWS):
            words = xg[j * XG_STRIDE:j * XG_STRIDE + MOE_BLOCK, :]
            lo = lax.bitcast_convert_type(jnp.left_shift(words, jnp.uint32(16)), _F32)
            hi = lax.bitcast_convert_type(jnp.bitwise_and(words, jnp.uint32(0xFFFF0000)), _F32)
            xrows[s, :, j * LANES:(j + 1) * LANES] = lo.astype(_MXU)
            xrows[s, :, half + j * LANES:half + (j + 1) * LANES] = hi.astype(_MXU)

    def start_scatter(blk, s):
        for r in range(MOE_BLOCK):
            scatter_copy(r, dst_ref[blk * MOE_BLOCK + r], s).start(priority=r % 2)

    def wait_scatter(s):
        for r in range(MOE_BLOCK):
            scatter_copy(0, 0, s).wait()

    def used(blk):
        return nv_ref[jnp.clip(blk, 0, nblk - 1)] > 0

    @pl.when(b == 0)
    def _():
        obuf[1] = jnp.zeros(obuf.shape[1:], _F32)
        spare_copy(0).start()
        spare_copy(1).start()
        spare_copy(0).wait()
        spare_copy(1).wait()
        gather_block(0, 0)

    @pl.when((b >= 2) & used(b - 2))
    def _():
        wait_scatter(slot)

    @pl.when(used(b))
    def _():
        @pl.when((b == 0) | (be_ref[b] != be_ref[jnp.maximum(b - 1, 0)]))
        def _():
            wg_bf[...] = wg_ref[...].astype(_MXU)
            wu_bf[...] = wu_ref[...].astype(_MXU)
            wd_bf[...] = wd_ref[...].astype(_MXU)

        x = xrows[slot]
        gather_block(jnp.minimum(b + 1, nblk - 1), 1 - slot)
        hb = jax.nn.silu(_dot(x, wg_bf[...])) * _dot(x, wu_bf[...])
        o = _dot(hb, wd_bf[...])
        for j in range(SUBLANES):
            obuf[slot, pl.ds(j, MOE_BLOCK, stride=SUBLANES), :] = o[:, j * LANES:(j + 1) * LANES]
        start_scatter(b, slot)

    @pl.when(b == nblk - 1)
    def _():
        @pl.when((b >= 1) & used(b - 1))
        def _():
            wait_scatter(1 - slot)

        @pl.when(used(b))
        def _():
            wait_scatter(slot)


def _final_kernel(h_ref, o2a_ref, o2b_ref, rprev_ref, g_ref, y_ref):
    n = h_ref.shape[0]
    y_ref[...] = _rms_norm(_mixer_input((h_ref, o2a_ref, o2b_ref, rprev_ref), True, n), g_ref[...])


def _stacked_weights(w):
    nstep = w['nstep']
    depth = w['w_in'].shape[0]
    sgu_w = w['sgu_w']
    sgu_b = w['sgu_b']
    vec = lambda a: a[:, None, :]
    pad = jnp.zeros((depth, D_MODEL, LANES - N_GROUPS - N_EXPERTS), _F32)
    w_rt = jnp.concatenate([w['w_group'], w['w_router'], pad], axis=-1)
    b_rt = jnp.concatenate([w['b_group'], w['b_router'], pad[:, 0, :]], axis=-1)
    head = (vec(w['mix_norm_g']), w['w_in'].astype(_MXU), w['pool_w'].astype(_MXU),
            vec(w['pool_scale']), vec(w['sgu_norm_g']), vec(w['sgu_norm_b']))
    tail = (w['conv_w'], vec(w['conv_b']), vec(w['conv_norm_g']), vec(w['conv_norm_b']),
            w['w_branch_pool'].astype(_MXU), w['w_branch_sgu'].astype(_MXU),
            w['w_branch_conv'].astype(_MXU), w['w_out'].astype(_MXU),
            vec(w['ffn_norm_g']), w_rt.astype(_MXU), vec(b_rt))
    sgu_p = (jnp.tril(sgu_w).astype(_MXU), jnp.repeat(sgu_b.transpose(0, 2, 1), SGU_HC, axis=2))
    corner = jnp.tril(sgu_w[:, :, :nstep, :nstep]).transpose(0, 2, 3, 1)
    sgu_s = (jnp.repeat(corner.reshape(depth, nstep * nstep, SGU_HEADS), SGU_HC, axis=2),
             jnp.repeat(sgu_b[:, :, :nstep].transpose(0, 2, 1), SGU_HC, axis=2))
    return head + sgu_p + tail, head + sgu_s + tail


def _weight_specs(ops, l):
    def spec(o):
        nd = o.ndim - 1
        return pl.BlockSpec((None,) + o.shape[1:], lambda *_: (l,) + (0,) * nd, pipeline_mode=pl.Buffered(1))
    return [spec(o) for o in ops]


def _ltri(n):
    r = lax.broadcasted_iota(jnp.int32, (n, n), 0)
    c = lax.broadcasted_iota(jnp.int32, (n, n), 1)
    return (c < r).astype(_MXU)


def _prev_layer_specs(tile, index, n_all):
    def slot1(*ids):
        i, z = index(*ids)
        return i + n_all // tile, z
    return [pl.BlockSpec((tile, D_MODEL), index), pl.BlockSpec((tile * SUBLANES, LANES), index),
            pl.BlockSpec((tile * SUBLANES, LANES), slot1), pl.BlockSpec((tile, ROUTE_COLS), index)]


def _prev_layer_args(prev):
    h, o2, route = prev
    return (h, o2, o2, route)


def _mixer_prompt(prev, x, sample, ops, l, *, nb, seq, n_all):
    combine = prev is not None
    nt = seq // TQ
    n_main = nb * nt
    n_steps = n_all // TQ
    h_s, hn_s, route_s, cnt_s = sample
    row = lambda s: (s, 0)
    first_row = lambda s: (0, 0)
    next_row = lambda s: (jnp.minimum(s + 1, n_main - 1), 0)
    tail_row = lambda s: (jnp.maximum(s - n_main, 0), 0)
    seq_blk = lambda s: (jnp.minimum(s // nt, nb - 1), 0, 0)
    if combine:
        inputs = _prev_layer_args(prev) * 2
        in_specs = _prev_layer_specs(TQ, first_row, n_all) + _prev_layer_specs(TQ, next_row, n_all)
    else:
        inputs = (x, x)
        in_specs = [pl.BlockSpec((TQ, D_MODEL), first_row), pl.BlockSpec((TQ, D_MODEL), next_row)]
    in_specs += [pl.BlockSpec((TQ, D_MODEL), tail_row), pl.BlockSpec((TQ * PACK_ROWS, LANES), tail_row),
                 pl.BlockSpec((TQ, ROUTE_COLS), tail_row), _const_spec(cnt_s.shape)]
    ltri = _ltri(TQ)
    out_shape = (jax.ShapeDtypeStruct((n_all, D_MODEL), _F32),
                 jax.ShapeDtypeStruct((n_all * PACK_ROWS, LANES), jnp.uint32),
                 jax.ShapeDtypeStruct((n_all, ROUTE_COLS), _F32),
                 jax.ShapeDtypeStruct((1, LANES), _F32),
                 jax.ShapeDtypeStruct((nb, POOL_HIST, W_BR), _F32),
                 jax.ShapeDtypeStruct((nb, CONV_HIST, W_BR), _F32))
    out_specs = (pl.BlockSpec((TQ, D_MODEL), row), pl.BlockSpec((TQ * PACK_ROWS, LANES), row),
                 pl.BlockSpec((TQ, ROUTE_COLS), row), pl.BlockSpec((1, LANES), lambda s: (0, 0)),
                 pl.BlockSpec((1, POOL_HIST, W_BR), seq_blk), pl.BlockSpec((1, CONV_HIST, W_BR), seq_blk))
    return pl.pallas_call(
        functools.partial(_mixer_prompt_kernel, combine=combine, nt=nt, n_main=n_main),
        grid=(n_steps,),
        in_specs=in_specs + _weight_specs(ops, l) + [_const_spec(ltri.shape)],
        out_specs=out_specs,
        out_shape=out_shape,
        scratch_shapes=[pltpu.VMEM((POOL_PAD + TQ, W_BR), _F32), pltpu.VMEM((SUBLANES, CONV_PAD + TQ, W_BR), _F32),
                        pltpu.VMEM((TQ, W_BR), _F32), pltpu.VMEM((1, LANES), _F32),
                        pltpu.VMEM((2, TQ, D_MODEL), _F32), pltpu.VMEM((2, TQ, D_MODEL), _MXU)],
        compiler_params=pltpu.CompilerParams(dimension_semantics=("arbitrary",), vmem_limit_bytes=VMEM_LIMIT),
        name="mixer_prompt",
    )(*inputs, h_s, hn_s, route_s, cnt_s, *ops, ltri)


def _mixer_sample(prev, x, sp, sc, ops, l, *, nseq, nstep, n_prompt):
    combine = prev is not None
    n = nseq * nstep
    blk = n_prompt // n
    row = lambda i: (blk, 0)
    zero = lambda i: (0, 0)
    if combine:
        inputs = _prev_layer_args(prev)
        in_specs = _prev_layer_specs(n, row, n_prompt + n)
    else:
        inputs = (x,)
        in_specs = [pl.BlockSpec((n, D_MODEL), zero)]
    in_specs += _weight_specs((sp, sc), l)
    ltri = _ltri(n)
    out_shape = (jax.ShapeDtypeStruct((n, D_MODEL), _F32), jax.ShapeDtypeStruct((n * PACK_ROWS, LANES), jnp.uint32),
                 jax.ShapeDtypeStruct((n, ROUTE_COLS), _F32), jax.ShapeDtypeStruct((1, LANES), _F32),
                 jax.ShapeDtypeStruct((n, W_BR), _F32), jax.ShapeDtypeStruct((n, W_BR), _F32),
                 jax.ShapeDtypeStruct((n, W_BR), _F32))
    out_specs = (pl.BlockSpec((n, D_MODEL), zero), pl.BlockSpec((n * PACK_ROWS, LANES), zero),
                 pl.BlockSpec((n, ROUTE_COLS), zero), pl.BlockSpec((1, LANES), zero),
                 pl.BlockSpec((n, W_BR), zero), pl.BlockSpec((n, W_BR), zero), pl.BlockSpec((n, W_BR), zero))
    return pl.pallas_call(
        functools.partial(_mixer_sample_kernel, combine=combine, nseq=nseq, nstep=nstep),
        grid=(1,),
        in_specs=in_specs + _weight_specs(ops, l) + [_const_spec(ltri.shape)],
        out_specs=out_specs,
        out_shape=out_shape,
        scratch_shapes=[pltpu.VMEM((n, W_BR), _F32), pltpu.VMEM((1, LANES), _F32)],
        compiler_params=pltpu.CompilerParams(dimension_semantics=("arbitrary",), vmem_limit_bytes=VMEM_LIMIT),
        name="mixer_sample",
    )(*inputs, sp, sc, *ops, ltri)


def _plan(route, cnt, *, n_blocks):
    n_all = route.shape[0]
    tile = n_all // PLAN_STEPS
    assert tile * PLAN_STEPS == n_all and tile % SUBLANES == 0
    nblk_lanes = -(-n_blocks // LANES) * LANES
    return pl.pallas_call(
        functools.partial(_plan_kernel, nblk_lanes=nblk_lanes),
        grid=(n_all // tile,),
        in_specs=[pl.BlockSpec((tile, ROUTE_COLS), lambda i: (i, 0)), pl.BlockSpec((1, LANES), lambda i: (0, 0))],
        out_specs=(pl.BlockSpec((tile, 2), lambda i: (i, 0)), pl.BlockSpec((SUBLANES, nblk_lanes), lambda i: (0, 0))),
        out_shape=(jax.ShapeDtypeStruct((n_all, 2), jnp.int32), jax.ShapeDtypeStruct((SUBLANES, nblk_lanes), jnp.int32)),
        compiler_params=pltpu.CompilerParams(dimension_semantics=("arbitrary",)),
        name="moe_plan",
    )(route, cnt)


def _rows(pos0, pos1, *, n_rows):
    n_tok = pos0.shape[0]
    src_pad = jnp.full((n_rows,), n_tok - 1, jnp.int32)
    dst_pad = 2 * n_tok + jnp.bitwise_and(jnp.arange(n_rows, dtype=jnp.int32), 2 * MOE_BLOCK - 1)
    smem = pl.BlockSpec(memory_space=pltpu.SMEM)
    hbm = pl.BlockSpec(memory_space=pl.ANY)
    return pl.pallas_call(
        functools.partial(_rows_kernel, n_tok=n_tok),
        in_specs=[smem, smem, hbm, hbm],
        out_specs=(smem, smem),
        out_shape=(jax.ShapeDtypeStruct((n_rows,), jnp.int32), jax.ShapeDtypeStruct((n_rows,), jnp.int32)),
        scratch_shapes=[pltpu.SemaphoreType.DMA((2,))],
        name="moe_rows",
    )(pos0, pos1, src_pad, dst_pad)


def _moe(be, nv, src, dst, hn_all, wg, wu, wd, l, *, n_blocks):
    n_all = hn_all.shape[0] // PACK_ROWS
    n_tiles = 2 * n_all + 2 * MOE_BLOCK
    wspec_in = pl.BlockSpec((None, None, D_MODEL, D_EXPERT), lambda b, be, *_: (l, be[b], 0, 0))
    wspec_out = pl.BlockSpec((None, None, D_EXPERT, D_MODEL), lambda b, be, *_: (l, be[b], 0, 0))
    grid_spec = pltpu.PrefetchScalarGridSpec(
        num_scalar_prefetch=4,
        grid=(n_blocks,),
        in_specs=[_const_spec(hn_all.shape), wspec_in, wspec_in, wspec_out],
        out_specs=pl.BlockSpec(memory_space=pl.ANY),
        scratch_shapes=[pltpu.VMEM((PACK_ROWS * XG_STRIDE, LANES), jnp.uint32),
                        pltpu.VMEM((2, MOE_BLOCK, D_MODEL), _MXU),
                        pltpu.VMEM((2, MOE_BLOCK * SUBLANES, LANES), _F32),
                        pltpu.VMEM((D_MODEL, D_EXPERT), _MXU), pltpu.VMEM((D_MODEL, D_EXPERT), _MXU),
                        pltpu.VMEM((D_EXPERT, D_MODEL), _MXU),
                        pltpu.SemaphoreType.DMA((2,))])
    return pl.pallas_call(
        functools.partial(_moe_kernel, n_tok=n_all),
        grid_spec=grid_spec,
        out_shape=jax.ShapeDtypeStruct((n_tiles * SUBLANES, LANES), _F32),
        compiler_params=pltpu.CompilerParams(dimension_semantics=("arbitrary",), vmem_limit_bytes=VMEM_LIMIT),
        name="moe_experts",
    )(be, nv, src, dst, hn_all, wg, wu, wd)


def _final(prev, g, *, row0, n, tile):
    n_all = prev[0].shape[0]
    blk0 = row0 // tile
    row = lambda i: (blk0 + i, 0)
    return pl.pallas_call(
        _final_kernel,
        grid=(n // tile,),
        in_specs=_prev_layer_specs(tile, row, n_all) + [pl.BlockSpec((1, D_MODEL), lambda i: (0, 0))],
        out_specs=pl.BlockSpec((tile, D_MODEL), lambda i: (i, 0)),
        out_shape=jax.ShapeDtypeStruct((n, D_MODEL), _F32),
        compiler_params=pltpu.CompilerParams(dimension_semantics=("arbitrary",)),
        name="final_norm",
    )(*_prev_layer_args(prev), g)


def kernel(x_prompt, x_sample, state_pool, state_conv, mix_norm_g, w_in, pool_w, pool_scale, sgu_norm_g, sgu_norm_b, sgu_w, sgu_b, conv_w, conv_b, conv_norm_g, conv_norm_b, w_branch_pool, w_branch_sgu, w_branch_conv, w_out, ffn_norm_g, w_group, b_group, w_router, b_router, w_gate_e, w_up_e, w_down_e, final_norm_g):
    nb, seq, _ = x_prompt.shape
    nseq, nstep, _ = x_sample.shape
    n_prompt = nb * seq
    n_sample = nseq * nstep
    n_all = n_prompt + n_sample
    assert seq % TQ == 0 and TQ % CHUNK == 0 and nstep < CHUNK and n_sample % TQ == 0
    assert n_prompt % n_sample == 0 and n_all % 512 == 0 and n_sample % MOE_BLOCK == 0
    n_blocks = (2 * n_all) // MOE_BLOCK + N_EXPERTS
    n_rows = n_blocks * MOE_BLOCK

    w = dict(nstep=nstep, mix_norm_g=mix_norm_g, w_in=w_in, pool_w=pool_w, pool_scale=pool_scale,
             sgu_norm_g=sgu_norm_g, sgu_norm_b=sgu_norm_b, sgu_w=sgu_w, sgu_b=sgu_b, conv_w=conv_w, conv_b=conv_b,
             conv_norm_g=conv_norm_g, conv_norm_b=conv_norm_b, w_branch_pool=w_branch_pool,
             w_branch_sgu=w_branch_sgu, w_branch_conv=w_branch_conv, w_out=w_out, ffn_norm_g=ffn_norm_g,
             w_group=w_group, b_group=b_group, w_router=w_router, b_router=b_router)

    xp = x_prompt.reshape(n_prompt, D_MODEL)
    xs = x_sample.transpose(1, 0, 2).reshape(n_sample, D_MODEL)
    sp_t = state_pool.transpose(0, 2, 1, 3)
    sc_t = state_conv.transpose(0, 2, 1, 3)

    pool_p, pool_s, conv_p, conv_s, chunk_v = [], [], [], [], []
    prev = None
    ops_p, ops_s = _stacked_weights(w)
    for l in range(DEPTH):
        h_s, hn_s, route_s, cnt_s, pnew, cnew, vn = _mixer_sample(prev, xs, sp_t, sc_t, ops_s, l, nseq=nseq,
                                                                  nstep=nstep, n_prompt=n_prompt)
        h, hn, route, cnt, pstate, cstate = _mixer_prompt(prev, xp, (h_s, hn_s, route_s, cnt_s), ops_p, l,
                                                          nb=nb, seq=seq, n_all=n_all)
        pos, binfo = _plan(route, cnt, n_blocks=n_blocks)
        src, dst = _rows(pos[:, 0], pos[:, 1], n_rows=n_rows)
        o2 = _moe(binfo[0, :n_blocks], binfo[1, :n_blocks], src, dst, hn, w_gate_e, w_up_e, w_down_e, l,
                  n_blocks=n_blocks)
        prev = (h, o2, route)
        to_seq = lambda a: a.reshape(nstep, nseq, W_BR).transpose(1, 0, 2)
        pool_p.append(pstate)
        conv_p.append(cstate)
        pool_s.append(jnp.concatenate([state_pool[l][:, nstep:], to_seq(pnew)], axis=1))
        conv_s.append(jnp.concatenate([state_conv[l][:, nstep:], to_seq(cnew)], axis=1))
        chunk_v.append(to_seq(vn))

    g = final_norm_g[None]
    y_prompt = _final(prev, g, row0=0, n=n_prompt, tile=512).reshape(nb, seq, D_MODEL)
    y_sample = _final(prev, g, row0=n_prompt, n=n_sample, tile=n_sample)
    y_sample = y_sample.reshape(nstep, nseq, D_MODEL).transpose(1, 0, 2)
    return (y_prompt, y_sample, jnp.stack(pool_p), jnp.stack(pool_s), jnp.stack(conv_p), jnp.stack(conv_s),
            jnp.stack(chunk_v))
```

```python
import functools

import jax
import jax.numpy as jnp
from jax import lax
from jax.experimental import pallas as pl
from jax.experimental.pallas import tpu as pltpu

D_MODEL = 1024
DEPTH = 4
PAST_LEN = 16384
W_BR = 512
POOL_WINDOWS = (2, 4, 8, 16)
POOL_GC = 128
POOL_HIST = 15
SGU_HEADS = 4
SGU_HC = 128
CHUNK = 128
CONV_WIDTH = 31
CONV_HIST = 30
N_GROUPS = 4
EPG = 8
N_EXPERTS = 32
D_EXPERT = 512
MOE_BLOCK = 128
EPS = 1e-6

LANES = 128
SUBLANES = 8
VMEM_LIMIT = 56 * 1024 * 1024

TQ = 256
ROUTE_COLS = 8
EXP_LANE0 = N_GROUPS
CONV_RB = 64
PACK_ROWS = D_MODEL // 2 // LANES
XG_STRIDE = MOE_BLOCK + SUBLANES
PLAN_STEPS = 3
ROWS_UNROLL = 16
POOL_PAD = 16
CONV_PAD = 32

_MXU = jnp.bfloat16
_F32 = jnp.float32


def _dot(a, b):
    return jnp.dot(a.astype(_MXU), b.astype(_MXU), preferred_element_type=_F32)


def _rms_norm(x, g):
    return x * lax.rsqrt(jnp.mean(x * x, axis=-1, keepdims=True) + EPS) * g


def _layer_norm(x, g, b):
    mu = jnp.mean(x, axis=-1, keepdims=True)
    xc = x - mu
    var = jnp.mean(xc * xc, axis=-1, keepdims=True)
    return xc * lax.rsqrt(var + EPS) * g + b


def _const_spec(shape):
    nd = len(shape)
    return pl.BlockSpec(shape, lambda *_: (0,) * nd, pipeline_mode=pl.Buffered(1))


def _tile_rows_to_tokens(ref, start, n, stride):
    return jnp.concatenate([ref[pl.ds(start + j, n, stride=stride), :] for j in range(SUBLANES)], axis=-1)


def _mixer_input(in_refs, combine, n):
    if not combine:
        return in_refs[0][...]
    h_ref, o2a_ref, o2b_ref, rprev_ref = in_refs
    g0 = rprev_ref[:, 4:5]
    g1 = rprev_ref[:, 5:6]
    o0 = _tile_rows_to_tokens(o2a_ref, 0, n, SUBLANES)
    o1 = _tile_rows_to_tokens(o2b_ref, 0, n, SUBLANES)
    return h_ref[...] + (g0 * o0 + g1 * o1)


def _route(hn, wrt_ref, brt_ref, ltri_ref, cnt_acc, route_ref):
    n = hn.shape[0]
    logits = _dot(hn, wrt_ref[...]) + brt_ref[...]
    lane = lax.broadcasted_iota(jnp.int32, (n, LANES), 1).astype(_F32)
    neg = jnp.float32(-jnp.inf)
    is_grp = lane < N_GROUPS
    gl = jnp.where(is_grp, logits, neg)
    gmax = jnp.max(gl, axis=-1, keepdims=True)
    grp = jnp.min(jnp.where(gl == gmax, lane, float(LANES)), axis=-1, keepdims=True)
    gsum = jnp.sum(jnp.where(is_grp, jnp.exp(logits - gmax), 0.0), axis=-1, keepdims=True)
    grp_w = 1.0 / gsum
    lo = EXP_LANE0 + grp * EPG
    el = jnp.where((lane >= lo) & (lane < lo + EPG), logits, neg)
    v0 = jnp.max(el, axis=-1, keepdims=True)
    i0 = jnp.min(jnp.where(el == v0, lane, float(LANES)), axis=-1, keepdims=True)
    el1 = jnp.where(lane == i0, neg, el)
    v1 = jnp.max(el1, axis=-1, keepdims=True)
    i1 = jnp.min(jnp.where(el1 == v1, lane, float(LANES)), axis=-1, keepdims=True)
    ex = jnp.exp(v1 - v0)
    den = 1.0 + ex
    g0 = grp_w / den
    g1 = grp_w * ex / den
    oh0 = lane == i0
    oh1 = lane == i1
    oh = jnp.where(oh0 | oh1, 1.0, 0.0)
    before = _dot(ltri_ref[...], oh) + cnt_acc[...]
    r0 = jnp.sum(jnp.where(oh0, before, 0.0), axis=-1, keepdims=True)
    r1 = jnp.sum(jnp.where(oh1, before, 0.0), axis=-1, keepdims=True)
    cnt_acc[...] = cnt_acc[...] + jnp.sum(oh, axis=0, keepdims=True)
    route_ref[:, 0:1] = i0 - EXP_LANE0
    route_ref[:, 1:2] = i1 - EXP_LANE0
    route_ref[:, 2:3] = r0
    route_ref[:, 3:4] = r1
    route_ref[:, 4:5] = g0
    route_ref[:, 5:6] = g1
    route_ref[:, 6:8] = jnp.zeros((n, 2), _F32)


def _gate_slab(xn, win_ref, k, nslab):
    width = 3 * D_MODEL // nslab
    lo = 5 * W_BR + k * width
    return _dot(xn, win_ref[:, lo:lo + width])


def _mixer_tail(x, xn, ya, yb, yc, gate_pre, wbp_ref, wbs_ref, wbc_ref, wout_ref, gffn_ref,
                wrt_ref, brt_ref, ltri_ref, cnt_acc, hout_ref, hn_ref, route_ref):
    n = x.shape[0]
    gates = jax.nn.sigmoid(gate_pre)
    merged = gates[:, 0:D_MODEL] * _dot(ya, wbp_ref[...])
    merged = merged + gates[:, D_MODEL:2 * D_MODEL] * _dot(yb, wbs_ref[...])
    merged = merged + gates[:, 2 * D_MODEL:3 * D_MODEL] * _dot(yc, wbc_ref[...])
    h = x + _dot(merged, wout_ref[...])
    hout_ref[...] = h
    hn = _rms_norm(h, gffn_ref[...])
    half = D_MODEL // 2
    lo = lax.bitcast_convert_type(hn[:, :half].astype(_MXU).astype(_F32), jnp.uint32)
    hi = lax.bitcast_convert_type(hn[:, half:].astype(_MXU).astype(_F32), jnp.uint32)
    words = jnp.bitwise_or(jnp.bitwise_and(hi, jnp.uint32(0xFFFF0000)), jnp.right_shift(lo, jnp.uint32(16)))
    for j in range(PACK_ROWS):
        hn_ref[pl.ds(j, n, stride=PACK_ROWS), :] = words[:, j * LANES:(j + 1) * LANES]
    _route(hn, wrt_ref, brt_ref, ltri_ref, cnt_acc, route_ref)


def _mixer_prompt_kernel(*refs, combine, nt, n_main):
    n_in = 4 if combine else 1
    in_refs = refs[:n_in]
    next_refs = refs[n_in:2 * n_in]
    hs_ref, hns_ref, routes_ref = refs[2 * n_in:2 * n_in + 3]
    hout_ref, hn_ref, route_ref = refs[-12:-9]
    step = pl.program_id(0)

    @pl.when(step >= n_main)
    def _():
        hout_ref[...] = hs_ref[...]
        hn_ref[...] = hns_ref[...]
        route_ref[...] = routes_ref[...]

    @pl.when(step < n_main)
    def _():
        _mixer_prompt_step(in_refs, next_refs, refs[2 * n_in + 3:], combine, nt)


def _mixer_prompt_step(first_refs, next_refs, refs, combine, nt):
    (cnt_in_ref, gmix_ref, win_ref, poolw_ref, pscale_ref, sgug_ref, sgub_ref, sguw_ref, sgubias_ref,
     convw_ref, convb_ref, cng_ref, cnb_ref, wbp_ref, wbs_ref, wbc_ref, wout_ref,
     gffn_ref, wrt_ref, brt_ref, ltri_ref,
     hout_ref, hn_ref, route_ref, cnt_ref, pstate_ref, cstate_ref,
     pbuf, cbuf, ybuf, cnt_acc, xbuf, xnbuf) = refs
    step = pl.program_id(0)
    i = lax.rem(step, nt)
    par = lax.rem(step, 2)

    def head(in_refs, slot):
        rows = _mixer_input(in_refs, combine, TQ)
        xbuf[slot] = rows
        xnbuf[slot] = _rms_norm(rows, gmix_ref[...]).astype(_MXU)

    @pl.when(step == 0)
    def _():
        cnt_acc[...] = cnt_in_ref[...]
        head(first_refs, 0)

    @pl.when(i == 0)
    def _():
        pbuf[0:POOL_PAD, :] = jnp.zeros((POOL_PAD, W_BR), _F32)
        cbuf[0, 0:CONV_PAD, :] = jnp.zeros((CONV_PAD, W_BR), _F32)

    xn = xnbuf[par]

    ca = _dot(xn, win_ref[:, 3 * W_BR:4 * W_BR])
    cg = _dot(xn, win_ref[:, 4 * W_BR:5 * W_BR])
    cbuf[0, CONV_PAD:CONV_PAD + TQ, :] = ca * jax.nn.sigmoid(cg)
    for s in range(1, SUBLANES):
        cbuf[s, SUBLANES:CONV_PAD + TQ, :] = cbuf[0, SUBLANES - s:CONV_PAD + TQ - s, :]
    n_lb = W_BR // LANES
    other = [lambda: _dot(xn, win_ref[:, 0:W_BR]), lambda: _dot(xn, win_ref[:, W_BR:2 * W_BR]),
             lambda: _dot(xn, win_ref[:, 2 * W_BR:3 * W_BR]), lambda: _gate_slab(xn, win_ref, 0, 1)]
    done = []
    n_og = TQ // SUBLANES
    pad_g = CONV_PAD // SUBLANES
    n_q = -(-CONV_WIDTH // SUBLANES)
    for lb in range(n_lb):
        sl = slice(lb * LANES, (lb + 1) * LANES)
        taps = [jnp.broadcast_to(convw_ref[CONV_HIST - j:CONV_WIDTH - j, sl], (SUBLANES, LANES))
                for j in range(CONV_WIDTH)]
        bias = jnp.broadcast_to(convb_ref[:, sl], (SUBLANES, LANES))
        open_acc = {}
        for g in range(pad_g - n_q + 1, pad_g + n_og):
            data = [cbuf[s, g * SUBLANES:(g + 1) * SUBLANES, sl] for s in range(SUBLANES)]
            for q in range(n_q):
                o = g - pad_g + q
                if 0 <= o < n_og:
                    acc = open_acc.get(o, bias)
                    for s in range(SUBLANES):
                        if q * SUBLANES + s < CONV_WIDTH:
                            acc = acc + data[s] * taps[q * SUBLANES + s]
                    open_acc[o] = acc
            o = g - pad_g
            if 0 <= o < n_og:
                ybuf[o * SUBLANES:(o + 1) * SUBLANES, sl] = open_acc.pop(o)
        done.append(other[lb]())
    p, u, v, gate_pre = done
    yc = jax.nn.silu(_layer_norm(ybuf[...], cng_ref[...], cnb_ref[...]))
    cstate_ref[0] = cbuf[0, CONV_PAD + TQ - CONV_HIST:CONV_PAD + TQ, :]
    cbuf[0, 0:CONV_PAD, :] = cbuf[0, TQ:TQ + CONV_PAD, :]

    pbuf[POOL_PAD:POOL_PAD + TQ, :] = p
    pos = i * TQ + lax.broadcasted_iota(jnp.int32, (TQ, 1), 0)
    ya_parts = []
    for gi, w in enumerate(POOL_WINDOWS):
        sl = slice(gi * POOL_GC, (gi + 1) * POOL_GC)
        acc = pbuf[:, sl]
        shift = 1
        while shift < w:
            acc = acc + pltpu.roll(acc, shift, axis=0)
            shift *= 2
        acc = acc[POOL_PAD:POOL_PAD + TQ]
        cnt = jnp.minimum(pos + 1, w).astype(_F32)
        d = acc / cnt - p[:, sl]
        ya_parts.append(_dot(d, poolw_ref[gi]))
    ya = jnp.concatenate(ya_parts, axis=-1) * pscale_ref[...]
    pstate_ref[0] = pbuf[POOL_PAD + TQ - POOL_HIST:POOL_PAD + TQ, :]
    pbuf[0:POOL_PAD, :] = pbuf[TQ:TQ + POOL_PAD, :]

    vn = _layer_norm(v, sgug_ref[...], sgub_ref[...])
    rows = []
    for c in range(TQ // CHUNK):
        heads = [_dot(sguw_ref[hd], vn[c * CHUNK:(c + 1) * CHUNK, hd * SGU_HC:(hd + 1) * SGU_HC])
                 for hd in range(SGU_HEADS)]
        rows.append(jnp.concatenate(heads, axis=-1) + sgubias_ref[...])
    yb = u * jnp.concatenate(rows, axis=0)

    _mixer_tail(xbuf[par], xn, ya, yb, yc, gate_pre, wbp_ref, wbs_ref, wbc_ref, wout_ref,
                gffn_ref, wrt_ref, brt_ref, ltri_ref, cnt_acc, hout_ref, hn_ref, route_ref)
    cnt_ref[...] = cnt_acc[...]
    head(next_refs, 1 - par)


def _mixer_sample_kernel(*refs, combine, nseq, nstep):
    n_in = 4 if combine else 1
    in_refs = refs[:n_in]
    (sp_ref, sc_ref, gmix_ref, win_ref, poolw_ref, pscale_ref, sgug_ref, sgub_ref, sguw_ref, sgubias_ref,
     convw_ref, convb_ref, cng_ref, cnb_ref, wbp_ref, wbs_ref, wbc_ref, wout_ref,
     gffn_ref, wrt_ref, brt_ref, ltri_ref,
     hout_ref, hn_ref, route_ref, cnt_ref, pnew_ref, cnew_ref, vn_ref,
     ybuf, cnt_acc) = refs[n_in:]
    n = nseq * nstep
    cnt_acc[...] = jnp.zeros_like(cnt_acc)
    x = _mixer_input(in_refs, combine, n)
    xn = _rms_norm(x, gmix_ref[...]).astype(_MXU)

    def slab(a, t):
        return a[t * nseq:(t + 1) * nseq]

    p = _dot(xn, win_ref[:, 0:W_BR])
    pnew_ref[...] = p
    ya_rows = []
    for t in range(nstep):
        parts = []
        for gi, w in enumerate(POOL_WINDOWS):
            sl = slice(gi * POOL_GC, (gi + 1) * POOL_GC)
            acc = slab(p, t)[:, sl]
            for j in range(1, w):
                k = POOL_HIST + t - j
                acc = acc + (sp_ref[k, :, sl] if k < POOL_HIST else slab(p, k - POOL_HIST)[:, sl])
            cnt = float(min(PAST_LEN + t + 1, w))
            d = acc / cnt - slab(p, t)[:, sl]
            parts.append(_dot(d, poolw_ref[gi]))
        ya_rows.append(jnp.concatenate(parts, axis=-1))
    ya = jnp.concatenate(ya_rows, axis=0) * pscale_ref[...]

    u = _dot(xn, win_ref[:, W_BR:2 * W_BR])
    v = _dot(xn, win_ref[:, 2 * W_BR:3 * W_BR])
    vn = _layer_norm(v, sgug_ref[...], sgub_ref[...])
    vn_ref[...] = vn
    s_rows = []
    for t in range(nstep):
        acc = jnp.broadcast_to(sgubias_ref[t:t + 1, :], (nseq, W_BR))
        for s in range(t + 1):
            acc = acc + sguw_ref[t * nstep + s:t * nstep + s + 1, :] * slab(vn, s)
        s_rows.append(acc)
    yb = u * jnp.concatenate(s_rows, axis=0)

    ca = _dot(xn, win_ref[:, 3 * W_BR:4 * W_BR])
    cg = _dot(xn, win_ref[:, 4 * W_BR:5 * W_BR])
    cnew_ref[...] = ca * jax.nn.sigmoid(cg)
    for t in range(nstep):
        for lb in range(W_BR // LANES):
            sl = slice(lb * LANES, (lb + 1) * LANES)
            acc = jnp.broadcast_to(convb_ref[:, sl], (nseq, LANES))
            for kw in range(CONV_WIDTH):
                k = t + kw
                row = sc_ref[k, :, sl] if k < CONV_HIST else cnew_ref[(k - CONV_HIST) * nseq:(k - CONV_HIST + 1) * nseq, sl]
                acc = acc + row * convw_ref[kw:kw + 1, sl]
            ybuf[t * nseq:(t + 1) * nseq, sl] = acc
    yc = jax.nn.silu(_layer_norm(ybuf[...], cng_ref[...], cnb_ref[...]))

    _mixer_tail(x, xn, ya, yb, yc, _gate_slab(xn, win_ref, 0, 1), wbp_ref, wbs_ref, wbc_ref, wout_ref,
                gffn_ref, wrt_ref, brt_ref, ltri_ref, cnt_acc, hout_ref, hn_ref, route_ref)
    cnt_ref[...] = cnt_acc[...]


def _lane_to_sublane(row):
    sub = lax.broadcasted_iota(jnp.int32, (LANES, LANES), 0)
    lane = lax.broadcasted_iota(jnp.int32, (LANES, LANES), 1)
    return jnp.sum(jnp.where(sub == lane, jnp.broadcast_to(row, (LANES, LANES)), 0.0), axis=1, keepdims=True)


def _plan_kernel(route_ref, cnt_ref, pos_ref, binfo_ref, *, nblk_lanes):
    cnt = cnt_ref[...]
    padded = jnp.floor((cnt + (MOE_BLOCK - 1)) * (1.0 / MOE_BLOCK)) * MOE_BLOCK
    lane1 = lax.broadcasted_iota(jnp.int32, (SUBLANES, LANES), 1)
    pends = jnp.broadcast_to(padded, (SUBLANES, LANES))
    sh = 1
    while sh < LANES:
        pends = pends + jnp.where(lane1 >= sh, pltpu.roll(pends, sh, axis=1), 0.0)
        sh *= 2
    pends = pends[0:1]
    pstart = pends - padded

    n = route_ref.shape[0]
    lane = lax.broadcasted_iota(jnp.int32, (n, LANES), 1).astype(_F32)
    for k in range(2):
        e_lane = route_ref[:, k:k + 1] + EXP_LANE0
        base = jnp.sum(jnp.where(lane == e_lane, pstart, 0.0), axis=-1, keepdims=True)
        pos_ref[:, k:k + 1] = (base + route_ref[:, 2 + k:3 + k]).astype(jnp.int32)

    @pl.when(pl.program_id(0) == 0)
    def _():
        pends_c = _lane_to_sublane(pends)
        pstart_c = _lane_to_sublane(pstart)
        cnt_c = _lane_to_sublane(cnt)
        sub = lax.broadcasted_iota(jnp.int32, (LANES, LANES), 0)
        is_exp = (sub >= EXP_LANE0) & (sub < EXP_LANE0 + N_EXPERTS)
        binfo_ref[...] = jnp.zeros_like(binfo_ref)
        for q in range(nblk_lanes // LANES):
            bstart = ((lax.broadcasted_iota(jnp.int32, (LANES, LANES), 1) + q * LANES) * MOE_BLOCK).astype(_F32)
            done = jnp.where(is_exp & (pends_c <= bstart), 1.0, 0.0)
            bexp = jnp.minimum(jnp.sum(done, axis=0, keepdims=True), float(N_EXPERTS - 1))
            inside = is_exp & (pstart_c <= bstart) & (bstart < pends_c)
            valid = jnp.clip(pstart_c + cnt_c - bstart, 0.0, float(MOE_BLOCK))
            nval = jnp.sum(jnp.where(inside, valid, 0.0), axis=0, keepdims=True)
            binfo_ref[0:1, q * LANES:(q + 1) * LANES] = bexp.astype(jnp.int32)
            binfo_ref[1:2, q * LANES:(q + 1) * LANES] = nval.astype(jnp.int32)


def _rows_kernel(pos0_ref, pos1_ref, src_pad_hbm, dst_pad_hbm, src_ref, dst_ref, sem, *, n_tok):
    unroll = ROWS_UNROLL
    fill_src = pltpu.make_async_copy(src_pad_hbm, src_ref, sem.at[0])
    fill_dst = pltpu.make_async_copy(dst_pad_hbm, dst_ref, sem.at[1])
    fill_src.start()
    fill_dst.start()
    fill_src.wait()
    fill_dst.wait()

    def put(i, c):
        for k in range(unroll):
            t = i * unroll + k
            p0 = pos0_ref[t]
            p1 = pos1_ref[t]
            src_ref[p0] = t
            src_ref[p1] = t
            dst_ref[p0] = t
            dst_ref[p1] = n_tok + t
        return c
    lax.fori_loop(0, n_tok // unroll, put, 0)


def _moe_kernel(be_ref, nv_ref, src_ref, dst_ref, hn_ref, wg_ref, wu_ref, wd_ref, o2_hbm,
                xg, obuf, wg_bf, wu_bf, wd_bf, ssem, *, n_tok):
    b = pl.program_id(0)
    nblk = pl.num_programs(0)
    slot = b % 2
    tile = SUBLANES
    spare0 = 2 * n_tok * tile

    def scatter_copy(row, dst, s):
        return pltpu.make_async_copy(obuf.at[s, pl.ds(row * tile, tile)],
                                     o2_hbm.at[pl.ds(pl.multiple_of(dst * tile, tile), tile)], ssem.at[s])

    def spare_copy(half):
        return pltpu.make_async_copy(obuf.at[1], o2_hbm.at[pl.ds(spare0 + half * MOE_BLOCK * tile, MOE_BLOCK * tile)],
                                     ssem.at[1])

    def gather_block(blk):
        for r in range(MOE_BLOCK):
            tok = src_ref[blk * MOE_BLOCK + r]
            xg[pl.ds(r, PACK_ROWS, stride=XG_STRIDE), :] = hn_ref[pl.ds(pl.multiple_of(tok * PACK_ROWS, PACK_ROWS),
                                                                         PACK_ROWS), :]
        lo, hi = [], []
        for j in range(PACK_ROWS):
            words = xg[j * XG_STRIDE:j * XG_STRIDE + MOE_BLOCK, :]
            lo.append(lax.bitcast_convert_type(jnp.left_shift(words, jnp.uint32(16)), _F32))
            hi.append(lax.bitcast_convert_type(jnp.bitwise_and(words, jnp.uint32(0xFFFF0000)), _F32))
        return jnp.concatenate(lo + hi, axis=-1).astype(_MXU)

    def start_scatter(blk, s):
        for r in range(MOE_BLOCK):
            scatter_copy(r, dst_ref[blk * MOE_BLOCK + r], s).start(priority=r % 2)

    def wait_scatter(s):
        for r in range(MOE_BLOCK):
            scatter_copy(0, 0, s).wait()

    def used(blk):
        return nv_ref[jnp.clip(blk, 0, nblk - 1)] > 0

    @pl.when(b == 0)
    def _():
        obuf[1] = jnp.zeros(obuf.shape[1:], _F32)
        spare_copy(0).start()
        spare_copy(1).start()
        spare_copy(0).wait()
        spare_copy(1).wait()

    @pl.when((b >= 2) & used(b - 2))
    def _():
        wait_scatter(slot)

    @pl.when(used(b))
    def _():
        @pl.when((b == 0) | (be_ref[b] != be_ref[jnp.maximum(b - 1, 0)]))
        def _():
            wg_bf[...] = wg_ref[...].astype(_MXU)
            wu_bf[...] = wu_ref[...].astype(_MXU)
            wd_bf[...] = wd_ref[...].astype(_MXU)

        x = gather_block(b)
        hb = jax.nn.silu(_dot(x, wg_bf[...])) * _dot(x, wu_bf[...])
        o = _dot(hb, wd_bf[...])
        for j in range(SUBLANES):
            obuf[slot, pl.ds(j, MOE_BLOCK, stride=SUBLANES), :] = o[:, j * LANES:(j + 1) * LANES]
        start_scatter(b, slot)

    @pl.when(b == nblk - 1)
    def _():
        @pl.when((b >= 1) & used(b - 1))
        def _():
            wait_scatter(1 - slot)

        @pl.when(used(b))
        def _():
            wait_scatter(slot)


def _final_kernel(h_ref, o2a_ref, o2b_ref, rprev_ref, g_ref, y_ref):
    n = h_ref.shape[0]
    y_ref[...] = _rms_norm(_mixer_input((h_ref, o2a_ref, o2b_ref, rprev_ref), True, n), g_ref[...])


def _stacked_weights(w):
    nstep = w['nstep']
    depth = w['w_in'].shape[0]
    sgu_w = w['sgu_w']
    sgu_b = w['sgu_b']
    vec = lambda a: a[:, None, :]
    pad = jnp.zeros((depth, D_MODEL, LANES - N_GROUPS - N_EXPERTS), _F32)
    w_rt = jnp.concatenate([w['w_group'], w['w_router'], pad], axis=-1)
    b_rt = jnp.concatenate([w['b_group'], w['b_router'], pad[:, 0, :]], axis=-1)
    head = (vec(w['mix_norm_g']), w['w_in'].astype(_MXU), w['pool_w'].astype(_MXU),
            vec(w['pool_scale']), vec(w['sgu_norm_g']), vec(w['sgu_norm_b']))
    tail = (w['conv_w'], vec(w['conv_b']), vec(w['conv_norm_g']), vec(w['conv_norm_b']),
            w['w_branch_pool'].astype(_MXU), w['w_branch_sgu'].astype(_MXU),
            w['w_branch_conv'].astype(_MXU), w['w_out'].astype(_MXU),
            vec(w['ffn_norm_g']), w_rt.astype(_MXU), vec(b_rt))
    sgu_p = (jnp.tril(sgu_w).astype(_MXU), jnp.repeat(sgu_b.transpose(0, 2, 1), SGU_HC, axis=2))
    corner = jnp.tril(sgu_w[:, :, :nstep, :nstep]).transpose(0, 2, 3, 1)
    sgu_s = (jnp.repeat(corner.reshape(depth, nstep * nstep, SGU_HEADS), SGU_HC, axis=2),
             jnp.repeat(sgu_b[:, :, :nstep].transpose(0, 2, 1), SGU_HC, axis=2))
    return head + sgu_p + tail, head + sgu_s + tail


def _weight_specs(ops, l):
    def spec(o):
        nd = o.ndim - 1
        return pl.BlockSpec((None,) + o.shape[1:], lambda *_: (l,) + (0,) * nd, pipeline_mode=pl.Buffered(1))
    return [spec(o) for o in ops]


def _ltri(n):
    r = lax.broadcasted_iota(jnp.int32, (n, n), 0)
    c = lax.broadcasted_iota(jnp.int32, (n, n), 1)
    return (c < r).astype(_MXU)


def _prev_layer_specs(tile, index, n_all):
    def slot1(*ids):
        i, z = index(*ids)
        return i + n_all // tile, z
    return [pl.BlockSpec((tile, D_MODEL), index), pl.BlockSpec((tile * SUBLANES, LANES), index),
            pl.BlockSpec((tile * SUBLANES, LANES), slot1), pl.BlockSpec((tile, ROUTE_COLS), index)]


def _prev_layer_args(prev):
    h, o2, route = prev
    return (h, o2, o2, route)


def _mixer_prompt(prev, x, sample, ops, l, *, nb, seq, n_all):
    combine = prev is not None
    nt = seq // TQ
    n_main = nb * nt
    n_steps = n_all // TQ
    h_s, hn_s, route_s, cnt_s = sample
    row = lambda s: (s, 0)
    first_row = lambda s: (0, 0)
    next_row = lambda s: (jnp.minimum(s + 1, n_main - 1), 0)
    tail_row = lambda s: (jnp.maximum(s - n_main, 0), 0)
    seq_blk = lambda s: (jnp.minimum(s // nt, nb - 1), 0, 0)
    if combine:
        inputs = _prev_layer_args(prev) * 2
        in_specs = _prev_layer_specs(TQ, first_row, n_all) + _prev_layer_specs(TQ, next_row, n_all)
    else:
        inputs = (x, x)
        in_specs = [pl.BlockSpec((TQ, D_MODEL), first_row), pl.BlockSpec((TQ, D_MODEL), next_row)]
    in_specs += [pl.BlockSpec((TQ, D_MODEL), tail_row), pl.BlockSpec((TQ * PACK_ROWS, LANES), tail_row),
                 pl.BlockSpec((TQ, ROUTE_COLS), tail_row), _const_spec(cnt_s.shape)]
    ltri = _ltri(TQ)
    out_shape = (jax.ShapeDtypeStruct((n_all, D_MODEL), _F32),
                 jax.ShapeDtypeStruct((n_all * PACK_ROWS, LANES), jnp.uint32),
                 jax.ShapeDtypeStruct((n_all, ROUTE_COLS), _F32),
                 jax.ShapeDtypeStruct((1, LANES), _F32),
                 jax.ShapeDtypeStruct((nb, POOL_HIST, W_BR), _F32),
                 jax.ShapeDtypeStruct((nb, CONV_HIST, W_BR), _F32))
    out_specs = (pl.BlockSpec((TQ, D_MODEL), row), pl.BlockSpec((TQ * PACK_ROWS, LANES), row),
                 pl.BlockSpec((TQ, ROUTE_COLS), row), pl.BlockSpec((1, LANES), lambda s: (0, 0)),
                 pl.BlockSpec((1, POOL_HIST, W_BR), seq_blk), pl.BlockSpec((1, CONV_HIST, W_BR), seq_blk))
    return pl.pallas_call(
        functools.partial(_mixer_prompt_kernel, combine=combine, nt=nt, n_main=n_main),
        grid=(n_steps,),
        in_specs=in_specs + _weight_specs(ops, l) + [_const_spec(ltri.shape)],
        out_specs=out_specs,
        out_shape=out_shape,
        scratch_shapes=[pltpu.VMEM((POOL_PAD + TQ, W_BR), _F32), pltpu.VMEM((SUBLANES, CONV_PAD + TQ, W_BR), _F32),
                        pltpu.VMEM((TQ, W_BR), _F32), pltpu.VMEM((1, LANES), _F32),
                        pltpu.VMEM((2, TQ, D_MODEL), _F32), pltpu.VMEM((2, TQ, D_MODEL), _MXU)],
        compiler_params=pltpu.CompilerParams(dimension_semantics=("arbitrary",), vmem_limit_bytes=VMEM_LIMIT),
        name="mixer_prompt",
    )(*inputs, h_s, hn_s, route_s, cnt_s, *ops, ltri)


def _mixer_sample(prev, x, sp, sc, ops, l, *, nseq, nstep, n_prompt):
    combine = prev is not None
    n = nseq * nstep
    blk = n_prompt // n
    row = lambda i: (blk, 0)
    zero = lambda i: (0, 0)
    if combine:
        inputs = _prev_layer_args(prev)
        in_specs = _prev_layer_specs(n, row, n_prompt + n)
    else:
        inputs = (x,)
        in_specs = [pl.BlockSpec((n, D_MODEL), zero)]
    in_specs += _weight_specs((sp, sc), l)
    ltri = _ltri(n)
    out_shape = (jax.ShapeDtypeStruct((n, D_MODEL), _F32), jax.ShapeDtypeStruct((n * PACK_ROWS, LANES), jnp.uint32),
                 jax.ShapeDtypeStruct((n, ROUTE_COLS), _F32), jax.ShapeDtypeStruct((1, LANES), _F32),
                 jax.ShapeDtypeStruct((n, W_BR), _F32), jax.ShapeDtypeStruct((n, W_BR), _F32),
                 jax.ShapeDtypeStruct((n, W_BR), _F32))
    out_specs = (pl.BlockSpec((n, D_MODEL), zero), pl.BlockSpec((n * PACK_ROWS, LANES), zero),
                 pl.BlockSpec((n, ROUTE_COLS), zero), pl.BlockSpec((1, LANES), zero),
                 pl.BlockSpec((n, W_BR), zero), pl.BlockSpec((n, W_BR), zero), pl.BlockSpec((n, W_BR), zero))
    return pl.pallas_call(
        functools.partial(_mixer_sample_kernel, combine=combine, nseq=nseq, nstep=nstep),
        grid=(1,),
        in_specs=in_specs + _weight_specs(ops, l) + [_const_spec(ltri.shape)],
        out_specs=out_specs,
        out_shape=out_shape,
        scratch_shapes=[pltpu.VMEM((n, W_BR), _F32), pltpu.VMEM((1, LANES), _F32)],
        compiler_params=pltpu.CompilerParams(dimension_semantics=("arbitrary",), vmem_limit_bytes=VMEM_LIMIT),
        name="mixer_sample",
    )(*inputs, sp, sc, *ops, ltri)


def _plan(route, cnt, *, n_blocks):
    n_all = route.shape[0]
    tile = n_all // PLAN_STEPS
    assert tile * PLAN_STEPS == n_all and tile % SUBLANES == 0
    nblk_lanes = -(-n_blocks // LANES) * LANES
    return pl.pallas_call(
        functools.partial(_plan_kernel, nblk_lanes=nblk_lanes),
        grid=(n_all // tile,),
        in_specs=[pl.BlockSpec((tile, ROUTE_COLS), lambda i: (i, 0)), pl.BlockSpec((1, LANES), lambda i: (0, 0))],
        out_specs=(pl.BlockSpec((tile, 2), lambda i: (i, 0)), pl.BlockSpec((SUBLANES, nblk_lanes), lambda i: (0, 0))),
        out_shape=(jax.ShapeDtypeStruct((n_all, 2), jnp.int32), jax.ShapeDtypeStruct((SUBLANES, nblk_lanes), jnp.int32)),
        compiler_params=pltpu.CompilerParams(dimension_semantics=("arbitrary",)),
        name="moe_plan",
    )(route, cnt)


def _rows(pos0, pos1, *, n_rows):
    n_tok = pos0.shape[0]
    src_pad = jnp.full((n_rows,), n_tok - 1, jnp.int32)
    dst_pad = 2 * n_tok + jnp.bitwise_and(jnp.arange(n_rows, dtype=jnp.int32), 2 * MOE_BLOCK - 1)
    smem = pl.BlockSpec(memory_space=pltpu.SMEM)
    hbm = pl.BlockSpec(memory_space=pl.ANY)
    return pl.pallas_call(
        functools.partial(_rows_kernel, n_tok=n_tok),
        in_specs=[smem, smem, hbm, hbm],
        out_specs=(smem, smem),
        out_shape=(jax.ShapeDtypeStruct((n_rows,), jnp.int32), jax.ShapeDtypeStruct((n_rows,), jnp.int32)),
        scratch_shapes=[pltpu.SemaphoreType.DMA((2,))],
        name="moe_rows",
    )(pos0, pos1, src_pad, dst_pad)


def _moe(be, nv, src, dst, hn_all, wg, wu, wd, l, *, n_blocks):
    n_all = hn_all.shape[0] // PACK_ROWS
    n_tiles = 2 * n_all + 2 * MOE_BLOCK
    wspec_in = pl.BlockSpec((None, None, D_MODEL, D_EXPERT), lambda b, be, *_: (l, be[b], 0, 0))
    wspec_out = pl.BlockSpec((None, None, D_EXPERT, D_MODEL), lambda b, be, *_: (l, be[b], 0, 0))
    grid_spec = pltpu.PrefetchScalarGridSpec(
        num_scalar_prefetch=4,
        grid=(n_blocks,),
        in_specs=[_const_spec(hn_all.shape), wspec_in, wspec_in, wspec_out],
        out_specs=pl.BlockSpec(memory_space=pl.ANY),
        scratch_shapes=[pltpu.VMEM((PACK_ROWS * XG_STRIDE, LANES), jnp.uint32),
                        pltpu.VMEM((2, MOE_BLOCK * SUBLANES, LANES), _F32),
                        pltpu.VMEM((D_MODEL, D_EXPERT), _MXU), pltpu.VMEM((D_MODEL, D_EXPERT), _MXU),
                        pltpu.VMEM((D_EXPERT, D_MODEL), _MXU),
                        pltpu.SemaphoreType.DMA((2,))])
    return pl.pallas_call(
        functools.partial(_moe_kernel, n_tok=n_all),
        grid_spec=grid_spec,
        out_shape=jax.ShapeDtypeStruct((n_tiles * SUBLANES, LANES), _F32),
        compiler_params=pltpu.CompilerParams(dimension_semantics=("arbitrary",), vmem_limit_bytes=VMEM_LIMIT),
        name="moe_experts",
    )(be, nv, src, dst, hn_all, wg, wu, wd)


def _final(prev, g, *, row0, n, tile):
    n_all = prev[0].shape[0]
    blk0 = row0 // tile
    row = lambda i: (blk0 + i, 0)
    return pl.pallas_call(
        _final_kernel,
        grid=(n // tile,),
        in_specs=_prev_layer_specs(tile, row, n_all) + [pl.BlockSpec((1, D_MODEL), lambda i: (0, 0))],
        out_specs=pl.BlockSpec((tile, D_MODEL), lambda i: (i, 0)),
        out_shape=jax.ShapeDtypeStruct((n, D_MODEL), _F32),
        compiler_params=pltpu.CompilerParams(dimension_semantics=("arbitrary",)),
        name="final_norm",
    )(*_prev_layer_args(prev), g)


def kernel(x_prompt, x_sample, state_pool, state_conv, mix_norm_g, w_in, pool_w, pool_scale, sgu_norm_g, sgu_norm_b, sgu_w, sgu_b, conv_w, conv_b, conv_norm_g, conv_norm_b, w_branch_pool, w_branch_sgu, w_branch_conv, w_out, ffn_norm_g, w_group, b_group, w_router, b_router, w_gate_e, w_up_e, w_down_e, final_norm_g):
    nb, seq, _ = x_prompt.shape
    nseq, nstep, _ = x_sample.shape
    n_prompt = nb * seq
    n_sample = nseq * nstep
    n_all = n_prompt + n_sample
    assert seq % TQ == 0 and TQ % CHUNK == 0 and nstep < CHUNK and n_sample % TQ == 0
    assert n_prompt % n_sample == 0 and n_all % 512 == 0 and n_sample % MOE_BLOCK == 0
    n_blocks = (2 * n_all) // MOE_BLOCK + N_EXPERTS
    n_rows = n_blocks * MOE_BLOCK

    w = dict(nstep=nstep, mix_norm_g=mix_norm_g, w_in=w_in, pool_w=pool_w, pool_scale=pool_scale,
             sgu_norm_g=sgu_norm_g, sgu_norm_b=sgu_norm_b, sgu_w=sgu_w, sgu_b=sgu_b, conv_w=conv_w, conv_b=conv_b,
             conv_norm_g=conv_norm_g, conv_norm_b=conv_norm_b, w_branch_pool=w_branch_pool,
             w_branch_sgu=w_branch_sgu, w_branch_conv=w_branch_conv, w_out=w_out, ffn_norm_g=ffn_norm_g,
             w_group=w_group, b_group=b_group, w_router=w_router, b_router=b_router)

    xp = x_prompt.reshape(n_prompt, D_MODEL)
    xs = x_sample.transpose(1, 0, 2).reshape(n_sample, D_MODEL)
    sp_t = state_pool.transpose(0, 2, 1, 3)
    sc_t = state_conv.transpose(0, 2, 1, 3)

    pool_p, pool_s, conv_p, conv_s, chunk_v = [], [], [], [], []
    prev = None
    ops_p, ops_s = _stacked_weights(w)
    for l in range(DEPTH):
        h_s, hn_s, route_s, cnt_s, pnew, cnew, vn = _mixer_sample(prev, xs, sp_t, sc_t, ops_s, l, nseq=nseq,
                                                                  nstep=nstep, n_prompt=n_prompt)
        h, hn, route, cnt, pstate, cstate = _mixer_prompt(prev, xp, (h_s, hn_s, route_s, cnt_s), ops_p, l,
                                                          nb=nb, seq=seq, n_all=n_all)
        pos, binfo = _plan(route, cnt, n_blocks=n_blocks)
        src, dst = _rows(pos[:, 0], pos[:, 1], n_rows=n_rows)
        o2 = _moe(binfo[0, :n_blocks], binfo[1, :n_blocks], src, dst, hn, w_gate_e, w_up_e, w_down_e, l,
                  n_blocks=n_blocks)
        prev = (h, o2, route)
        to_seq = lambda a: a.reshape(nstep, nseq, W_BR).transpose(1, 0, 2)
        pool_p.append(pstate)
        conv_p.append(cstate)
        pool_s.append(jnp.concatenate([state_pool[l][:, nstep:], to_seq(pnew)], axis=1))
        conv_s.append(jnp.concatenate([state_conv[l][:, nstep:], to_seq(cnew)], axis=1))
        chunk_v.append(to_seq(vn))

    g = final_norm_g[None]
    y_prompt = _final(prev, g, row0=0, n=n_prompt, tile=512).reshape(nb, seq, D_MODEL)
    y_sample = _final(prev, g, row0=n_prompt, n=n_sample, tile=n_sample)
    y_sample = y_sample.reshape(nstep, nseq, D_MODEL).transpose(1, 0, 2)
    return (y_prompt, y_sample, jnp.stack(pool_p), jnp.stack(pool_s), jnp.stack(conv_p), jnp.stack(conv_s),
            jnp.stack(chunk_v))
```

```python
import functools

import jax
import jax.numpy as jnp
from jax import lax
from jax.experimental import pallas as pl
from jax.experimental.pallas import tpu as pltpu

D_MODEL = 1024
DEPTH = 4
PAST_LEN = 16384
W_BR = 512
POOL_WINDOWS = (2, 4, 8, 16)
POOL_GC = 128
POOL_HIST = 15
SGU_HEADS = 4
SGU_HC = 128
CHUNK = 128
CONV_WIDTH = 31
CONV_HIST = 30
N_GROUPS = 4
EPG = 8
N_EXPERTS = 32
D_EXPERT = 512
MOE_BLOCK = 128
EPS = 1e-6

LANES = 128
SUBLANES = 8
VMEM_LIMIT = 56 * 1024 * 1024

TQ = 256
ROUTE_COLS = 8
EXP_LANE0 = N_GROUPS
PACK_ROWS = D_MODEL // 2 // LANES
XG_STRIDE = MOE_BLOCK + SUBLANES
PLAN_STEPS = 3
ROWS_UNROLL = 16
FINAL_TILE = 512
POOL_PAD = 16
CONV_PAD = 32

_MXU = jnp.bfloat16
_F32 = jnp.float32


def _dot(a, b):
    return jnp.dot(a.astype(_MXU), b.astype(_MXU), preferred_element_type=_F32)


def _rms_norm(x, g):
    return x * lax.rsqrt(jnp.mean(x * x, axis=-1, keepdims=True) + EPS) * g


def _layer_norm(x, g, b):
    mu = jnp.mean(x, axis=-1, keepdims=True)
    xc = x - mu
    var = jnp.mean(xc * xc, axis=-1, keepdims=True)
    return xc * lax.rsqrt(var + EPS) * g + b


def _const_spec(shape):
    nd = len(shape)
    return pl.BlockSpec(shape, lambda *_: (0,) * nd, pipeline_mode=pl.Buffered(1))


def _tile_rows_to_tokens(ref, start, n, stride):
    return jnp.concatenate([ref[pl.ds(start + j, n, stride=stride), :] for j in range(SUBLANES)], axis=-1)


def _mixer_input(in_refs, combine, n):
    if not combine:
        return in_refs[0][...]
    h_ref, o2a_ref, o2b_ref, rprev_ref = in_refs
    g0 = rprev_ref[:, 4:5]
    g1 = rprev_ref[:, 5:6]
    o0 = _tile_rows_to_tokens(o2a_ref, 0, n, SUBLANES)
    o1 = _tile_rows_to_tokens(o2b_ref, 0, n, SUBLANES)
    return h_ref[...] + (g0 * o0 + g1 * o1)


def _route(hn, wrt_ref, brt_ref, ltri_ref, cnt_acc, route_ref):
    n = hn.shape[0]
    logits = _dot(hn, wrt_ref[...]) + brt_ref[...]
    lane = lax.broadcasted_iota(jnp.int32, (n, LANES), 1).astype(_F32)
    neg = jnp.float32(-jnp.inf)
    is_grp = lane < N_GROUPS
    gl = jnp.where(is_grp, logits, neg)
    gmax = jnp.max(gl, axis=-1, keepdims=True)
    grp = jnp.min(jnp.where(gl == gmax, lane, float(LANES)), axis=-1, keepdims=True)
    gsum = jnp.sum(jnp.where(is_grp, jnp.exp(logits - gmax), 0.0), axis=-1, keepdims=True)
    grp_w = 1.0 / gsum
    lo = EXP_LANE0 + grp * EPG
    el = jnp.where((lane >= lo) & (lane < lo + EPG), logits, neg)
    v0 = jnp.max(el, axis=-1, keepdims=True)
    i0 = jnp.min(jnp.where(el == v0, lane, float(LANES)), axis=-1, keepdims=True)
    el1 = jnp.where(lane == i0, neg, el)
    v1 = jnp.max(el1, axis=-1, keepdims=True)
    i1 = jnp.min(jnp.where(el1 == v1, lane, float(LANES)), axis=-1, keepdims=True)
    ex = jnp.exp(v1 - v0)
    den = 1.0 + ex
    g0 = grp_w / den
    g1 = grp_w * ex / den
    oh0 = lane == i0
    oh1 = lane == i1
    oh = jnp.where(oh0 | oh1, 1.0, 0.0)
    before = _dot(ltri_ref[...], oh) + cnt_acc[...]
    r0 = jnp.sum(jnp.where(oh0, before, 0.0), axis=-1, keepdims=True)
    r1 = jnp.sum(jnp.where(oh1, before, 0.0), axis=-1, keepdims=True)
    cnt_acc[...] = cnt_acc[...] + jnp.sum(oh, axis=0, keepdims=True)
    route_ref[:, 0:1] = i0 - EXP_LANE0
    route_ref[:, 1:2] = i1 - EXP_LANE0
    route_ref[:, 2:3] = r0
    route_ref[:, 3:4] = r1
    route_ref[:, 4:5] = g0
    route_ref[:, 5:6] = g1
    route_ref[:, 6:8] = jnp.zeros((n, 2), _F32)


def _gate_slab(xn, win_ref, k, nslab):
    width = 3 * D_MODEL // nslab
    lo = 5 * W_BR + k * width
    return _dot(xn, win_ref[:, lo:lo + width])


def _mixer_tail(x, xn, ya, yb, yc, gate_pre, wbp_ref, wbs_ref, wbc_ref, wout_ref, gffn_ref,
                wrt_ref, brt_ref, ltri_ref, cnt_acc, hout_ref, hn_ref, route_ref):
    n = x.shape[0]
    gates = jax.nn.sigmoid(gate_pre)
    merged = gates[:, 0:D_MODEL] * _dot(ya, wbp_ref[...])
    merged = merged + gates[:, D_MODEL:2 * D_MODEL] * _dot(yb, wbs_ref[...])
    merged = merged + gates[:, 2 * D_MODEL:3 * D_MODEL] * _dot(yc, wbc_ref[...])
    h = x + _dot(merged, wout_ref[...])
    hout_ref[...] = h
    hn = _rms_norm(h, gffn_ref[...])
    half = D_MODEL // 2
    lo = lax.bitcast_convert_type(hn[:, :half].astype(_MXU).astype(_F32), jnp.uint32)
    hi = lax.bitcast_convert_type(hn[:, half:].astype(_MXU).astype(_F32), jnp.uint32)
    words = jnp.bitwise_or(jnp.bitwise_and(hi, jnp.uint32(0xFFFF0000)), jnp.right_shift(lo, jnp.uint32(16)))
    for j in range(PACK_ROWS):
        hn_ref[pl.ds(j, n, stride=PACK_ROWS), :] = words[:, j * LANES:(j + 1) * LANES]
    _route(hn, wrt_ref, brt_ref, ltri_ref, cnt_acc, route_ref)


def _mixer_prompt_kernel(*refs, combine, nt, n_main):
    n_in = 4 if combine else 1
    in_refs = refs[:n_in]
    next_refs = refs[n_in:2 * n_in]
    hs_ref, hns_ref, routes_ref = refs[2 * n_in:2 * n_in + 3]
    hout_ref, hn_ref, route_ref = refs[-12:-9]
    step = pl.program_id(0)

    @pl.when(step >= n_main)
    def _():
        hout_ref[...] = hs_ref[...]
        hn_ref[...] = hns_ref[...]
        route_ref[...] = routes_ref[...]

    @pl.when(step < n_main)
    def _():
        _mixer_prompt_step(in_refs, next_refs, refs[2 * n_in + 3:], combine, nt)


def _mixer_prompt_step(first_refs, next_refs, refs, combine, nt):
    (cnt_in_ref, gmix_ref, win_ref, poolw_ref, pscale_ref, sgug_ref, sgub_ref, sguw_ref, sgubias_ref,
     convw_ref, convb_ref, cng_ref, cnb_ref, wbp_ref, wbs_ref, wbc_ref, wout_ref,
     gffn_ref, wrt_ref, brt_ref, ltri_ref,
     hout_ref, hn_ref, route_ref, cnt_ref, pstate_ref, cstate_ref,
     pbuf, cbuf, ybuf, cnt_acc, xbuf, xnbuf) = refs
    step = pl.program_id(0)
    i = lax.rem(step, nt)
    par = lax.rem(step, 2)

    def head(in_refs, slot):
        rows = _mixer_input(in_refs, combine, TQ)
        xbuf[slot] = rows
        xnbuf[slot] = _rms_norm(rows, gmix_ref[...]).astype(_MXU)

    @pl.when(step == 0)
    def _():
        cnt_acc[...] = cnt_in_ref[...]
        head(first_refs, 0)

    @pl.when(i == 0)
    def _():
        pbuf[0:POOL_PAD, :] = jnp.zeros((POOL_PAD, W_BR), _F32)
        cbuf[0, 0:CONV_PAD, :] = jnp.zeros((CONV_PAD, W_BR), _F32)

    xn = xnbuf[par]

    ca = _dot(xn, win_ref[:, 3 * W_BR:4 * W_BR])
    cg = _dot(xn, win_ref[:, 4 * W_BR:5 * W_BR])
    cbuf[0, CONV_PAD:CONV_PAD + TQ, :] = ca * jax.nn.sigmoid(cg)
    for s in range(1, SUBLANES):
        cbuf[s, SUBLANES:CONV_PAD + TQ, :] = cbuf[0, SUBLANES - s:CONV_PAD + TQ - s, :]
    n_lb = W_BR // LANES
    other = [lambda: _dot(xn, win_ref[:, 0:W_BR]), lambda: _dot(xn, win_ref[:, W_BR:2 * W_BR]),
             lambda: _dot(xn, win_ref[:, 2 * W_BR:3 * W_BR]), lambda: _gate_slab(xn, win_ref, 0, 1)]
    done = []
    n_og = TQ // SUBLANES
    pad_g = CONV_PAD // SUBLANES
    n_q = -(-CONV_WIDTH // SUBLANES)
    for lb in range(n_lb):
        sl = slice(lb * LANES, (lb + 1) * LANES)
        taps = [jnp.broadcast_to(convw_ref[CONV_HIST - j:CONV_WIDTH - j, sl], (SUBLANES, LANES))
                for j in range(CONV_WIDTH)]
        bias = jnp.broadcast_to(convb_ref[:, sl], (SUBLANES, LANES))
        open_acc = {}
        for g in range(pad_g - n_q + 1, pad_g + n_og):
            data = [cbuf[s, g * SUBLANES:(g + 1) * SUBLANES, sl] for s in range(SUBLANES)]
            for q in range(n_q):
                o = g - pad_g + q
                if 0 <= o < n_og:
                    acc = open_acc.get(o, bias)
                    for s in range(SUBLANES):
                        if q * SUBLANES + s < CONV_WIDTH:
                            acc = acc + data[s] * taps[q * SUBLANES + s]
                    open_acc[o] = acc
            o = g - pad_g
            if 0 <= o < n_og:
                ybuf[o * SUBLANES:(o + 1) * SUBLANES, sl] = open_acc.pop(o)
        done.append(other[lb]())
    p, u, v, gate_pre = done
    yc = jax.nn.silu(_layer_norm(ybuf[...], cng_ref[...], cnb_ref[...]))
    cstate_ref[0] = cbuf[0, CONV_PAD + TQ - CONV_HIST:CONV_PAD + TQ, :]
    cbuf[0, 0:CONV_PAD, :] = cbuf[0, TQ:TQ + CONV_PAD, :]

    pbuf[POOL_PAD:POOL_PAD + TQ, :] = p
    pos = i * TQ + lax.broadcasted_iota(jnp.int32, (TQ, 1), 0)
    ya_parts = []
    for gi, w in enumerate(POOL_WINDOWS):
        sl = slice(gi * POOL_GC, (gi + 1) * POOL_GC)
        acc = pbuf[:, sl]
        shift = 1
        while shift < w:
            acc = acc + pltpu.roll(acc, shift, axis=0)
            shift *= 2
        acc = acc[POOL_PAD:POOL_PAD + TQ]
        cnt = jnp.minimum(pos + 1, w).astype(_F32)
        d = acc / cnt - p[:, sl]
        ya_parts.append(_dot(d, poolw_ref[gi]))
    ya = jnp.concatenate(ya_parts, axis=-1) * pscale_ref[...]
    pstate_ref[0] = pbuf[POOL_PAD + TQ - POOL_HIST:POOL_PAD + TQ, :]
    pbuf[0:POOL_PAD, :] = pbuf[TQ:TQ + POOL_PAD, :]

    vn = _layer_norm(v, sgug_ref[...], sgub_ref[...])
    rows = []
    for c in range(TQ // CHUNK):
        heads = [_dot(sguw_ref[hd], vn[c * CHUNK:(c + 1) * CHUNK, hd * SGU_HC:(hd + 1) * SGU_HC])
                 for hd in range(SGU_HEADS)]
        rows.append(jnp.concatenate(heads, axis=-1) + sgubias_ref[...])
    yb = u * jnp.concatenate(rows, axis=0)

    _mixer_tail(xbuf[par], xn, ya, yb, yc, gate_pre, wbp_ref, wbs_ref, wbc_ref, wout_ref,
                gffn_ref, wrt_ref, brt_ref, ltri_ref, cnt_acc, hout_ref, hn_ref, route_ref)
    cnt_ref[...] = cnt_acc[...]
    head(next_refs, 1 - par)


def _mixer_sample_kernel(*refs, combine, nseq, nstep):
    n_in = 4 if combine else 1
    in_refs = refs[:n_in]
    (sp_ref, sc_ref, gmix_ref, win_ref, poolw_ref, pscale_ref, sgug_ref, sgub_ref, sguw_ref, sgubias_ref,
     convw_ref, convb_ref, cng_ref, cnb_ref, wbp_ref, wbs_ref, wbc_ref, wout_ref,
     gffn_ref, wrt_ref, brt_ref, ltri_ref,
     hout_ref, hn_ref, route_ref, cnt_ref, pnew_ref, cnew_ref, vn_ref,
     ybuf, cnt_acc) = refs[n_in:]
    n = nseq * nstep
    cnt_acc[...] = jnp.zeros_like(cnt_acc)
    x = _mixer_input(in_refs, combine, n)
    xn = _rms_norm(x, gmix_ref[...]).astype(_MXU)

    def slab(a, t):
        return a[t * nseq:(t + 1) * nseq]

    p = _dot(xn, win_ref[:, 0:W_BR])
    pnew_ref[...] = p
    ya_rows = []
    for t in range(nstep):
        parts = []
        for gi, w in enumerate(POOL_WINDOWS):
            sl = slice(gi * POOL_GC, (gi + 1) * POOL_GC)
            acc = slab(p, t)[:, sl]
            for j in range(1, w):
                k = POOL_HIST + t - j
                acc = acc + (sp_ref[k, :, sl] if k < POOL_HIST else slab(p, k - POOL_HIST)[:, sl])
            cnt = float(min(PAST_LEN + t + 1, w))
            d = acc / cnt - slab(p, t)[:, sl]
            parts.append(_dot(d, poolw_ref[gi]))
        ya_rows.append(jnp.concatenate(parts, axis=-1))
    ya = jnp.concatenate(ya_rows, axis=0) * pscale_ref[...]

    u = _dot(xn, win_ref[:, W_BR:2 * W_BR])
    v = _dot(xn, win_ref[:, 2 * W_BR:3 * W_BR])
    vn = _layer_norm(v, sgug_ref[...], sgub_ref[...])
    vn_ref[...] = vn
    s_rows = []
    for t in range(nstep):
        acc = jnp.broadcast_to(sgubias_ref[t:t + 1, :], (nseq, W_BR))
        for s in range(t + 1):
            acc = acc + sguw_ref[t * nstep + s:t * nstep + s + 1, :] * slab(vn, s)
        s_rows.append(acc)
    yb = u * jnp.concatenate(s_rows, axis=0)

    ca = _dot(xn, win_ref[:, 3 * W_BR:4 * W_BR])
    cg = _dot(xn, win_ref[:, 4 * W_BR:5 * W_BR])
    cnew_ref[...] = ca * jax.nn.sigmoid(cg)
    for t in range(nstep):
        for lb in range(W_BR // LANES):
            sl = slice(lb * LANES, (lb + 1) * LANES)
            acc = jnp.broadcast_to(convb_ref[:, sl], (nseq, LANES))
            for kw in range(CONV_WIDTH):
                k = t + kw
                row = sc_ref[k, :, sl] if k < CONV_HIST else cnew_ref[(k - CONV_HIST) * nseq:(k - CONV_HIST + 1) * nseq, sl]
                acc = acc + row * convw_ref[kw:kw + 1, sl]
            ybuf[t * nseq:(t + 1) * nseq, sl] = acc
    yc = jax.nn.silu(_layer_norm(ybuf[...], cng_ref[...], cnb_ref[...]))

    _mixer_tail(x, xn, ya, yb, yc, _gate_slab(xn, win_ref, 0, 1), wbp_ref, wbs_ref, wbc_ref, wout_ref,
                gffn_ref, wrt_ref, brt_ref, ltri_ref, cnt_acc, hout_ref, hn_ref, route_ref)
    cnt_ref[...] = cnt_acc[...]


def _lane_to_sublane(row):
    sub = lax.broadcasted_iota(jnp.int32, (LANES, LANES), 0)
    lane = lax.broadcasted_iota(jnp.int32, (LANES, LANES), 1)
    return jnp.sum(jnp.where(sub == lane, jnp.broadcast_to(row, (LANES, LANES)), 0.0), axis=1, keepdims=True)


def _plan_kernel(route_ref, cnt_ref, pos_ref, binfo_ref, *, nblk_lanes):
    cnt = cnt_ref[...]
    padded = jnp.floor((cnt + (MOE_BLOCK - 1)) * (1.0 / MOE_BLOCK)) * MOE_BLOCK
    lane1 = lax.broadcasted_iota(jnp.int32, (SUBLANES, LANES), 1)
    pends = jnp.broadcast_to(padded, (SUBLANES, LANES))
    sh = 1
    while sh < LANES:
        pends = pends + jnp.where(lane1 >= sh, pltpu.roll(pends, sh, axis=1), 0.0)
        sh *= 2
    pends = pends[0:1]
    pstart = pends - padded

    n = route_ref.shape[0]
    lane = lax.broadcasted_iota(jnp.int32, (n, LANES), 1).astype(_F32)
    for k in range(2):
        e_lane = route_ref[:, k:k + 1] + EXP_LANE0
        base = jnp.sum(jnp.where(lane == e_lane, pstart, 0.0), axis=-1, keepdims=True)
        pos_ref[:, k:k + 1] = (base + route_ref[:, 2 + k:3 + k]).astype(jnp.int32)

    @pl.when(pl.program_id(0) == 0)
    def _():
        pends_c = _lane_to_sublane(pends)
        pstart_c = _lane_to_sublane(pstart)
        cnt_c = _lane_to_sublane(cnt)
        sub = lax.broadcasted_iota(jnp.int32, (LANES, LANES), 0)
        is_exp = (sub >= EXP_LANE0) & (sub < EXP_LANE0 + N_EXPERTS)
        binfo_ref[...] = jnp.zeros_like(binfo_ref)
        for q in range(nblk_lanes // LANES):
            bstart = ((lax.broadcasted_iota(jnp.int32, (LANES, LANES), 1) + q * LANES) * MOE_BLOCK).astype(_F32)
            done = jnp.where(is_exp & (pends_c <= bstart), 1.0, 0.0)
            bexp = jnp.minimum(jnp.sum(done, axis=0, keepdims=True), float(N_EXPERTS - 1))
            inside = is_exp & (pstart_c <= bstart) & (bstart < pends_c)
            valid = jnp.clip(pstart_c + cnt_c - bstart, 0.0, float(MOE_BLOCK))
            nval = jnp.sum(jnp.where(inside, valid, 0.0), axis=0, keepdims=True)
            binfo_ref[0:1, q * LANES:(q + 1) * LANES] = bexp.astype(jnp.int32)
            binfo_ref[1:2, q * LANES:(q + 1) * LANES] = nval.astype(jnp.int32)


def _rows_kernel(pos0_ref, pos1_ref, src_pad_hbm, dst_pad_hbm, src_ref, dst_ref, sem, *, n_tok):
    unroll = ROWS_UNROLL
    fill_src = pltpu.make_async_copy(src_pad_hbm, src_ref, sem.at[0])
    fill_dst = pltpu.make_async_copy(dst_pad_hbm, dst_ref, sem.at[1])
    fill_src.start()
    fill_dst.start()
    fill_src.wait()
    fill_dst.wait()

    def put(i, c):
        for k in range(unroll):
            t = i * unroll + k
            p0 = pos0_ref[t]
            p1 = pos1_ref[t]
            src_ref[p0] = t
            src_ref[p1] = t
            dst_ref[p0] = t
            dst_ref[p1] = n_tok + t
        return c
    lax.fori_loop(0, n_tok // unroll, put, 0)


def _moe_kernel(be_ref, nv_ref, src_ref, dst_ref, hn_ref, wg_ref, wu_ref, wd_ref, o2_hbm,
                xg, obuf, wg_bf, wu_bf, wd_bf, ssem, *, n_tok):
    b = pl.program_id(0)
    nblk = pl.num_programs(0)
    slot = b % 2
    tile = SUBLANES
    spare0 = 2 * n_tok * tile

    def scatter_copy(row, dst, s):
        return pltpu.make_async_copy(obuf.at[s, pl.ds(row * tile, tile)],
                                     o2_hbm.at[pl.ds(pl.multiple_of(dst * tile, tile), tile)], ssem.at[s])

    def spare_copy(half):
        return pltpu.make_async_copy(obuf.at[1], o2_hbm.at[pl.ds(spare0 + half * MOE_BLOCK * tile, MOE_BLOCK * tile)],
                                     ssem.at[1])

    def gather_block(blk):
        for r in range(MOE_BLOCK):
            tok = src_ref[blk * MOE_BLOCK + r]
            xg[pl.ds(r, PACK_ROWS, stride=XG_STRIDE), :] = hn_ref[pl.ds(pl.multiple_of(tok * PACK_ROWS, PACK_ROWS),
                                                                         PACK_ROWS), :]
        lo, hi = [], []
        for j in range(PACK_ROWS):
            words = xg[j * XG_STRIDE:j * XG_STRIDE + MOE_BLOCK, :]
            lo.append(lax.bitcast_convert_type(jnp.left_shift(words, jnp.uint32(16)), _F32))
            hi.append(lax.bitcast_convert_type(jnp.bitwise_and(words, jnp.uint32(0xFFFF0000)), _F32))
        return jnp.concatenate(lo + hi, axis=-1).astype(_MXU)

    def start_scatter(blk, s):
        for r in range(MOE_BLOCK):
            scatter_copy(r, dst_ref[blk * MOE_BLOCK + r], s).start(priority=r % 2)

    def wait_scatter(s):
        for r in range(MOE_BLOCK):
            scatter_copy(0, 0, s).wait()

    def used(blk):
        return nv_ref[jnp.clip(blk, 0, nblk - 1)] > 0

    @pl.when(b == 0)
    def _():
        obuf[1] = jnp.zeros(obuf.shape[1:], _F32)
        spare_copy(0).start()
        spare_copy(1).start()
        spare_copy(0).wait()
        spare_copy(1).wait()

    @pl.when((b >= 2) & used(b - 2))
    def _():
        wait_scatter(slot)

    @pl.when(used(b))
    def _():
        @pl.when((b == 0) | (be_ref[b] != be_ref[jnp.maximum(b - 1, 0)]))
        def _():
            wg_bf[...] = wg_ref[...].astype(_MXU)
            wu_bf[...] = wu_ref[...].astype(_MXU)
            wd_bf[...] = wd_ref[...].astype(_MXU)

        x = gather_block(b)
        hb = jax.nn.silu(_dot(x, wg_bf[...])) * _dot(x, wu_bf[...])
        o = _dot(hb, wd_bf[...])
        for j in range(SUBLANES):
            obuf[slot, pl.ds(j, MOE_BLOCK, stride=SUBLANES), :] = o[:, j * LANES:(j + 1) * LANES]
        start_scatter(b, slot)

    @pl.when(b == nblk - 1)
    def _():
        @pl.when((b >= 1) & used(b - 1))
        def _():
            wait_scatter(1 - slot)

        @pl.when(used(b))
        def _():
            wait_scatter(slot)


def _final_kernel(h_ref, o2a_ref, o2b_ref, rprev_ref, g_ref, y_ref):
    n = h_ref.shape[0]
    y_ref[...] = _rms_norm(_mixer_input((h_ref, o2a_ref, o2b_ref, rprev_ref), True, n), g_ref[...])


def _stacked_weights(w):
    nstep = w['nstep']
    depth = w['w_in'].shape[0]
    sgu_w = w['sgu_w']
    sgu_b = w['sgu_b']
    vec = lambda a: a[:, None, :]
    pad = jnp.zeros((depth, D_MODEL, LANES - N_GROUPS - N_EXPERTS), _F32)
    w_rt = jnp.concatenate([w['w_group'], w['w_router'], pad], axis=-1)
    b_rt = jnp.concatenate([w['b_group'], w['b_router'], pad[:, 0, :]], axis=-1)
    head = (vec(w['mix_norm_g']), w['w_in'].astype(_MXU), w['pool_w'].astype(_MXU),
            vec(w['pool_scale']), vec(w['sgu_norm_g']), vec(w['sgu_norm_b']))
    tail = (w['conv_w'], vec(w['conv_b']), vec(w['conv_norm_g']), vec(w['conv_norm_b']),
            w['w_branch_pool'].astype(_MXU), w['w_branch_sgu'].astype(_MXU),
            w['w_branch_conv'].astype(_MXU), w['w_out'].astype(_MXU),
            vec(w['ffn_norm_g']), w_rt.astype(_MXU), vec(b_rt))
    sgu_p = (jnp.tril(sgu_w).astype(_MXU), jnp.repeat(sgu_b.transpose(0, 2, 1), SGU_HC, axis=2))
    corner = jnp.tril(sgu_w[:, :, :nstep, :nstep]).transpose(0, 2, 3, 1)
    sgu_s = (jnp.repeat(corner.reshape(depth, nstep * nstep, SGU_HEADS), SGU_HC, axis=2),
             jnp.repeat(sgu_b[:, :, :nstep].transpose(0, 2, 1), SGU_HC, axis=2))
    return head + sgu_p + tail, head + sgu_s + tail


def _weight_specs(ops, l):
    def spec(o):
        nd = o.ndim - 1
        return pl.BlockSpec((None,) + o.shape[1:], lambda *_: (l,) + (0,) * nd, pipeline_mode=pl.Buffered(1))
    return [spec(o) for o in ops]


def _ltri(n):
    r = lax.broadcasted_iota(jnp.int32, (n, n), 0)
    c = lax.broadcasted_iota(jnp.int32, (n, n), 1)
    return (c < r).astype(_MXU)


def _prev_layer_specs(tile, index, n_all):
    def slot1(*ids):
        i, z = index(*ids)
        return i + n_all // tile, z
    return [pl.BlockSpec((tile, D_MODEL), index), pl.BlockSpec((tile * SUBLANES, LANES), index),
            pl.BlockSpec((tile * SUBLANES, LANES), slot1), pl.BlockSpec((tile, ROUTE_COLS), index)]


def _prev_layer_args(prev):
    h, o2, route = prev
    return (h, o2, o2, route)


def _mixer_prompt(prev, x, sample, ops, l, *, nb, seq, n_all):
    combine = prev is not None
    nt = seq // TQ
    n_main = nb * nt
    n_steps = n_all // TQ
    h_s, hn_s, route_s, cnt_s = sample
    row = lambda s: (s, 0)
    first_row = lambda s: (0, 0)
    next_row = lambda s: (jnp.minimum(s + 1, n_main - 1), 0)
    tail_row = lambda s: (jnp.maximum(s - n_main, 0), 0)
    seq_blk = lambda s: (jnp.minimum(s // nt, nb - 1), 0, 0)
    if combine:
        inputs = _prev_layer_args(prev) * 2
        in_specs = _prev_layer_specs(TQ, first_row, n_all) + _prev_layer_specs(TQ, next_row, n_all)
    else:
        inputs = (x, x)
        in_specs = [pl.BlockSpec((TQ, D_MODEL), first_row), pl.BlockSpec((TQ, D_MODEL), next_row)]
    in_specs += [pl.BlockSpec((TQ, D_MODEL), tail_row), pl.BlockSpec((TQ * PACK_ROWS, LANES), tail_row),
                 pl.BlockSpec((TQ, ROUTE_COLS), tail_row), _const_spec(cnt_s.shape)]
    ltri = _ltri(TQ)
    out_shape = (jax.ShapeDtypeStruct((n_all, D_MODEL), _F32),
                 jax.ShapeDtypeStruct((n_all * PACK_ROWS, LANES), jnp.uint32),
                 jax.ShapeDtypeStruct((n_all, ROUTE_COLS), _F32),
                 jax.ShapeDtypeStruct((1, LANES), _F32),
                 jax.ShapeDtypeStruct((nb, POOL_HIST, W_BR), _F32),
                 jax.ShapeDtypeStruct((nb, CONV_HIST, W_BR), _F32))
    out_specs = (pl.BlockSpec((TQ, D_MODEL), row), pl.BlockSpec((TQ * PACK_ROWS, LANES), row),
                 pl.BlockSpec((TQ, ROUTE_COLS), row), pl.BlockSpec((1, LANES), lambda s: (0, 0)),
                 pl.BlockSpec((1, POOL_HIST, W_BR), seq_blk), pl.BlockSpec((1, CONV_HIST, W_BR), seq_blk))
    return pl.pallas_call(
        functools.partial(_mixer_prompt_kernel, combine=combine, nt=nt, n_main=n_main),
        grid=(n_steps,),
        in_specs=in_specs + _weight_specs(ops, l) + [_const_spec(ltri.shape)],
        out_specs=out_specs,
        out_shape=out_shape,
        scratch_shapes=[pltpu.VMEM((POOL_PAD + TQ, W_BR), _F32), pltpu.VMEM((SUBLANES, CONV_PAD + TQ, W_BR), _F32),
                        pltpu.VMEM((TQ, W_BR), _F32), pltpu.VMEM((1, LANES), _F32),
                        pltpu.VMEM((2, TQ, D_MODEL), _F32), pltpu.VMEM((2, TQ, D_MODEL), _MXU)],
        compiler_params=pltpu.CompilerParams(dimension_semantics=("arbitrary",), vmem_limit_bytes=VMEM_LIMIT),
        name="mixer_prompt",
    )(*inputs, h_s, hn_s, route_s, cnt_s, *ops, ltri)


def _mixer_sample(prev, x, sp, sc, ops, l, *, nseq, nstep, n_prompt):
    combine = prev is not None
    n = nseq * nstep
    blk = n_prompt // n
    row = lambda i: (blk, 0)
    zero = lambda i: (0, 0)
    if combine:
        inputs = _prev_layer_args(prev)
        in_specs = _prev_layer_specs(n, row, n_prompt + n)
    else:
        inputs = (x,)
        in_specs = [pl.BlockSpec((n, D_MODEL), zero)]
    in_specs += _weight_specs((sp, sc), l)
    ltri = _ltri(n)
    out_shape = (jax.ShapeDtypeStruct((n, D_MODEL), _F32), jax.ShapeDtypeStruct((n * PACK_ROWS, LANES), jnp.uint32),
                 jax.ShapeDtypeStruct((n, ROUTE_COLS), _F32), jax.ShapeDtypeStruct((1, LANES), _F32),
                 jax.ShapeDtypeStruct((n, W_BR), _F32), jax.ShapeDtypeStruct((n, W_BR), _F32),
                 jax.ShapeDtypeStruct((n, W_BR), _F32))
    out_specs = (pl.BlockSpec((n, D_MODEL), zero), pl.BlockSpec((n * PACK_ROWS, LANES), zero),
                 pl.BlockSpec((n, ROUTE_COLS), zero), pl.BlockSpec((1, LANES), zero),
                 pl.BlockSpec((n, W_BR), zero), pl.BlockSpec((n, W_BR), zero), pl.BlockSpec((n, W_BR), zero))
    return pl.pallas_call(
        functools.partial(_mixer_sample_kernel, combine=combine, nseq=nseq, nstep=nstep),
        grid=(1,),
        in_specs=in_specs + _weight_specs(ops, l) + [_const_spec(ltri.shape)],
        out_specs=out_specs,
        out_shape=out_shape,
        scratch_shapes=[pltpu.VMEM((n, W_BR), _F32), pltpu.VMEM((1, LANES), _F32)],
        compiler_params=pltpu.CompilerParams(dimension_semantics=("arbitrary",), vmem_limit_bytes=VMEM_LIMIT),
        name="mixer_sample",
    )(*inputs, sp, sc, *ops, ltri)


def _plan(route, cnt, *, n_blocks):
    n_all = route.shape[0]
    tile = n_all // PLAN_STEPS
    assert tile * PLAN_STEPS == n_all and tile % SUBLANES == 0
    nblk_lanes = -(-n_blocks // LANES) * LANES
    return pl.pallas_call(
        functools.partial(_plan_kernel, nblk_lanes=nblk_lanes),
        grid=(n_all // tile,),
        in_specs=[pl.BlockSpec((tile, ROUTE_COLS), lambda i: (i, 0)), pl.BlockSpec((1, LANES), lambda i: (0, 0))],
        out_specs=(pl.BlockSpec((tile, 2), lambda i: (i, 0)), pl.BlockSpec((SUBLANES, nblk_lanes), lambda i: (0, 0))),
        out_shape=(jax.ShapeDtypeStruct((n_all, 2), jnp.int32), jax.ShapeDtypeStruct((SUBLANES, nblk_lanes), jnp.int32)),
        compiler_params=pltpu.CompilerParams(dimension_semantics=("arbitrary",)),
        name="moe_plan",
    )(route, cnt)


def _rows(pos0, pos1, *, n_rows):
    n_tok = pos0.shape[0]
    src_pad = jnp.full((n_rows,), n_tok - 1, jnp.int32)
    dst_pad = 2 * n_tok + jnp.bitwise_and(jnp.arange(n_rows, dtype=jnp.int32), 2 * MOE_BLOCK - 1)
    smem = pl.BlockSpec(memory_space=pltpu.SMEM)
    hbm = pl.BlockSpec(memory_space=pl.ANY)
    return pl.pallas_call(
        functools.partial(_rows_kernel, n_tok=n_tok),
        in_specs=[smem, smem, hbm, hbm],
        out_specs=(smem, smem),
        out_shape=(jax.ShapeDtypeStruct((n_rows,), jnp.int32), jax.ShapeDtypeStruct((n_rows,), jnp.int32)),
        scratch_shapes=[pltpu.SemaphoreType.DMA((2,))],
        name="moe_rows",
    )(pos0, pos1, src_pad, dst_pad)


def _moe(be, nv, src, dst, hn_all, wg, wu, wd, l, *, n_blocks):
    n_all = hn_all.shape[0] // PACK_ROWS
    n_tiles = 2 * n_all + 2 * MOE_BLOCK
    wspec_in = pl.BlockSpec((None, None, D_MODEL, D_EXPERT), lambda b, be, *_: (l, be[b], 0, 0))
    wspec_out = pl.BlockSpec((None, None, D_EXPERT, D_MODEL), lambda b, be, *_: (l, be[b], 0, 0))
    grid_spec = pltpu.PrefetchScalarGridSpec(
        num_scalar_prefetch=4,
        grid=(n_blocks,),
        in_specs=[_const_spec(hn_all.shape), wspec_in, wspec_in, wspec_out],
        out_specs=pl.BlockSpec(memory_space=pl.ANY),
        scratch_shapes=[pltpu.VMEM((PACK_ROWS * XG_STRIDE, LANES), jnp.uint32),
                        pltpu.VMEM((2, MOE_BLOCK * SUBLANES, LANES), _F32),
                        pltpu.VMEM((D_MODEL, D_EXPERT), _MXU), pltpu.VMEM((D_MODEL, D_EXPERT), _MXU),
                        pltpu.VMEM((D_EXPERT, D_MODEL), _MXU),
                        pltpu.SemaphoreType.DMA((2,))])
    return pl.pallas_call(
        functools.partial(_moe_kernel, n_tok=n_all),
        grid_spec=grid_spec,
        out_shape=jax.ShapeDtypeStruct((n_tiles * SUBLANES, LANES), _F32),
        compiler_params=pltpu.CompilerParams(dimension_semantics=("arbitrary",), vmem_limit_bytes=VMEM_LIMIT),
        name="moe_experts",
    )(be, nv, src, dst, hn_all, wg, wu, wd)


def _final(prev, g, *, row0, n, tile):
    n_all = prev[0].shape[0]
    blk0 = row0 // tile
    row = lambda i: (blk0 + i, 0)
    return pl.pallas_call(
        _final_kernel,
        grid=(n // tile,),
        in_specs=_prev_layer_specs(tile, row, n_all) + [pl.BlockSpec((1, D_MODEL), lambda i: (0, 0))],
        out_specs=pl.BlockSpec((tile, D_MODEL), lambda i: (i, 0)),
        out_shape=jax.ShapeDtypeStruct((n, D_MODEL), _F32),
        compiler_params=pltpu.CompilerParams(dimension_semantics=("arbitrary",)),
        name="final_norm",
    )(*_prev_layer_args(prev), g)


def kernel(x_prompt, x_sample, state_pool, state_conv, mix_norm_g, w_in, pool_w, pool_scale, sgu_norm_g, sgu_norm_b, sgu_w, sgu_b, conv_w, conv_b, conv_norm_g, conv_norm_b, w_branch_pool, w_branch_sgu, w_branch_conv, w_out, ffn_norm_g, w_group, b_group, w_router, b_router, w_gate_e, w_up_e, w_down_e, final_norm_g):
    nb, seq, _ = x_prompt.shape
    nseq, nstep, _ = x_sample.shape
    n_prompt = nb * seq
    n_sample = nseq * nstep
    n_all = n_prompt + n_sample
    assert seq % TQ == 0 and TQ % CHUNK == 0 and nstep < CHUNK and n_sample % TQ == 0
    assert n_prompt % n_sample == 0 and n_all % FINAL_TILE == 0 and n_sample % MOE_BLOCK == 0
    n_blocks = (2 * n_all) // MOE_BLOCK + N_EXPERTS
    n_rows = n_blocks * MOE_BLOCK

    w = dict(nstep=nstep, mix_norm_g=mix_norm_g, w_in=w_in, pool_w=pool_w, pool_scale=pool_scale,
             sgu_norm_g=sgu_norm_g, sgu_norm_b=sgu_norm_b, sgu_w=sgu_w, sgu_b=sgu_b, conv_w=conv_w, conv_b=conv_b,
             conv_norm_g=conv_norm_g, conv_norm_b=conv_norm_b, w_branch_pool=w_branch_pool,
             w_branch_sgu=w_branch_sgu, w_branch_conv=w_branch_conv, w_out=w_out, ffn_norm_g=ffn_norm_g,
             w_group=w_group, b_group=b_group, w_router=w_router, b_router=b_router)

    xp = x_prompt.reshape(n_prompt, D_MODEL)
    xs = x_sample.transpose(1, 0, 2).reshape(n_sample, D_MODEL)
    sp_t = state_pool.transpose(0, 2, 1, 3)
    sc_t = state_conv.transpose(0, 2, 1, 3)

    pool_p, pool_s, conv_p, conv_s, chunk_v = [], [], [], [], []
    prev = None
    ops_p, ops_s = _stacked_weights(w)
    for l in range(DEPTH):
        h_s, hn_s, route_s, cnt_s, pnew, cnew, vn = _mixer_sample(prev, xs, sp_t, sc_t, ops_s, l, nseq=nseq,
                                                                  nstep=nstep, n_prompt=n_prompt)
        h, hn, route, cnt, pstate, cstate = _mixer_prompt(prev, xp, (h_s, hn_s, route_s, cnt_s), ops_p, l,
                                                          nb=nb, seq=seq, n_all=n_all)
        pos, binfo = _plan(route, cnt, n_blocks=n_blocks)
        src, dst = _rows(pos[:, 0], pos[:, 1], n_rows=n_rows)
        o2 = _moe(binfo[0, :n_blocks], binfo[1, :n_blocks], src, dst, hn, w_gate_e, w_up_e, w_down_e, l,
                  n_blocks=n_blocks)
        prev = (h, o2, route)
        to_seq = lambda a: a.reshape(nstep, nseq, W_BR).transpose(1, 0, 2)
        pool_p.append(pstate)
        conv_p.append(cstate)
        pool_s.append(jnp.concatenate([state_pool[l][:, nstep:], to_seq(pnew)], axis=1))
        conv_s.append(jnp.concatenate([state_conv[l][:, nstep:], to_seq(cnew)], axis=1))
        chunk_v.append(to_seq(vn))

    g = final_norm_g[None]
    y_prompt = _final(prev, g, row0=0, n=n_prompt, tile=FINAL_TILE).reshape(nb, seq, D_MODEL)
    y_sample = _final(prev, g, row0=n_prompt, n=n_sample, tile=n_sample)
    y_sample = y_sample.reshape(nstep, nseq, D_MODEL).transpose(1, 0, 2)
    return (y_prompt, y_sample, jnp.stack(pool_p), jnp.stack(pool_s), jnp.stack(conv_p), jnp.stack(conv_s),
            jnp.stack(chunk_v))
```

```python
import functools

import jax
import jax.numpy as jnp
from jax import lax
from jax.experimental import pallas as pl
from jax.experimental.pallas import tpu as pltpu

D_MODEL = 1024
DEPTH = 4
PAST_LEN = 16384
W_BR = 512
POOL_WINDOWS = (2, 4, 8, 16)
POOL_GC = 128
POOL_HIST = 15
SGU_HEADS = 4
SGU_HC = 128
CHUNK = 128
CONV_WIDTH = 31
CONV_HIST = 30
N_GROUPS = 4
EPG = 8
N_EXPERTS = 32
D_EXPERT = 512
MOE_BLOCK = 128
EPS = 1e-6

LANES = 128
SUBLANES = 8
VMEM_LIMIT = 56 * 1024 * 1024

TQ = 256
ROUTE_COLS = 8
EXP_LANE0 = N_GROUPS
PACK_ROWS = D_MODEL // 2 // LANES
XG_STRIDE = MOE_BLOCK + SUBLANES
PLAN_STEPS = 3
ROWS_UNROLL = 16
FINAL_TILE = 512
POOL_PAD = 16
CONV_PAD = 32

_MXU = jnp.bfloat16
_F32 = jnp.float32


def _dot(a, b):
    return jnp.dot(a.astype(_MXU), b.astype(_MXU), preferred_element_type=_F32)


def _rms_norm(x, g):
    return x * lax.rsqrt(jnp.mean(x * x, axis=-1, keepdims=True) + EPS) * g


def _layer_norm(x, g, b):
    mu = jnp.mean(x, axis=-1, keepdims=True)
    xc = x - mu
    var = jnp.mean(xc * xc, axis=-1, keepdims=True)
    return xc * lax.rsqrt(var + EPS) * g + b


def _const_spec(shape):
    nd = len(shape)
    return pl.BlockSpec(shape, lambda *_: (0,) * nd, pipeline_mode=pl.Buffered(1))


def _tile_rows_to_tokens(ref, start, n, stride):
    return jnp.concatenate([ref[pl.ds(start + j, n, stride=stride), :] for j in range(SUBLANES)], axis=-1)


def _mixer_input(in_refs, combine, n):
    if not combine:
        return in_refs[0][...]
    h_ref, o2a_ref, o2b_ref, rprev_ref = in_refs
    g0 = rprev_ref[:, 4:5]
    g1 = rprev_ref[:, 5:6]
    o0 = _tile_rows_to_tokens(o2a_ref, 0, n, SUBLANES)
    o1 = _tile_rows_to_tokens(o2b_ref, 0, n, SUBLANES)
    return h_ref[...] + (g0 * o0 + g1 * o1)


def _route(hn, wrt_ref, brt_ref, ltri_ref, cnt_acc, route_ref):
    n = hn.shape[0]
    logits = _dot(hn, wrt_ref[...]) + brt_ref[...]
    lane = lax.broadcasted_iota(jnp.int32, (n, LANES), 1).astype(_F32)
    neg = jnp.float32(-jnp.inf)
    is_grp = lane < N_GROUPS
    gl = jnp.where(is_grp, logits, neg)
    gmax = jnp.max(gl, axis=-1, keepdims=True)
    grp = jnp.min(jnp.where(gl == gmax, lane, float(LANES)), axis=-1, keepdims=True)
    gsum = jnp.sum(jnp.where(is_grp, jnp.exp(logits - gmax), 0.0), axis=-1, keepdims=True)
    grp_w = 1.0 / gsum
    lo = EXP_LANE0 + grp * EPG
    el = jnp.where((lane >= lo) & (lane < lo + EPG), logits, neg)
    v0 = jnp.max(el, axis=-1, keepdims=True)
    i0 = jnp.min(jnp.where(el == v0, lane, float(LANES)), axis=-1, keepdims=True)
    el1 = jnp.where(lane == i0, neg, el)
    v1 = jnp.max(el1, axis=-1, keepdims=True)
    i1 = jnp.min(jnp.where(el1 == v1, lane, float(LANES)), axis=-1, keepdims=True)
    ex = jnp.exp(v1 - v0)
    den = 1.0 + ex
    g0 = grp_w / den
    g1 = grp_w * ex / den
    oh0 = lane == i0
    oh1 = lane == i1
    oh = jnp.where(oh0 | oh1, 1.0, 0.0)
    before = _dot(ltri_ref[...], oh) + cnt_acc[...]
    r0 = jnp.sum(jnp.where(oh0, before, 0.0), axis=-1, keepdims=True)
    r1 = jnp.sum(jnp.where(oh1, before, 0.0), axis=-1, keepdims=True)
    cnt_acc[...] = cnt_acc[...] + jnp.sum(oh, axis=0, keepdims=True)
    route_ref[:, 0:1] = i0 - EXP_LANE0
    route_ref[:, 1:2] = i1 - EXP_LANE0
    route_ref[:, 2:3] = r0
    route_ref[:, 3:4] = r1
    route_ref[:, 4:5] = g0
    route_ref[:, 5:6] = g1
    route_ref[:, 6:8] = jnp.zeros((n, 2), _F32)


def _gate_slab(xn, win_ref, k, nslab):
    width = 3 * D_MODEL // nslab
    lo = 5 * W_BR + k * width
    return _dot(xn, win_ref[:, lo:lo + width])


def _mixer_tail(x, xn, ya, yb, yc, gate_pre, wbp_ref, wbs_ref, wbc_ref, wout_ref, gffn_ref,
                wrt_ref, brt_ref, ltri_ref, cnt_acc, hout_ref, hn_ref, route_ref):
    n = x.shape[0]
    gates = jax.nn.sigmoid(gate_pre)
    merged = gates[:, 0:D_MODEL] * _dot(ya, wbp_ref[...])
    merged = merged + gates[:, D_MODEL:2 * D_MODEL] * _dot(yb, wbs_ref[...])
    merged = merged + gates[:, 2 * D_MODEL:3 * D_MODEL] * _dot(yc, wbc_ref[...])
    h = x + _dot(merged, wout_ref[...])
    hout_ref[...] = h
    hn = _rms_norm(h, gffn_ref[...])
    half = D_MODEL // 2
    lo = lax.bitcast_convert_type(hn[:, :half].astype(_MXU).astype(_F32), jnp.uint32)
    hi = lax.bitcast_convert_type(hn[:, half:].astype(_MXU).astype(_F32), jnp.uint32)
    words = jnp.bitwise_or(jnp.bitwise_and(hi, jnp.uint32(0xFFFF0000)), jnp.right_shift(lo, jnp.uint32(16)))
    for j in range(PACK_ROWS):
        hn_ref[pl.ds(j, n, stride=PACK_ROWS), :] = words[:, j * LANES:(j + 1) * LANES]
    _route(hn, wrt_ref, brt_ref, ltri_ref, cnt_acc, route_ref)


def _mixer_prompt_kernel(*refs, combine, nt, n_main):
    n_in = 4 if combine else 1
    in_refs = refs[:n_in]
    next_refs = refs[n_in:2 * n_in]
    hs_ref, hns_ref, routes_ref = refs[2 * n_in:2 * n_in + 3]
    hout_ref, hn_ref, route_ref = refs[-12:-9]
    step = pl.program_id(0)

    @pl.when(step >= n_main)
    def _():
        hout_ref[...] = hs_ref[...]
        hn_ref[...] = hns_ref[...]
        route_ref[...] = routes_ref[...]

    @pl.when(step < n_main)
    def _():
        _mixer_prompt_step(in_refs, next_refs, refs[2 * n_in + 3:], combine, nt)


def _mixer_prompt_step(first_refs, next_refs, refs, combine, nt):
    (cnt_in_ref, gmix_ref, win_ref, poolw_ref, pscale_ref, sgug_ref, sgub_ref, sguw_ref, sgubias_ref,
     convw_ref, convb_ref, cng_ref, cnb_ref, wbp_ref, wbs_ref, wbc_ref, wout_ref,
     gffn_ref, wrt_ref, brt_ref, ltri_ref,
     hout_ref, hn_ref, route_ref, cnt_ref, pstate_ref, cstate_ref,
     pbuf, cbuf, ybuf, cnt_acc, xbuf, xnbuf) = refs
    step = pl.program_id(0)
    i = lax.rem(step, nt)
    par = lax.rem(step, 2)

    def head(in_refs, slot):
        rows = _mixer_input(in_refs, combine, TQ)
        xbuf[slot] = rows
        xnbuf[slot] = _rms_norm(rows, gmix_ref[...]).astype(_MXU)

    @pl.when(step == 0)
    def _():
        cnt_acc[...] = cnt_in_ref[...]
        head(first_refs, 0)

    @pl.when(i == 0)
    def _():
        pbuf[0:POOL_PAD, :] = jnp.zeros((POOL_PAD, W_BR), _F32)
        cbuf[0, 0:CONV_PAD, :] = jnp.zeros((CONV_PAD, W_BR), _F32)

    xn = xnbuf[par]

    ca = _dot(xn, win_ref[:, 3 * W_BR:4 * W_BR])
    cg = _dot(xn, win_ref[:, 4 * W_BR:5 * W_BR])
    cbuf[0, CONV_PAD:CONV_PAD + TQ, :] = ca * jax.nn.sigmoid(cg)
    for s in range(1, SUBLANES):
        cbuf[s, SUBLANES:CONV_PAD + TQ, :] = cbuf[0, SUBLANES - s:CONV_PAD + TQ - s, :]
    n_lb = W_BR // LANES
    other = [lambda: _dot(xn, win_ref[:, 0:W_BR]), lambda: _dot(xn, win_ref[:, W_BR:2 * W_BR]),
             lambda: _dot(xn, win_ref[:, 2 * W_BR:3 * W_BR]), lambda: _gate_slab(xn, win_ref, 0, 1)]
    done = []
    n_og = TQ // SUBLANES
    pad_g = CONV_PAD // SUBLANES
    n_q = -(-CONV_WIDTH // SUBLANES)
    for lb in range(n_lb):
        sl = slice(lb * LANES, (lb + 1) * LANES)
        taps = [jnp.broadcast_to(convw_ref[CONV_HIST - j:CONV_WIDTH - j, sl], (SUBLANES, LANES))
                for j in range(CONV_WIDTH)]
        bias = jnp.broadcast_to(convb_ref[:, sl], (SUBLANES, LANES))
        open_acc = {}
        for g in range(pad_g - n_q + 1, pad_g + n_og):
            data = [cbuf[s, g * SUBLANES:(g + 1) * SUBLANES, sl] for s in range(SUBLANES)]
            for q in range(n_q):
                o = g - pad_g + q
                if 0 <= o < n_og:
                    acc = open_acc.get(o, bias)
                    for s in range(SUBLANES):
                        if q * SUBLANES + s < CONV_WIDTH:
                            acc = acc + data[s] * taps[q * SUBLANES + s]
                    open_acc[o] = acc
            o = g - pad_g
            if 0 <= o < n_og:
                ybuf[o * SUBLANES:(o + 1) * SUBLANES, sl] = open_acc.pop(o)
        done.append(other[lb]())
    p, u, v, gate_pre = done
    yc = jax.nn.silu(_layer_norm(ybuf[...], cng_ref[...], cnb_ref[...]))
    cstate_ref[0] = cbuf[0, CONV_PAD + TQ - CONV_HIST:CONV_PAD + TQ, :]
    cbuf[0, 0:CONV_PAD, :] = cbuf[0, TQ:TQ + CONV_PAD, :]

    pbuf[POOL_PAD:POOL_PAD + TQ, :] = p
    pos = i * TQ + lax.broadcasted_iota(jnp.int32, (TQ, 1), 0)
    ya_parts = []
    for gi, w in enumerate(POOL_WINDOWS):
        sl = slice(gi * POOL_GC, (gi + 1) * POOL_GC)
        acc = pbuf[:, sl]
        shift = 1
        while shift < w:
            acc = acc + pltpu.roll(acc, shift, axis=0)
            shift *= 2
        acc = acc[POOL_PAD:POOL_PAD + TQ]
        cnt = jnp.minimum(pos + 1, w).astype(_F32)
        d = acc / cnt - p[:, sl]
        ya_parts.append(_dot(d, poolw_ref[gi]))
    ya = jnp.concatenate(ya_parts, axis=-1) * pscale_ref[...]
    pstate_ref[0] = pbuf[POOL_PAD + TQ - POOL_HIST:POOL_PAD + TQ, :]
    pbuf[0:POOL_PAD, :] = pbuf[TQ:TQ + POOL_PAD, :]

    vn = _layer_norm(v, sgug_ref[...], sgub_ref[...])
    rows = []
    for c in range(TQ // CHUNK):
        heads = [_dot(sguw_ref[hd], vn[c * CHUNK:(c + 1) * CHUNK, hd * SGU_HC:(hd + 1) * SGU_HC])
                 for hd in range(SGU_HEADS)]
        rows.append(jnp.concatenate(heads, axis=-1) + sgubias_ref[...])
    yb = u * jnp.concatenate(rows, axis=0)

    _mixer_tail(xbuf[par], xn, ya, yb, yc, gate_pre, wbp_ref, wbs_ref, wbc_ref, wout_ref,
                gffn_ref, wrt_ref, brt_ref, ltri_ref, cnt_acc, hout_ref, hn_ref, route_ref)
    cnt_ref[...] = cnt_acc[...]
    head(next_refs, 1 - par)


def _mixer_sample_kernel(*refs, combine, nseq, nstep):
    n_in = 4 if combine else 1
    in_refs = refs[:n_in]
    (sp_ref, sc_ref, gmix_ref, win_ref, poolw_ref, pscale_ref, sgug_ref, sgub_ref, sguw_ref, sgubias_ref,
     convw_ref, convb_ref, cng_ref, cnb_ref, wbp_ref, wbs_ref, wbc_ref, wout_ref,
     gffn_ref, wrt_ref, brt_ref, ltri_ref,
     hout_ref, hn_ref, route_ref, cnt_ref, pnew_ref, cnew_ref, vn_ref,
     ybuf, cnt_acc) = refs[n_in:]
    n = nseq * nstep
    cnt_acc[...] = jnp.zeros_like(cnt_acc)
    x = _mixer_input(in_refs, combine, n)
    xn = _rms_norm(x, gmix_ref[...]).astype(_MXU)

    def slab(a, t):
        return a[t * nseq:(t + 1) * nseq]

    p = _dot(xn, win_ref[:, 0:W_BR])
    pnew_ref[...] = p
    ya_rows = []
    for t in range(nstep):
        parts = []
        for gi, w in enumerate(POOL_WINDOWS):
            sl = slice(gi * POOL_GC, (gi + 1) * POOL_GC)
            acc = slab(p, t)[:, sl]
            for j in range(1, w):
                k = POOL_HIST + t - j
                acc = acc + (sp_ref[k, :, sl] if k < POOL_HIST else slab(p, k - POOL_HIST)[:, sl])
            cnt = float(min(PAST_LEN + t + 1, w))
            d = acc / cnt - slab(p, t)[:, sl]
            parts.append(_dot(d, poolw_ref[gi]))
        ya_rows.append(jnp.concatenate(parts, axis=-1))
    ya = jnp.concatenate(ya_rows, axis=0) * pscale_ref[...]

    u = _dot(xn, win_ref[:, W_BR:2 * W_BR])
    v = _dot(xn, win_ref[:, 2 * W_BR:3 * W_BR])
    vn = _layer_norm(v, sgug_ref[...], sgub_ref[...])
    vn_ref[...] = vn
    s_rows = []
    for t in range(nstep):
        acc = jnp.broadcast_to(sgubias_ref[t:t + 1, :], (nseq, W_BR))
        for s in range(t + 1):
            acc = acc + sguw_ref[t * nstep + s:t * nstep + s + 1, :] * slab(vn, s)
        s_rows.append(acc)
    yb = u * jnp.concatenate(s_rows, axis=0)

    ca = _dot(xn, win_ref[:, 3 * W_BR:4 * W_BR])
    cg = _dot(xn, win_ref[:, 4 * W_BR:5 * W_BR])
    cnew_ref[...] = ca * jax.nn.sigmoid(cg)
    for t in range(nstep):
        for lb in range(W_BR // LANES):
            sl = slice(lb * LANES, (lb + 1) * LANES)
            acc = jnp.broadcast_to(convb_ref[:, sl], (nseq, LANES))
            for kw in range(CONV_WIDTH):
                k = t + kw
                row = sc_ref[k, :, sl] if k < CONV_HIST else cnew_ref[(k - CONV_HIST) * nseq:(k - CONV_HIST + 1) * nseq, sl]
                acc = acc + row * convw_ref[kw:kw + 1, sl]
            ybuf[t * nseq:(t + 1) * nseq, sl] = acc
    yc = jax.nn.silu(_layer_norm(ybuf[...], cng_ref[...], cnb_ref[...]))

    _mixer_tail(x, xn, ya, yb, yc, _gate_slab(xn, win_ref, 0, 1), wbp_ref, wbs_ref, wbc_ref, wout_ref,
                gffn_ref, wrt_ref, brt_ref, ltri_ref, cnt_acc, hout_ref, hn_ref, route_ref)
    cnt_ref[...] = cnt_acc[...]


def _lane_to_sublane(row):
    sub = lax.broadcasted_iota(jnp.int32, (LANES, LANES), 0)
    lane = lax.broadcasted_iota(jnp.int32, (LANES, LANES), 1)
    return jnp.sum(jnp.where(sub == lane, jnp.broadcast_to(row, (LANES, LANES)), 0.0), axis=1, keepdims=True)


def _plan_kernel(route_ref, cnt_ref, pos_ref, binfo_ref, *, nblk_lanes):
    cnt = cnt_ref[...]
    padded = jnp.floor((cnt + (MOE_BLOCK - 1)) * (1.0 / MOE_BLOCK)) * MOE_BLOCK
    lane1 = lax.broadcasted_iota(jnp.int32, (SUBLANES, LANES), 1)
    pends = jnp.broadcast_to(padded, (SUBLANES, LANES))
    sh = 1
    while sh < LANES:
        pends = pends + jnp.where(lane1 >= sh, pltpu.roll(pends, sh, axis=1), 0.0)
        sh *= 2
    pends = pends[0:1]
    pstart = pends - padded

    n = route_ref.shape[0]
    lane = lax.broadcasted_iota(jnp.int32, (n, LANES), 1).astype(_F32)
    for k in range(2):
        e_lane = route_ref[:, k:k + 1] + EXP_LANE0
        base = jnp.sum(jnp.where(lane == e_lane, pstart, 0.0), axis=-1, keepdims=True)
        pos_ref[:, k:k + 1] = (base + route_ref[:, 2 + k:3 + k]).astype(jnp.int32)

    @pl.when(pl.program_id(0) == 0)
    def _():
        pends_c = _lane_to_sublane(pends)
        pstart_c = _lane_to_sublane(pstart)
        cnt_c = _lane_to_sublane(cnt)
        sub = lax.broadcasted_iota(jnp.int32, (LANES, LANES), 0)
        is_exp = (sub >= EXP_LANE0) & (sub < EXP_LANE0 + N_EXPERTS)
        binfo_ref[...] = jnp.zeros_like(binfo_ref)
        for q in range(nblk_lanes // LANES):
            bstart = ((lax.broadcasted_iota(jnp.int32, (LANES, LANES), 1) + q * LANES) * MOE_BLOCK).astype(_F32)
            done = jnp.where(is_exp & (pends_c <= bstart), 1.0, 0.0)
            bexp = jnp.minimum(jnp.sum(done, axis=0, keepdims=True), float(N_EXPERTS - 1))
            inside = is_exp & (pstart_c <= bstart) & (bstart < pends_c)
            valid = jnp.clip(pstart_c + cnt_c - bstart, 0.0, float(MOE_BLOCK))
            nval = jnp.sum(jnp.where(inside, valid, 0.0), axis=0, keepdims=True)
            binfo_ref[0:1, q * LANES:(q + 1) * LANES] = bexp.astype(jnp.int32)
            binfo_ref[1:2, q * LANES:(q + 1) * LANES] = nval.astype(jnp.int32)


def _rows_kernel(pos0_ref, pos1_ref, src_pad_hbm, dst_pad_hbm, src_ref, dst_ref, sem, *, n_tok):
    unroll = ROWS_UNROLL
    fill_src = pltpu.make_async_copy(src_pad_hbm, src_ref, sem.at[0])
    fill_dst = pltpu.make_async_copy(dst_pad_hbm, dst_ref, sem.at[1])
    fill_src.start()
    fill_dst.start()
    fill_src.wait()
    fill_dst.wait()

    def put(i, c):
        for k in range(unroll):
            t = i * unroll + k
            p0 = pos0_ref[t]
            p1 = pos1_ref[t]
            src_ref[p0] = t
            src_ref[p1] = t
            dst_ref[p0] = t
            dst_ref[p1] = n_tok + t
        return c
    lax.fori_loop(0, n_tok // unroll, put, 0)


def _moe_kernel(be_ref, nv_ref, src_ref, dst_ref, hn_ref, wg_ref, wu_ref, wd_ref, o2_hbm,
                xg, obuf, wg_bf, wu_bf, wd_bf, ssem, *, n_tok):
    b = pl.program_id(0)
    nblk = pl.num_programs(0)
    slot = b % 2
    tile = SUBLANES
    spare0 = 2 * n_tok * tile

    def scatter_copy(row, dst, s):
        return pltpu.make_async_copy(obuf.at[s, pl.ds(row * tile, tile)],
                                     o2_hbm.at[pl.ds(pl.multiple_of(dst * tile, tile), tile)], ssem.at[s])

    def spare_copy(half):
        return pltpu.make_async_copy(obuf.at[1], o2_hbm.at[pl.ds(spare0 + half * MOE_BLOCK * tile, MOE_BLOCK * tile)],
                                     ssem.at[1])

    def gather_block(blk):
        for r in range(MOE_BLOCK):
            tok = src_ref[blk * MOE_BLOCK + r]
            xg[pl.ds(r, PACK_ROWS, stride=XG_STRIDE), :] = hn_ref[pl.ds(pl.multiple_of(tok * PACK_ROWS, PACK_ROWS),
                                                                         PACK_ROWS), :]
        lo, hi = [], []
        for j in range(PACK_ROWS):
            words = xg[j * XG_STRIDE:j * XG_STRIDE + MOE_BLOCK, :]
            lo.append(lax.bitcast_convert_type(jnp.left_shift(words, jnp.uint32(16)), _F32))
            hi.append(lax.bitcast_convert_type(jnp.bitwise_and(words, jnp.uint32(0xFFFF0000)), _F32))
        return jnp.concatenate(lo + hi, axis=-1).astype(_MXU)

    def start_scatter(blk, s):
        for r in range(MOE_BLOCK):
            scatter_copy(r, dst_ref[blk * MOE_BLOCK + r], s).start(priority=r % 2)

    def wait_scatter(s):
        for r in range(MOE_BLOCK):
            scatter_copy(0, 0, s).wait()

    def used(blk):
        return nv_ref[jnp.clip(blk, 0, nblk - 1)] > 0

    @pl.when(b == 0)
    def _():
        obuf[1] = jnp.zeros(obuf.shape[1:], _F32)
        spare_copy(0).start()
        spare_copy(1).start()
        spare_copy(0).wait()
        spare_copy(1).wait()

    @pl.when((b >= 2) & used(b - 2))
    def _():
        wait_scatter(slot)

    @pl.when(used(b))
    def _():
        @pl.when((b == 0) | (be_ref[b] != be_ref[jnp.maximum(b - 1, 0)]))
        def _():
            wg_bf[...] = wg_ref[...].astype(_MXU)
            wu_bf[...] = wu_ref[...].astype(_MXU)
            wd_bf[...] = wd_ref[...].astype(_MXU)

        x = gather_block(b)
        hb = jax.nn.silu(_dot(x, wg_bf[...])) * _dot(x, wu_bf[...])
        o = _dot(hb, wd_bf[...])
        for j in range(SUBLANES):
            obuf[slot, pl.ds(j, MOE_BLOCK, stride=SUBLANES), :] = o[:, j * LANES:(j + 1) * LANES]
        start_scatter(b, slot)

    @pl.when(b == nblk - 1)
    def _():
        @pl.when((b >= 1) & used(b - 1))
        def _():
            wait_scatter(1 - slot)

        @pl.when(used(b))
        def _():
            wait_scatter(slot)


def _final_kernel(h_ref, o2a_ref, o2b_ref, rprev_ref, g_ref, y_ref):
    n = h_ref.shape[0]
    y_ref[...] = _rms_norm(_mixer_input((h_ref, o2a_ref, o2b_ref, rprev_ref), True, n), g_ref[...])


def _stacked_weights(w):
    nstep = w['nstep']
    depth = w['w_in'].shape[0]
    sgu_w = w['sgu_w']
    sgu_b = w['sgu_b']
    vec = lambda a: a[:, None, :]
    pad = jnp.zeros((depth, D_MODEL, LANES - N_GROUPS - N_EXPERTS), _F32)
    w_rt = jnp.concatenate([w['w_group'], w['w_router'], pad], axis=-1)
    b_rt = jnp.concatenate([w['b_group'], w['b_router'], pad[:, 0, :]], axis=-1)
    head = (vec(w['mix_norm_g']), w['w_in'].astype(_MXU), w['pool_w'].astype(_MXU),
            vec(w['pool_scale']), vec(w['sgu_norm_g']), vec(w['sgu_norm_b']))
    tail = (w['conv_w'], vec(w['conv_b']), vec(w['conv_norm_g']), vec(w['conv_norm_b']),
            w['w_branch_pool'].astype(_MXU), w['w_branch_sgu'].astype(_MXU),
            w['w_branch_conv'].astype(_MXU), w['w_out'].astype(_MXU),
            vec(w['ffn_norm_g']), w_rt.astype(_MXU), vec(b_rt))
    sgu_p = (jnp.tril(sgu_w).astype(_MXU), jnp.repeat(sgu_b.transpose(0, 2, 1), SGU_HC, axis=2))
    corner = jnp.tril(sgu_w[:, :, :nstep, :nstep]).transpose(0, 2, 3, 1)
    sgu_s = (jnp.repeat(corner.reshape(depth, nstep * nstep, SGU_HEADS), SGU_HC, axis=2),
             jnp.repeat(sgu_b[:, :, :nstep].transpose(0, 2, 1), SGU_HC, axis=2))
    return head + sgu_p + tail, head + sgu_s + tail


def _weight_specs(ops, l):
    def spec(o):
        nd = o.ndim - 1
        return pl.BlockSpec((None,) + o.shape[1:], lambda *_: (l,) + (0,) * nd, pipeline_mode=pl.Buffered(1))
    return [spec(o) for o in ops]


def _ltri(n):
    r = lax.broadcasted_iota(jnp.int32, (n, n), 0)
    c = lax.broadcasted_iota(jnp.int32, (n, n), 1)
    return (c < r).astype(_MXU)


def _prev_layer_specs(tile, index, n_all):
    def slot1(*ids):
        i, z = index(*ids)
        return i + n_all // tile, z
    return [pl.BlockSpec((tile, D_MODEL), index), pl.BlockSpec((tile * SUBLANES, LANES), index),
            pl.BlockSpec((tile * SUBLANES, LANES), slot1), pl.BlockSpec((tile, ROUTE_COLS), index)]


def _prev_layer_args(prev):
    h, o2, route = prev
    return (h, o2, o2, route)


def _mixer_prompt(prev, x, sample, ops, l, *, nb, seq, n_all):
    combine = prev is not None
    nt = seq // TQ
    n_main = nb * nt
    n_steps = n_all // TQ
    h_s, hn_s, route_s, cnt_s = sample
    row = lambda s: (s, 0)
    first_row = lambda s: (0, 0)
    next_row = lambda s: (jnp.minimum(s + 1, n_main - 1), 0)
    tail_row = lambda s: (jnp.maximum(s - n_main, 0), 0)
    seq_blk = lambda s: (jnp.minimum(s // nt, nb - 1), 0, 0)
    if combine:
        inputs = _prev_layer_args(prev) * 2
        in_specs = _prev_layer_specs(TQ, first_row, n_all) + _prev_layer_specs(TQ, next_row, n_all)
    else:
        inputs = (x, x)
        in_specs = [pl.BlockSpec((TQ, D_MODEL), first_row), pl.BlockSpec((TQ, D_MODEL), next_row)]
    in_specs += [pl.BlockSpec((TQ, D_MODEL), tail_row), pl.BlockSpec((TQ * PACK_ROWS, LANES), tail_row),
                 pl.BlockSpec((TQ, ROUTE_COLS), tail_row), _const_spec(cnt_s.shape)]
    ltri = _ltri(TQ)
    out_shape = (jax.ShapeDtypeStruct((n_all, D_MODEL), _F32),
                 jax.ShapeDtypeStruct((n_all * PACK_ROWS, LANES), jnp.uint32),
                 jax.ShapeDtypeStruct((n_all, ROUTE_COLS), _F32),
                 jax.ShapeDtypeStruct((1, LANES), _F32),
                 jax.ShapeDtypeStruct((nb, POOL_HIST, W_BR), _F32),
                 jax.ShapeDtypeStruct((nb, CONV_HIST, W_BR), _F32))
    out_specs = (pl.BlockSpec((TQ, D_MODEL), row), pl.BlockSpec((TQ * PACK_ROWS, LANES), row),
                 pl.BlockSpec((TQ, ROUTE_COLS), row), pl.BlockSpec((1, LANES), lambda s: (0, 0)),
                 pl.BlockSpec((1, POOL_HIST, W_BR), seq_blk), pl.BlockSpec((1, CONV_HIST, W_BR), seq_blk))
    return pl.pallas_call(
        functools.partial(_mixer_prompt_kernel, combine=combine, nt=nt, n_main=n_main),
        grid=(n_steps,),
        in_specs=in_specs + _weight_specs(ops, l) + [_const_spec(ltri.shape)],
        out_specs=out_specs,
        out_shape=out_shape,
        scratch_shapes=[pltpu.VMEM((POOL_PAD + TQ, W_BR), _F32), pltpu.VMEM((SUBLANES, CONV_PAD + TQ, W_BR), _F32),
                        pltpu.VMEM((TQ, W_BR), _F32), pltpu.VMEM((1, LANES), _F32),
                        pltpu.VMEM((2, TQ, D_MODEL), _F32), pltpu.VMEM((2, TQ, D_MODEL), _MXU)],
        compiler_params=pltpu.CompilerParams(dimension_semantics=("arbitrary",), vmem_limit_bytes=VMEM_LIMIT),
        name="mixer_prompt",
    )(*inputs, h_s, hn_s, route_s, cnt_s, *ops, ltri)


def _mixer_sample(prev, x, sp, sc, ops, l, *, nseq, nstep, n_prompt):
    combine = prev is not None
    n = nseq * nstep
    blk = n_prompt // n
    row = lambda i: (blk, 0)
    zero = lambda i: (0, 0)
    if combine:
        inputs = _prev_layer_args(prev)
        in_specs = _prev_layer_specs(n, row, n_prompt + n)
    else:
        inputs = (x,)
        in_specs = [pl.BlockSpec((n, D_MODEL), zero)]
    in_specs += _weight_specs((sp, sc), l)
    ltri = _ltri(n)
    out_shape = (jax.ShapeDtypeStruct((n, D_MODEL), _F32), jax.ShapeDtypeStruct((n * PACK_ROWS, LANES), jnp.uint32),
                 jax.ShapeDtypeStruct((n, ROUTE_COLS), _F32), jax.ShapeDtypeStruct((1, LANES), _F32),
                 jax.ShapeDtypeStruct((n, W_BR), _F32), jax.ShapeDtypeStruct((n, W_BR), _F32),
                 jax.ShapeDtypeStruct((n, W_BR), _F32))
    out_specs = (pl.BlockSpec((n, D_MODEL), zero), pl.BlockSpec((n * PACK_ROWS, LANES), zero),
                 pl.BlockSpec((n, ROUTE_COLS), zero), pl.BlockSpec((1, LANES), zero),
                 pl.BlockSpec((n, W_BR), zero), pl.BlockSpec((n, W_BR), zero), pl.BlockSpec((n, W_BR), zero))
    return pl.pallas_call(
        functools.partial(_mixer_sample_kernel, combine=combine, nseq=nseq, nstep=nstep),
        grid=(1,),
        in_specs=in_specs + _weight_specs(ops, l) + [_const_spec(ltri.shape)],
        out_specs=out_specs,
        out_shape=out_shape,
        scratch_shapes=[pltpu.VMEM((n, W_BR), _F32), pltpu.VMEM((1, LANES), _F32)],
        compiler_params=pltpu.CompilerParams(dimension_semantics=("arbitrary",), vmem_limit_bytes=VMEM_LIMIT),
        name="mixer_sample",
    )(*inputs, sp, sc, *ops, ltri)


def _plan(route, cnt, *, n_blocks):
    n_all = route.shape[0]
    tile = n_all // PLAN_STEPS
    assert tile * PLAN_STEPS == n_all and tile % SUBLANES == 0
    nblk_lanes = -(-n_blocks // LANES) * LANES
    return pl.pallas_call(
        functools.partial(_plan_kernel, nblk_lanes=nblk_lanes),
        grid=(n_all // tile,),
        in_specs=[pl.BlockSpec((tile, ROUTE_COLS), lambda i: (i, 0)), pl.BlockSpec((1, LANES), lambda i: (0, 0))],
        out_specs=(pl.BlockSpec((tile, 2), lambda i: (i, 0)), pl.BlockSpec((SUBLANES, nblk_lanes), lambda i: (0, 0))),
        out_shape=(jax.ShapeDtypeStruct((n_all, 2), jnp.int32), jax.ShapeDtypeStruct((SUBLANES, nblk_lanes), jnp.int32)),
        compiler_params=pltpu.CompilerParams(dimension_semantics=("arbitrary",)),
        name="moe_plan",
    )(route, cnt)


def _rows(pos0, pos1, *, n_rows):
    n_tok = pos0.shape[0]
    src_pad = jnp.full((n_rows,), n_tok - 1, jnp.int32)
    dst_pad = 2 * n_tok + jnp.bitwise_and(jnp.arange(n_rows, dtype=jnp.int32), 2 * MOE_BLOCK - 1)
    smem = pl.BlockSpec(memory_space=pltpu.SMEM)
    hbm = pl.BlockSpec(memory_space=pl.ANY)
    return pl.pallas_call(
        functools.partial(_rows_kernel, n_tok=n_tok),
        in_specs=[smem, smem, hbm, hbm],
        out_specs=(smem, smem),
        out_shape=(jax.ShapeDtypeStruct((n_rows,), jnp.int32), jax.ShapeDtypeStruct((n_rows,), jnp.int32)),
        scratch_shapes=[pltpu.SemaphoreType.DMA((2,))],
        name="moe_rows",
    )(pos0, pos1, src_pad, dst_pad)


def _moe(be, nv, src, dst, hn_all, wg, wu, wd, l, *, n_blocks):
    n_all = hn_all.shape[0] // PACK_ROWS
    n_tiles = 2 * n_all + 2 * MOE_BLOCK
    wspec_in = pl.BlockSpec((None, None, D_MODEL, D_EXPERT), lambda b, be, *_: (l, be[b], 0, 0))
    wspec_out = pl.BlockSpec((None, None, D_EXPERT, D_MODEL), lambda b, be, *_: (l, be[b], 0, 0))
    grid_spec = pltpu.PrefetchScalarGridSpec(
        num_scalar_prefetch=4,
        grid=(n_blocks,),
        in_specs=[_const_spec(hn_all.shape), wspec_in, wspec_in, wspec_out],
        out_specs=pl.BlockSpec(memory_space=pl.ANY),
        scratch_shapes=[pltpu.VMEM((PACK_ROWS * XG_STRIDE, LANES), jnp.uint32),
                        pltpu.VMEM((2, MOE_BLOCK * SUBLANES, LANES), _F32),
                        pltpu.VMEM((D_MODEL, D_EXPERT), _MXU), pltpu.VMEM((D_MODEL, D_EXPERT), _MXU),
                        pltpu.VMEM((D_EXPERT, D_MODEL), _MXU),
                        pltpu.SemaphoreType.DMA((2,))])
    return pl.pallas_call(
        functools.partial(_moe_kernel, n_tok=n_all),
        grid_spec=grid_spec,
        out_shape=jax.ShapeDtypeStruct((n_tiles * SUBLANES, LANES), _F32),
        compiler_params=pltpu.CompilerParams(dimension_semantics=("arbitrary",), vmem_limit_bytes=VMEM_LIMIT),
        name="moe_experts",
    )(be, nv, src, dst, hn_all, wg, wu, wd)


def _final(prev, g, *, row0, n, tile):
    n_all = prev[0].shape[0]
    blk0 = row0 // tile
    row = lambda i: (blk0 + i, 0)
    return pl.pallas_call(
        _final_kernel,
        grid=(n // tile,),
        in_specs=_prev_layer_specs(tile, row, n_all) + [pl.BlockSpec((1, D_MODEL), lambda i: (0, 0))],
        out_specs=pl.BlockSpec((tile, D_MODEL), lambda i: (i, 0)),
        out_shape=jax.ShapeDtypeStruct((n, D_MODEL), _F32),
        compiler_params=pltpu.CompilerParams(dimension_semantics=("arbitrary",)),
        name="final_norm",
    )(*_prev_layer_args(prev), g)


def kernel(x_prompt, x_sample, state_pool, state_conv, mix_norm_g, w_in, pool_w, pool_scale, sgu_norm_g, sgu_norm_b, sgu_w, sgu_b, conv_w, conv_b, conv_norm_g, conv_norm_b, w_branch_pool, w_branch_sgu, w_branch_conv, w_out, ffn_norm_g, w_group, b_group, w_router, b_router, w_gate_e, w_up_e, w_down_e, final_norm_g):
    nb, seq, _ = x_prompt.shape
    nseq, nstep, _ = x_sample.shape
    n_prompt = nb * seq
    n_sample = nseq * nstep
    n_all = n_prompt + n_sample
    assert seq % TQ == 0 and TQ % CHUNK == 0 and nstep < CHUNK and n_sample % TQ == 0
    assert n_prompt % n_sample == 0 and n_all % FINAL_TILE == 0 and n_sample % MOE_BLOCK == 0
    n_blocks = (2 * n_all) // MOE_BLOCK + N_EXPERTS
    n_rows = n_blocks * MOE_BLOCK

    w = dict(nstep=nstep, mix_norm_g=mix_norm_g, w_in=w_in, pool_w=pool_w, pool_scale=pool_scale,
             sgu_norm_g=sgu_norm_g, sgu_norm_b=sgu_norm_b, sgu_w=sgu_w, sgu_b=sgu_b, conv_w=conv_w, conv_b=conv_b,
             conv_norm_g=conv_norm_g, conv_norm_b=conv_norm_b, w_branch_pool=w_branch_pool,
             w_branch_sgu=w_branch_sgu, w_branch_conv=w_branch_conv, w_out=w_out, ffn_norm_g=ffn_norm_g,
             w_group=w_group, b_group=b_group, w_router=w_router, b_router=b_router)

    xp = x_prompt.reshape(n_prompt, D_MODEL)
    xs = x_sample.transpose(1, 0, 2).reshape(n_sample, D_MODEL)
    sp_t = state_pool.transpose(0, 2, 1, 3)
    sc_t = state_conv.transpose(0, 2, 1, 3)

    pool_p, pool_s, conv_p, conv_s, chunk_v = [], [], [], [], []
    prev = None
    ops_p, ops_s = _stacked_weights(w)
    for l in range(DEPTH):
        h_s, hn_s, route_s, cnt_s, pnew, cnew, vn = _mixer_sample(prev, xs, sp_t, sc_t, ops_s, l, nseq=nseq,
                                                                  nstep=nstep, n_prompt=n_prompt)
        h, hn, route, cnt, pstate, cstate = _mixer_prompt(prev, xp, (h_s, hn_s, route_s, cnt_s), ops_p, l,
                                                          nb=nb, seq=seq, n_all=n_all)
        pos, binfo = _plan(route, cnt, n_blocks=n_blocks)
        src, dst = _rows(pos[:, 0], pos[:, 1], n_rows=n_rows)
        o2 = _moe(binfo[0, :n_blocks], binfo[1, :n_blocks], src, dst, hn, w_gate_e, w_up_e, w_down_e, l,
                  n_blocks=n_blocks)
        prev = (h, o2, route)
        pool_p.append(pstate)
        conv_p.append(cstate)
        pool_s.append(pnew)
        conv_s.append(cnew)
        chunk_v.append(vn)

    g = final_norm_g[None]
    y_prompt = _final(prev, g, row0=0, n=n_prompt, tile=FINAL_TILE).reshape(nb, seq, D_MODEL)
    y_sample = _final(prev, g, row0=n_prompt, n=n_sample, tile=n_sample)
    y_sample = y_sample.reshape(nstep, nseq, D_MODEL).transpose(1, 0, 2)
    to_seq = lambda per_layer: jnp.stack(per_layer).reshape(DEPTH, nstep, nseq, W_BR).transpose(0, 2, 1, 3)
    pool_sample = jnp.concatenate([state_pool[:, :, nstep:], to_seq(pool_s)], axis=2)
    conv_sample = jnp.concatenate([state_conv[:, :, nstep:], to_seq(conv_s)], axis=2)
    return (y_prompt, y_sample, jnp.stack(pool_p), pool_sample, jnp.stack(conv_p), conv_sample, to_seq(chunk_v))
```

```python
import functools

import jax
import jax.numpy as jnp
from jax import lax
from jax.experimental import pallas as pl
from jax.experimental.pallas import tpu as pltpu

D_MODEL = 1024
DEPTH = 4
PAST_LEN = 16384
W_BR = 512
POOL_WINDOWS = (2, 4, 8, 16)
POOL_GC = 128
POOL_HIST = 15
SGU_HEADS = 4
SGU_HC = 128
CHUNK = 128
CONV_WIDTH = 31
CONV_HIST = 30
N_GROUPS = 4
EPG = 8
N_EXPERTS = 32
D_EXPERT = 512
MOE_BLOCK = 128
EPS = 1e-6

LANES = 128
SUBLANES = 8
VMEM_LIMIT = 56 * 1024 * 1024

TQ = 256
ROUTE_COLS = 8
EXP_LANE0 = N_GROUPS
PACK_ROWS = D_MODEL // 2 // LANES
XG_STRIDE = MOE_BLOCK + SUBLANES
PLAN_STEPS = 3
ROWS_UNROLL = 16
FINAL_TILE = 512
POOL_PAD = 16
CONV_PAD = 32

_MXU = jnp.bfloat16
_F32 = jnp.float32


def _dot(a, b):
    return jnp.dot(a.astype(_MXU), b.astype(_MXU), preferred_element_type=_F32)


def _rms_norm(x, g):
    return x * lax.rsqrt(jnp.mean(x * x, axis=-1, keepdims=True) + EPS) * g


def _layer_norm(x, g, b):
    mu = jnp.mean(x, axis=-1, keepdims=True)
    xc = x - mu
    var = jnp.mean(xc * xc, axis=-1, keepdims=True)
    return xc * lax.rsqrt(var + EPS) * g + b


def _const_spec(shape):
    nd = len(shape)
    return pl.BlockSpec(shape, lambda *_: (0,) * nd, pipeline_mode=pl.Buffered(1))


def _tile_rows_to_tokens(ref, start, n, stride):
    return jnp.concatenate([ref[pl.ds(start + j, n, stride=stride), :] for j in range(SUBLANES)], axis=-1)


def _mixer_input(in_refs, combine, n):
    if not combine:
        return in_refs[0][...]
    h_ref, o2a_ref, o2b_ref, rprev_ref = in_refs
    g0 = rprev_ref[:, 4:5]
    g1 = rprev_ref[:, 5:6]
    o0 = _tile_rows_to_tokens(o2a_ref, 0, n, SUBLANES)
    o1 = _tile_rows_to_tokens(o2b_ref, 0, n, SUBLANES)
    return h_ref[...] + (g0 * o0 + g1 * o1)


def _route(hn, wrt_ref, brt_ref, ltri_ref, cnt_acc, route_ref):
    n = hn.shape[0]
    logits = _dot(hn, wrt_ref[...]) + brt_ref[...]
    lane = lax.broadcasted_iota(jnp.int32, (n, LANES), 1).astype(_F32)
    neg = jnp.float32(-jnp.inf)
    is_grp = lane < N_GROUPS
    gl = jnp.where(is_grp, logits, neg)
    gmax = jnp.max(gl, axis=-1, keepdims=True)
    grp = jnp.min(jnp.where(gl == gmax, lane, float(LANES)), axis=-1, keepdims=True)
    gsum = jnp.sum(jnp.where(is_grp, jnp.exp(logits - gmax), 0.0), axis=-1, keepdims=True)
    grp_w = 1.0 / gsum
    lo = EXP_LANE0 + grp * EPG
    el = jnp.where((lane >= lo) & (lane < lo + EPG), logits, neg)
    v0 = jnp.max(el, axis=-1, keepdims=True)
    i0 = jnp.min(jnp.where(el == v0, lane, float(LANES)), axis=-1, keepdims=True)
    el1 = jnp.where(lane == i0, neg, el)
    v1 = jnp.max(el1, axis=-1, keepdims=True)
    i1 = jnp.min(jnp.where(el1 == v1, lane, float(LANES)), axis=-1, keepdims=True)
    ex = jnp.exp(v1 - v0)
    den = 1.0 + ex
    g0 = grp_w / den
    g1 = grp_w * ex / den
    oh0 = lane == i0
    oh1 = lane == i1
    oh = jnp.where(oh0 | oh1, 1.0, 0.0)
    before = _dot(ltri_ref[...], oh) + cnt_acc[...]
    r0 = jnp.sum(jnp.where(oh0, before, 0.0), axis=-1, keepdims=True)
    r1 = jnp.sum(jnp.where(oh1, before, 0.0), axis=-1, keepdims=True)
    cnt_acc[...] = cnt_acc[...] + jnp.sum(oh, axis=0, keepdims=True)
    route_ref[:, 0:1] = i0 - EXP_LANE0
    route_ref[:, 1:2] = i1 - EXP_LANE0
    route_ref[:, 2:3] = r0
    route_ref[:, 3:4] = r1
    route_ref[:, 4:5] = g0
    route_ref[:, 5:6] = g1
    route_ref[:, 6:8] = jnp.zeros((n, 2), _F32)


def _gate_slab(xn, win_ref, k, nslab):
    width = 3 * D_MODEL // nslab
    lo = 5 * W_BR + k * width
    return _dot(xn, win_ref[:, lo:lo + width])


def _mixer_tail(x, xn, ya, yb, yc, gate_pre, wbp_ref, wbs_ref, wbc_ref, wout_ref, gffn_ref,
                wrt_ref, brt_ref, ltri_ref, cnt_acc, hout_ref, hn_ref, route_ref):
    n = x.shape[0]
    gates = jax.nn.sigmoid(gate_pre)
    merged = gates[:, 0:D_MODEL] * _dot(ya, wbp_ref[...])
    merged = merged + gates[:, D_MODEL:2 * D_MODEL] * _dot(yb, wbs_ref[...])
    merged = merged + gates[:, 2 * D_MODEL:3 * D_MODEL] * _dot(yc, wbc_ref[...])
    h = x + _dot(merged, wout_ref[...])
    hout_ref[...] = h
    hn = _rms_norm(h, gffn_ref[...])
    half = D_MODEL // 2
    lo = lax.bitcast_convert_type(hn[:, :half].astype(_MXU).astype(_F32), jnp.uint32)
    hi = lax.bitcast_convert_type(hn[:, half:].astype(_MXU).astype(_F32), jnp.uint32)
    words = jnp.bitwise_or(jnp.bitwise_and(hi, jnp.uint32(0xFFFF0000)), jnp.right_shift(lo, jnp.uint32(16)))
    for j in range(PACK_ROWS):
        hn_ref[pl.ds(j, n, stride=PACK_ROWS), :] = words[:, j * LANES:(j + 1) * LANES]
    _route(hn, wrt_ref, brt_ref, ltri_ref, cnt_acc, route_ref)


def _mixer_prompt_kernel(*refs, combine, nt, n_main):
    n_in = 4 if combine else 1
    in_refs = refs[:n_in]
    next_refs = refs[n_in:2 * n_in]
    hs_ref, hns_ref, routes_ref = refs[2 * n_in:2 * n_in + 3]
    hout_ref, hn_ref, route_ref = refs[-12:-9]
    step = pl.program_id(0)

    @pl.when(step >= n_main)
    def _():
        hout_ref[...] = hs_ref[...]
        hn_ref[...] = hns_ref[...]
        route_ref[...] = routes_ref[...]

    @pl.when(step < n_main)
    def _():
        _mixer_prompt_step(in_refs, next_refs, refs[2 * n_in + 3:], combine, nt)


def _mixer_prompt_step(first_refs, next_refs, refs, combine, nt):
    (cnt_in_ref, gmix_ref, win_ref, poolw_ref, pscale_ref, sgug_ref, sgub_ref, sguw_ref, sgubias_ref,
     convw_ref, convb_ref, cng_ref, cnb_ref, wbp_ref, wbs_ref, wbc_ref, wout_ref,
     gffn_ref, wrt_ref, brt_ref, ltri_ref,
     hout_ref, hn_ref, route_ref, cnt_ref, pstate_ref, cstate_ref,
     pbuf, cbuf, ybuf, cnt_acc, xbuf, xnbuf) = refs
    step = pl.program_id(0)
    i = lax.rem(step, nt)
    par = lax.rem(step, 2)

    def head(in_refs, slot):
        rows = _mixer_input(in_refs, combine, TQ)
        xbuf[slot] = rows
        xnbuf[slot] = _rms_norm(rows, gmix_ref[...]).astype(_MXU)

    @pl.when(step == 0)
    def _():
        cnt_acc[...] = cnt_in_ref[...]
        head(first_refs, 0)

    @pl.when(i == 0)
    def _():
        pbuf[0:POOL_PAD, :] = jnp.zeros((POOL_PAD, W_BR), _F32)
        cbuf[0, 0:CONV_PAD, :] = jnp.zeros((CONV_PAD, W_BR), _F32)

    xn = xnbuf[par]

    ca = _dot(xn, win_ref[:, 3 * W_BR:4 * W_BR])
    cg = _dot(xn, win_ref[:, 4 * W_BR:5 * W_BR])
    cbuf[0, CONV_PAD:CONV_PAD + TQ, :] = ca * jax.nn.sigmoid(cg)
    for s in range(1, SUBLANES):
        cbuf[s, SUBLANES:CONV_PAD + TQ, :] = cbuf[0, SUBLANES - s:CONV_PAD + TQ - s, :]
    n_lb = W_BR // LANES
    other = [lambda: _dot(xn, win_ref[:, 0:W_BR]), lambda: _dot(xn, win_ref[:, W_BR:2 * W_BR]),
             lambda: _dot(xn, win_ref[:, 2 * W_BR:3 * W_BR]), lambda: _gate_slab(xn, win_ref, 0, 1)]
    done = []
    n_og = TQ // SUBLANES
    pad_g = CONV_PAD // SUBLANES
    n_q = -(-CONV_WIDTH // SUBLANES)
    for lb in range(n_lb):
        sl = slice(lb * LANES, (lb + 1) * LANES)
        taps = [jnp.broadcast_to(convw_ref[CONV_HIST - j:CONV_WIDTH - j, sl], (SUBLANES, LANES))
                for j in range(CONV_WIDTH)]
        bias = jnp.broadcast_to(convb_ref[:, sl], (SUBLANES, LANES))
        open_acc = {}
        for g in range(pad_g - n_q + 1, pad_g + n_og):
            data = [cbuf[s, g * SUBLANES:(g + 1) * SUBLANES, sl] for s in range(SUBLANES)]
            for q in range(n_q):
                o = g - pad_g + q
                if 0 <= o < n_og:
                    acc = open_acc.get(o, bias)
                    for s in range(SUBLANES):
                        if q * SUBLANES + s < CONV_WIDTH:
                            acc = acc + data[s] * taps[q * SUBLANES + s]
                    open_acc[o] = acc
            o = g - pad_g
            if 0 <= o < n_og:
                ybuf[o * SUBLANES:(o + 1) * SUBLANES, sl] = open_acc.pop(o)
        done.append(other[lb]())
    p, u, v, gate_pre = done
    yc = jax.nn.silu(_layer_norm(ybuf[...], cng_ref[...], cnb_ref[...]))
    cstate_ref[0] = cbuf[0, CONV_PAD + TQ - CONV_HIST:CONV_PAD + TQ, :]
    cbuf[0, 0:CONV_PAD, :] = cbuf[0, TQ:TQ + CONV_PAD, :]

    pbuf[POOL_PAD:POOL_PAD + TQ, :] = p
    pos = i * TQ + lax.broadcasted_iota(jnp.int32, (TQ, 1), 0)
    ya_parts = []
    for gi, w in enumerate(POOL_WINDOWS):
        sl = slice(gi * POOL_GC, (gi + 1) * POOL_GC)
        acc = pbuf[:, sl]
        shift = 1
        while shift < w:
            acc = acc + pltpu.roll(acc, shift, axis=0)
            shift *= 2
        acc = acc[POOL_PAD:POOL_PAD + TQ]
        cnt = jnp.minimum(pos + 1, w).astype(_F32)
        d = acc / cnt - p[:, sl]
        ya_parts.append(_dot(d, poolw_ref[gi]))
    ya = jnp.concatenate(ya_parts, axis=-1) * pscale_ref[...]
    pstate_ref[0] = pbuf[POOL_PAD + TQ - POOL_HIST:POOL_PAD + TQ, :]
    pbuf[0:POOL_PAD, :] = pbuf[TQ:TQ + POOL_PAD, :]

    vn = _layer_norm(v, sgug_ref[...], sgub_ref[...])
    rows = []
    for c in range(TQ // CHUNK):
        heads = [_dot(sguw_ref[hd], vn[c * CHUNK:(c + 1) * CHUNK, hd * SGU_HC:(hd + 1) * SGU_HC])
                 for hd in range(SGU_HEADS)]
        rows.append(jnp.concatenate(heads, axis=-1) + sgubias_ref[...])
    yb = u * jnp.concatenate(rows, axis=0)

    _mixer_tail(xbuf[par], xn, ya, yb, yc, gate_pre, wbp_ref, wbs_ref, wbc_ref, wout_ref,
                gffn_ref, wrt_ref, brt_ref, ltri_ref, cnt_acc, hout_ref, hn_ref, route_ref)
    cnt_ref[...] = cnt_acc[...]
    head(next_refs, 1 - par)


def _mixer_sample_kernel(*refs, combine, nseq, nstep):
    n_in = 4 if combine else 1
    in_refs = refs[:n_in]
    (sp_ref, sc_ref, gmix_ref, win_ref, poolw_ref, pscale_ref, sgug_ref, sgub_ref, sguw_ref, sgubias_ref,
     convw_ref, convb_ref, cng_ref, cnb_ref, wbp_ref, wbs_ref, wbc_ref, wout_ref,
     gffn_ref, wrt_ref, brt_ref, ltri_ref,
     hout_ref, hn_ref, route_ref, cnt_ref, pnew_ref, cnew_ref, vn_ref,
     ybuf, cnt_acc) = refs[n_in:]
    n = nseq * nstep
    cnt_acc[...] = jnp.zeros_like(cnt_acc)
    x = _mixer_input(in_refs, combine, n)
    xn = _rms_norm(x, gmix_ref[...]).astype(_MXU)

    def slab(a, t):
        return a[t * nseq:(t + 1) * nseq]

    p = _dot(xn, win_ref[:, 0:W_BR])
    pnew_ref[...] = p
    ya_rows = []
    for t in range(nstep):
        parts = []
        for gi, w in enumerate(POOL_WINDOWS):
            sl = slice(gi * POOL_GC, (gi + 1) * POOL_GC)
            acc = slab(p, t)[:, sl]
            for j in range(1, w):
                k = POOL_HIST + t - j
                acc = acc + (sp_ref[k, :, sl] if k < POOL_HIST else slab(p, k - POOL_HIST)[:, sl])
            cnt = float(min(PAST_LEN + t + 1, w))
            d = acc / cnt - slab(p, t)[:, sl]
            parts.append(_dot(d, poolw_ref[gi]))
        ya_rows.append(jnp.concatenate(parts, axis=-1))
    ya = jnp.concatenate(ya_rows, axis=0) * pscale_ref[...]

    u = _dot(xn, win_ref[:, W_BR:2 * W_BR])
    v = _dot(xn, win_ref[:, 2 * W_BR:3 * W_BR])
    vn = _layer_norm(v, sgug_ref[...], sgub_ref[...])
    vn_ref[...] = vn
    s_rows = []
    for t in range(nstep):
        acc = jnp.broadcast_to(sgubias_ref[t:t + 1, :], (nseq, W_BR))
        for s in range(t + 1):
            acc = acc + sguw_ref[t * nstep + s:t * nstep + s + 1, :] * slab(vn, s)
        s_rows.append(acc)
    yb = u * jnp.concatenate(s_rows, axis=0)

    ca = _dot(xn, win_ref[:, 3 * W_BR:4 * W_BR])
    cg = _dot(xn, win_ref[:, 4 * W_BR:5 * W_BR])
    cnew_ref[...] = ca * jax.nn.sigmoid(cg)
    for t in range(nstep):
        for lb in range(W_BR // LANES):
            sl = slice(lb * LANES, (lb + 1) * LANES)
            acc = jnp.broadcast_to(convb_ref[:, sl], (nseq, LANES))
            for kw in range(CONV_WIDTH):
                k = t + kw
                row = sc_ref[k, :, sl] if k < CONV_HIST else cnew_ref[(k - CONV_HIST) * nseq:(k - CONV_HIST + 1) * nseq, sl]
                acc = acc + row * convw_ref[kw:kw + 1, sl]
            ybuf[t * nseq:(t + 1) * nseq, sl] = acc
    yc = jax.nn.silu(_layer_norm(ybuf[...], cng_ref[...], cnb_ref[...]))

    _mixer_tail(x, xn, ya, yb, yc, _gate_slab(xn, win_ref, 0, 1), wbp_ref, wbs_ref, wbc_ref, wout_ref,
                gffn_ref, wrt_ref, brt_ref, ltri_ref, cnt_acc, hout_ref, hn_ref, route_ref)
    cnt_ref[...] = cnt_acc[...]


def _lane_to_sublane(row):
    sub = lax.broadcasted_iota(jnp.int32, (LANES, LANES), 0)
    lane = lax.broadcasted_iota(jnp.int32, (LANES, LANES), 1)
    return jnp.sum(jnp.where(sub == lane, jnp.broadcast_to(row, (LANES, LANES)), 0.0), axis=1, keepdims=True)


def _plan_kernel(route_ref, cnt_ref, pos_ref, binfo_ref, *, nblk_lanes):
    cnt = cnt_ref[...]
    padded = jnp.floor((cnt + (MOE_BLOCK - 1)) * (1.0 / MOE_BLOCK)) * MOE_BLOCK
    lane1 = lax.broadcasted_iota(jnp.int32, (SUBLANES, LANES), 1)
    pends = jnp.broadcast_to(padded, (SUBLANES, LANES))
    sh = 1
    while sh < LANES:
        pends = pends + jnp.where(lane1 >= sh, pltpu.roll(pends, sh, axis=1), 0.0)
        sh *= 2
    pends = pends[0:1]
    pstart = pends - padded

    n = route_ref.shape[0]
    lane = lax.broadcasted_iota(jnp.int32, (n, LANES), 1).astype(_F32)
    for k in range(2):
        e_lane = route_ref[:, k:k + 1] + EXP_LANE0
        base = jnp.sum(jnp.where(lane == e_lane, pstart, 0.0), axis=-1, keepdims=True)
        pos_ref[:, k:k + 1] = (base + route_ref[:, 2 + k:3 + k]).astype(jnp.int32)

    @pl.when(pl.program_id(0) == 0)
    def _():
        pends_c = _lane_to_sublane(pends)
        pstart_c = _lane_to_sublane(pstart)
        cnt_c = _lane_to_sublane(cnt)
        sub = lax.broadcasted_iota(jnp.int32, (LANES, LANES), 0)
        is_exp = (sub >= EXP_LANE0) & (sub < EXP_LANE0 + N_EXPERTS)
        binfo_ref[...] = jnp.zeros_like(binfo_ref)
        for q in range(nblk_lanes // LANES):
            bstart = ((lax.broadcasted_iota(jnp.int32, (LANES, LANES), 1) + q * LANES) * MOE_BLOCK).astype(_F32)
            done = jnp.where(is_exp & (pends_c <= bstart), 1.0, 0.0)
            bexp = jnp.minimum(jnp.sum(done, axis=0, keepdims=True), float(N_EXPERTS - 1))
            inside = is_exp & (pstart_c <= bstart) & (bstart < pends_c)
            valid = jnp.clip(pstart_c + cnt_c - bstart, 0.0, float(MOE_BLOCK))
            nval = jnp.sum(jnp.where(inside, valid, 0.0), axis=0, keepdims=True)
            binfo_ref[0:1, q * LANES:(q + 1) * LANES] = bexp.astype(jnp.int32)
            binfo_ref[1:2, q * LANES:(q + 1) * LANES] = nval.astype(jnp.int32)


def _rows_kernel(pos0_ref, pos1_ref, src_pad_hbm, dst_pad_hbm, src_ref, dst_ref, sem, *, n_tok):
    unroll = ROWS_UNROLL
    fill_src = pltpu.make_async_copy(src_pad_hbm, src_ref, sem.at[0])
    fill_dst = pltpu.make_async_copy(dst_pad_hbm, dst_ref, sem.at[1])
    fill_src.start()
    fill_dst.start()
    fill_src.wait()
    fill_dst.wait()

    def put(i, c):
        for k in range(unroll):
            t = i * unroll + k
            p0 = pos0_ref[t]
            p1 = pos1_ref[t]
            src_ref[p0] = t
            src_ref[p1] = t
            dst_ref[p0] = t
            dst_ref[p1] = n_tok + t
        return c
    lax.fori_loop(0, n_tok // unroll, put, 0)


def _moe_kernel(be_ref, nv_ref, src_ref, dst_ref, hn_ref, wg_ref, wu_ref, wd_ref, o2_hbm,
                xg, obuf, wg_bf, wu_bf, wd_bf, ssem, *, n_tok):
    b = pl.program_id(0)
    nblk = pl.num_programs(0)
    slot = b % 2
    tile = SUBLANES
    spare0 = 2 * n_tok * tile

    def scatter_copy(row, dst, s):
        return pltpu.make_async_copy(obuf.at[s, pl.ds(row * tile, tile)],
                                     o2_hbm.at[pl.ds(pl.multiple_of(dst * tile, tile), tile)], ssem.at[s])

    def spare_copy(half):
        return pltpu.make_async_copy(obuf.at[1], o2_hbm.at[pl.ds(spare0 + half * MOE_BLOCK * tile, MOE_BLOCK * tile)],
                                     ssem.at[1])

    def gather_block(blk):
        for r in range(MOE_BLOCK):
            tok = src_ref[blk * MOE_BLOCK + r]
            xg[pl.ds(r, PACK_ROWS, stride=XG_STRIDE), :] = hn_ref[pl.ds(pl.multiple_of(tok * PACK_ROWS, PACK_ROWS),
                                                                         PACK_ROWS), :]
        lo, hi = [], []
        for j in range(PACK_ROWS):
            words = xg[j * XG_STRIDE:j * XG_STRIDE + MOE_BLOCK, :]
            lo.append(lax.bitcast_convert_type(jnp.left_shift(words, jnp.uint32(16)), _F32))
            hi.append(lax.bitcast_convert_type(jnp.bitwise_and(words, jnp.uint32(0xFFFF0000)), _F32))
        return jnp.concatenate(lo + hi, axis=-1).astype(_MXU)

    def start_scatter(blk, s):
        for r in range(MOE_BLOCK):
            scatter_copy(r, dst_ref[blk * MOE_BLOCK + r], s).start(priority=r % 2)

    def wait_scatter(s):
        for r in range(MOE_BLOCK):
            scatter_copy(0, 0, s).wait()

    def used(blk):
        return nv_ref[jnp.clip(blk, 0, nblk - 1)] > 0

    @pl.when(b == 0)
    def _():
        obuf[1] = jnp.zeros(obuf.shape[1:], _F32)
        spare_copy(0).start()
        spare_copy(1).start()
        spare_copy(0).wait()
        spare_copy(1).wait()

    @pl.when((b >= 2) & used(b - 2))
    def _():
        wait_scatter(slot)

    @pl.when(used(b))
    def _():
        @pl.when((b == 0) | (be_ref[b] != be_ref[jnp.maximum(b - 1, 0)]))
        def _():
            wg_bf[...] = wg_ref[...].astype(_MXU)
            wu_bf[...] = wu_ref[...].astype(_MXU)
            wd_bf[...] = wd_ref[...].astype(_MXU)

        x = gather_block(b)
        hb = jax.nn.silu(_dot(x, wg_bf[...])) * _dot(x, wu_bf[...])
        o = _dot(hb, wd_bf[...])
        for j in range(SUBLANES):
            obuf[slot, pl.ds(j, MOE_BLOCK, stride=SUBLANES), :] = o[:, j * LANES:(j + 1) * LANES]
        start_scatter(b, slot)

    @pl.when(b == nblk - 1)
    def _():
        @pl.when((b >= 1) & used(b - 1))
        def _():
            wait_scatter(1 - slot)

        @pl.when(used(b))
        def _():
            wait_scatter(slot)


def _final_kernel(h_ref, o2a_ref, o2b_ref, rprev_ref, g_ref, y_ref):
    n = h_ref.shape[0]
    y_ref[...] = _rms_norm(_mixer_input((h_ref, o2a_ref, o2b_ref, rprev_ref), True, n), g_ref[...])


def _stacked_weights(w):
    nstep = w['nstep']
    depth = w['w_in'].shape[0]
    sgu_w = w['sgu_w']
    sgu_b = w['sgu_b']
    vec = lambda a: a[:, None, :]
    pad = jnp.zeros((depth, D_MODEL, LANES - N_GROUPS - N_EXPERTS), _F32)
    w_rt = jnp.concatenate([w['w_group'], w['w_router'], pad], axis=-1)
    b_rt = jnp.concatenate([w['b_group'], w['b_router'], pad[:, 0, :]], axis=-1)
    head = (vec(w['mix_norm_g']), w['w_in'].astype(_MXU), w['pool_w'].astype(_MXU),
            vec(w['pool_scale']), vec(w['sgu_norm_g']), vec(w['sgu_norm_b']))
    tail = (w['conv_w'], vec(w['conv_b']), vec(w['conv_norm_g']), vec(w['conv_norm_b']),
            w['w_branch_pool'].astype(_MXU), w['w_branch_sgu'].astype(_MXU),
            w['w_branch_conv'].astype(_MXU), w['w_out'].astype(_MXU),
            vec(w['ffn_norm_g']), w_rt.astype(_MXU), vec(b_rt))
    sgu_p = (jnp.tril(sgu_w).astype(_MXU), jnp.repeat(sgu_b.transpose(0, 2, 1), SGU_HC, axis=2))
    corner = jnp.tril(sgu_w[:, :, :nstep, :nstep]).transpose(0, 2, 3, 1)
    sgu_s = (jnp.repeat(corner.reshape(depth, nstep * nstep, SGU_HEADS), SGU_HC, axis=2),
             jnp.repeat(sgu_b[:, :, :nstep].transpose(0, 2, 1), SGU_HC, axis=2))
    return head + sgu_p + tail, head + sgu_s + tail


def _weight_specs(ops, l):
    def spec(o):
        nd = o.ndim - 1
        return pl.BlockSpec((None,) + o.shape[1:], lambda *_: (l,) + (0,) * nd, pipeline_mode=pl.Buffered(1))
    return [spec(o) for o in ops]


def _ltri(n):
    r = lax.broadcasted_iota(jnp.int32, (n, n), 0)
    c = lax.broadcasted_iota(jnp.int32, (n, n), 1)
    return (c < r).astype(_MXU)


def _prev_layer_specs(tile, index, n_all):
    def slot1(*ids):
        i, z = index(*ids)
        return i + n_all // tile, z
    return [pl.BlockSpec((tile, D_MODEL), index), pl.BlockSpec((tile * SUBLANES, LANES), index),
            pl.BlockSpec((tile * SUBLANES, LANES), slot1), pl.BlockSpec((tile, ROUTE_COLS), index)]


def _prev_layer_args(prev):
    h, o2, route = prev
    return (h, o2, o2, route)


def _mixer_prompt(prev, x, sample, ops, l, *, nb, seq, n_all):
    combine = prev is not None
    nt = seq // TQ
    n_main = nb * nt
    n_steps = n_all // TQ
    h_s, hn_s, route_s, cnt_s = sample
    row = lambda s: (s, 0)
    first_row = lambda s: (0, 0)
    next_row = lambda s: (jnp.minimum(s + 1, n_main - 1), 0)
    tail_row = lambda s: (jnp.maximum(s - n_main, 0), 0)
    seq_blk = lambda s: (jnp.minimum(s // nt, nb - 1), 0, 0)
    if combine:
        inputs = _prev_layer_args(prev) * 2
        in_specs = _prev_layer_specs(TQ, first_row, n_all) + _prev_layer_specs(TQ, next_row, n_all)
    else:
        inputs = (x, x)
        in_specs = [pl.BlockSpec((TQ, D_MODEL), first_row), pl.BlockSpec((TQ, D_MODEL), next_row)]
    in_specs += [pl.BlockSpec((TQ, D_MODEL), tail_row), pl.BlockSpec((TQ * PACK_ROWS, LANES), tail_row),
                 pl.BlockSpec((TQ, ROUTE_COLS), tail_row), _const_spec(cnt_s.shape)]
    ltri = _ltri(TQ)
    out_shape = (jax.ShapeDtypeStruct((n_all, D_MODEL), _F32),
                 jax.ShapeDtypeStruct((n_all * PACK_ROWS, LANES), jnp.uint32),
                 jax.ShapeDtypeStruct((n_all, ROUTE_COLS), _F32),
                 jax.ShapeDtypeStruct((1, LANES), _F32),
                 jax.ShapeDtypeStruct((nb, POOL_HIST, W_BR), _F32),
                 jax.ShapeDtypeStruct((nb, CONV_HIST, W_BR), _F32))
    out_specs = (pl.BlockSpec((TQ, D_MODEL), row), pl.BlockSpec((TQ * PACK_ROWS, LANES), row),
                 pl.BlockSpec((TQ, ROUTE_COLS), row), pl.BlockSpec((1, LANES), lambda s: (0, 0)),
                 pl.BlockSpec((1, POOL_HIST, W_BR), seq_blk), pl.BlockSpec((1, CONV_HIST, W_BR), seq_blk))
    return pl.pallas_call(
        functools.partial(_mixer_prompt_kernel, combine=combine, nt=nt, n_main=n_main),
        grid=(n_steps,),
        in_specs=in_specs + _weight_specs(ops, l) + [_const_spec(ltri.shape)],
        out_specs=out_specs,
        out_shape=out_shape,
        scratch_shapes=[pltpu.VMEM((POOL_PAD + TQ, W_BR), _F32), pltpu.VMEM((SUBLANES, CONV_PAD + TQ, W_BR), _F32),
                        pltpu.VMEM((TQ, W_BR), _F32), pltpu.VMEM((1, LANES), _F32),
                        pltpu.VMEM((2, TQ, D_MODEL), _F32), pltpu.VMEM((2, TQ, D_MODEL), _MXU)],
        compiler_params=pltpu.CompilerParams(dimension_semantics=("arbitrary",), vmem_limit_bytes=VMEM_LIMIT),
        name="mixer_prompt",
    )(*inputs, h_s, hn_s, route_s, cnt_s, *ops, ltri)


def _mixer_sample(prev, x, sp, sc, ops, l, *, nseq, nstep, n_prompt):
    combine = prev is not None
    n = nseq * nstep
    blk = n_prompt // n
    row = lambda i: (blk, 0)
    zero = lambda i: (0, 0)
    if combine:
        inputs = _prev_layer_args(prev)
        in_specs = _prev_layer_specs(n, row, n_prompt + n)
    else:
        inputs = (x,)
        in_specs = [pl.BlockSpec((n, D_MODEL), zero)]
    in_specs += _weight_specs((sp, sc), l)
    ltri = _ltri(n)
    out_shape = (jax.ShapeDtypeStruct((n, D_MODEL), _F32), jax.ShapeDtypeStruct((n * PACK_ROWS, LANES), jnp.uint32),
                 jax.ShapeDtypeStruct((n, ROUTE_COLS), _F32), jax.ShapeDtypeStruct((1, LANES), _F32),
                 jax.ShapeDtypeStruct((n, W_BR), _F32), jax.ShapeDtypeStruct((n, W_BR), _F32),
                 jax.ShapeDtypeStruct((n, W_BR), _F32))
    out_specs = (pl.BlockSpec((n, D_MODEL), zero), pl.BlockSpec((n * PACK_ROWS, LANES), zero),
                 pl.BlockSpec((n, ROUTE_COLS), zero), pl.BlockSpec((1, LANES), zero),
                 pl.BlockSpec((n, W_BR), zero), pl.BlockSpec((n, W_BR), zero), pl.BlockSpec((n, W_BR), zero))
    return pl.pallas_call(
        functools.partial(_mixer_sample_kernel, combine=combine, nseq=nseq, nstep=nstep),
        grid=(1,),
        in_specs=in_specs + _weight_specs(ops, l) + [_const_spec(ltri.shape)],
        out_specs=out_specs,
        out_shape=out_shape,
        scratch_shapes=[pltpu.VMEM((n, W_BR), _F32), pltpu.VMEM((1, LANES), _F32)],
        compiler_params=pltpu.CompilerParams(dimension_semantics=("arbitrary",), vmem_limit_bytes=VMEM_LIMIT),
        name="mixer_sample",
    )(*inputs, sp, sc, *ops, ltri)


def _plan(route, cnt, *, n_blocks):
    n_all = route.shape[0]
    tile = n_all // PLAN_STEPS
    assert tile * PLAN_STEPS == n_all and tile % SUBLANES == 0
    nblk_lanes = -(-n_blocks // LANES) * LANES
    return pl.pallas_call(
        functools.partial(_plan_kernel, nblk_lanes=nblk_lanes),
        grid=(n_all // tile,),
        in_specs=[pl.BlockSpec((tile, ROUTE_COLS), lambda i: (i, 0)), pl.BlockSpec((1, LANES), lambda i: (0, 0))],
        out_specs=(pl.BlockSpec((tile, 2), lambda i: (i, 0)), pl.BlockSpec((SUBLANES, nblk_lanes), lambda i: (0, 0))),
        out_shape=(jax.ShapeDtypeStruct((n_all, 2), jnp.int32), jax.ShapeDtypeStruct((SUBLANES, nblk_lanes), jnp.int32)),
        compiler_params=pltpu.CompilerParams(dimension_semantics=("arbitrary",)),
        name="moe_plan",
    )(route, cnt)


def _rows(pos0, pos1, *, n_rows):
    n_tok = pos0.shape[0]
    src_pad = jnp.full((n_rows,), n_tok - 1, jnp.int32)
    dst_pad = 2 * n_tok + jnp.bitwise_and(jnp.arange(n_rows, dtype=jnp.int32), 2 * MOE_BLOCK - 1)
    smem = pl.BlockSpec(memory_space=pltpu.SMEM)
    hbm = pl.BlockSpec(memory_space=pl.ANY)
    return pl.pallas_call(
        functools.partial(_rows_kernel, n_tok=n_tok),
        in_specs=[smem, smem, hbm, hbm],
        out_specs=(smem, smem),
        out_shape=(jax.ShapeDtypeStruct((n_rows,), jnp.int32), jax.ShapeDtypeStruct((n_rows,), jnp.int32)),
        scratch_shapes=[pltpu.SemaphoreType.DMA((2,))],
        name="moe_rows",
    )(pos0, pos1, src_pad, dst_pad)


def _moe(be, nv, src, dst, hn_all, wg, wu, wd, l, *, n_blocks):
    n_all = hn_all.shape[0] // PACK_ROWS
    n_tiles = 2 * n_all + 2 * MOE_BLOCK
    wspec_in = pl.BlockSpec((None, None, D_MODEL, D_EXPERT), lambda b, be, *_: (l, be[b], 0, 0))
    wspec_out = pl.BlockSpec((None, None, D_EXPERT, D_MODEL), lambda b, be, *_: (l, be[b], 0, 0))
    grid_spec = pltpu.PrefetchScalarGridSpec(
        num_scalar_prefetch=4,
        grid=(n_blocks,),
        in_specs=[_const_spec(hn_all.shape), wspec_in, wspec_in, wspec_out],
        out_specs=pl.BlockSpec(memory_space=pl.ANY),
        scratch_shapes=[pltpu.VMEM((PACK_ROWS * XG_STRIDE, LANES), jnp.uint32),
                        pltpu.VMEM((2, MOE_BLOCK * SUBLANES, LANES), _F32),
                        pltpu.VMEM((D_MODEL, D_EXPERT), _MXU), pltpu.VMEM((D_MODEL, D_EXPERT), _MXU),
                        pltpu.VMEM((D_EXPERT, D_MODEL), _MXU),
                        pltpu.SemaphoreType.DMA((2,))])
    return pl.pallas_call(
        functools.partial(_moe_kernel, n_tok=n_all),
        grid_spec=grid_spec,
        out_shape=jax.ShapeDtypeStruct((n_tiles * SUBLANES, LANES), _F32),
        compiler_params=pltpu.CompilerParams(dimension_semantics=("arbitrary",), vmem_limit_bytes=VMEM_LIMIT),
        name="moe_experts",
    )(be, nv, src, dst, hn_all, wg, wu, wd)


def _final(prev, g, *, row0, n, tile):
    n_all = prev[0].shape[0]
    blk0 = row0 // tile
    row = lambda i: (blk0 + i, 0)
    return pl.pallas_call(
        _final_kernel,
        grid=(n // tile,),
        in_specs=_prev_layer_specs(tile, row, n_all) + [pl.BlockSpec((1, D_MODEL), lambda i: (0, 0))],
        out_specs=pl.BlockSpec((tile, D_MODEL), lambda i: (i, 0)),
        out_shape=jax.ShapeDtypeStruct((n, D_MODEL), _F32),
        compiler_params=pltpu.CompilerParams(dimension_semantics=("arbitrary",)),
        name="final_norm",
    )(*_prev_layer_args(prev), g)


def kernel(x_prompt, x_sample, state_pool, state_conv, mix_norm_g, w_in, pool_w, pool_scale, sgu_norm_g, sgu_norm_b, sgu_w, sgu_b, conv_w, conv_b, conv_norm_g, conv_norm_b, w_branch_pool, w_branch_sgu, w_branch_conv, w_out, ffn_norm_g, w_group, b_group, w_router, b_router, w_gate_e, w_up_e, w_down_e, final_norm_g):
    nb, seq, _ = x_prompt.shape
    nseq, nstep, _ = x_sample.shape
    n_prompt = nb * seq
    n_sample = nseq * nstep
    n_all = n_prompt + n_sample
    assert seq % TQ == 0 and TQ % CHUNK == 0 and nstep < CHUNK and n_sample % TQ == 0
    assert n_prompt % n_sample == 0 and n_all % FINAL_TILE == 0 and n_sample % MOE_BLOCK == 0
    n_blocks = (2 * n_all) // MOE_BLOCK + N_EXPERTS
    n_rows = n_blocks * MOE_BLOCK

    w = dict(nstep=nstep, mix_norm_g=mix_norm_g, w_in=w_in, pool_w=pool_w, pool_scale=pool_scale,
             sgu_norm_g=sgu_norm_g, sgu_norm_b=sgu_norm_b, sgu_w=sgu_w, sgu_b=sgu_b, conv_w=conv_w, conv_b=conv_b,
             conv_norm_g=conv_norm_g, conv_norm_b=conv_norm_b, w_branch_pool=w_branch_pool,
             w_branch_sgu=w_branch_sgu, w_branch_conv=w_branch_conv, w_out=w_out, ffn_norm_g=ffn_norm_g,
             w_group=w_group, b_group=b_group, w_router=w_router, b_router=b_router)

    xp = x_prompt.reshape(n_prompt, D_MODEL)
    xs = x_sample.transpose(1, 0, 2).reshape(n_sample, D_MODEL)
    sp_t = state_pool.transpose(0, 2, 1, 3)
    sc_t = state_conv.transpose(0, 2, 1, 3)

    pool_p, pool_s, conv_p, conv_s, chunk_v = [], [], [], [], []
    prev = None
    ops_p, ops_s = _stacked_weights(w)
    for l in range(DEPTH):
        h_s, hn_s, route_s, cnt_s, pnew, cnew, vn = _mixer_sample(prev, xs, sp_t, sc_t, ops_s, l, nseq=nseq,
                                                                  nstep=nstep, n_prompt=n_prompt)
        h, hn, route, cnt, pstate, cstate = _mixer_prompt(prev, xp, (h_s, hn_s, route_s, cnt_s), ops_p, l,
                                                          nb=nb, seq=seq, n_all=n_all)
        pos, binfo = _plan(route, cnt, n_blocks=n_blocks)
        src, dst = _rows(pos[:, 0], pos[:, 1], n_rows=n_rows)
        o2 = _moe(binfo[0, :n_blocks], binfo[1, :n_blocks], src, dst, hn, w_gate_e, w_up_e, w_down_e, l,
                  n_blocks=n_blocks)
        prev = (h, o2, route)
        pool_p.append(pstate)
        conv_p.append(cstate)
        pool_s.append(pnew)
        conv_s.append(cnew)
        chunk_v.append(vn)

    g = final_norm_g[None]
    y_prompt = _final(prev, g, row0=0, n=n_prompt, tile=FINAL_TILE).reshape(nb, seq, D_MODEL)
    y_sample = _final(prev, g, row0=n_prompt, n=n_sample, tile=n_sample)
    y_sample = y_sample.reshape(nstep, nseq, D_MODEL).transpose(1, 0, 2)
    to_seq = lambda per_layer: jnp.stack(per_layer).reshape(DEPTH, nstep, nseq, W_BR).transpose(0, 2, 1, 3)
    def shifted_state(state, new_rows):
        hist = state.shape[2]
        old = lax.pad(state, jnp.zeros((), state.dtype), [(0, 0, 0), (0, 0, 0), (-nstep, nstep, 0), (0, 0, 0)])
        new = lax.pad(new_rows, jnp.zeros((), state.dtype), [(0, 0, 0), (0, 0, 0), (hist - nstep, 0, 0), (0, 0, 0)])
        row = lax.broadcasted_iota(jnp.int32, state.shape, 2)
        return jnp.where(row < hist - nstep, old, new)

    pool_sample = shifted_state(state_pool, to_seq(pool_s))
    conv_sample = shifted_state(state_conv, to_seq(conv_s))
    return (y_prompt, y_sample, jnp.stack(pool_p), pool_sample, jnp.stack(conv_p), conv_sample, to_seq(chunk_v))
```
